```python
import math
import jax, jax.numpy as jnp
from jax import lax
import numpy as np

D_MODEL = 1024
BATCH = 4
SEQ = 4096
DEPTH = 4
DEC_BATCH = 16
DEC_SEQ = 64
PAST_LEN = 1024

CHUNK = 64
N_MIXERS = 3
N_RG = (DEPTH + 2) // 3
N_SWA = (DEPTH + 1) // 3
N_FOX = DEPTH // 3
D_FF = 4 * D_MODEL
HEAD_DIM = 64
N_HEADS = D_MODEL // HEAD_DIM
SWA_KV_HEADS = 4
SWA_GROUP = N_HEADS // SWA_KV_HEADS
WINDOW = 128
WIN_CHUNKS = WINDOW // CHUNK
FOX_Q_BLOCK = 128
LRU_WIDTH = D_MODEL
LRU_BLOCKS = 4
LRU_BLOCK_W = LRU_WIDTH // LRU_BLOCKS
CONV_WIDTH = 4
LRU_C = 8.0
N_BUCKETS = 32
MAX_DISTANCE = 128
ALPHA = (2.0 * DEPTH) ** 0.25
BETA = (8.0 * DEPTH) ** -0.25
LN_EPS = 1e-5
ATTN_SCALE = HEAD_DIM ** -0.5

kernel_name = "hybrid_streaming_encoder_step"


def layer_norm(x, g, b):
    xf = x.astype(jnp.float32)
    mu = jnp.mean(xf, axis=-1, keepdims=True)
    var = jnp.mean(jnp.square(xf - mu), axis=-1, keepdims=True)
    return ((xf - mu) * lax.rsqrt(var + LN_EPS) * g.astype(jnp.float32) + b.astype(jnp.float32)).astype(x.dtype)


def sq_relu_mlp(x, w_up, w_down):
    return jnp.square(jax.nn.relu(x @ w_up)) @ w_down


def _linear_combine(e1, e2):
    a1, b1 = e1
    a2, b2 = e2
    return a1 * a2, a2 * b1 + b2


def rglru_mixer(x, conv_buf, h0, w_in, conv_w, conv_b, gate_w, gate_b, lam, w_out):
    B, T, _ = x.shape
    gate, u = jnp.split(x @ w_in, 2, axis=-1)
    u_pad = jnp.concatenate([conv_buf.astype(u.dtype), u], axis=1)
    conv = conv_b + u_pad[:, 0:T] * conv_w[0]
    for k in range(1, CONV_WIDTH):
        conv = conv + u_pad[:, k:k + T] * conv_w[k]
    new_buf = u_pad[:, T:]
    gates = jnp.einsum('btnd,nde->btne', conv.reshape(B, T, LRU_BLOCKS, LRU_BLOCK_W), gate_w) + gate_b
    gates = jax.nn.sigmoid(gates.astype(jnp.float32))
    r = gates[..., :LRU_BLOCK_W].reshape(B, T, LRU_WIDTH)
    i_g = gates[..., LRU_BLOCK_W:].reshape(B, T, LRU_WIDTH)
    log_a = -LRU_C * r * jax.nn.softplus(-lam.astype(jnp.float32))
    a = jnp.exp(log_a)
    b = jnp.sqrt(-jnp.expm1(2.0 * log_a)) * (i_g * conv.astype(jnp.float32))
    a_cum, b_cum = lax.associative_scan(_linear_combine, (a, b), axis=1)
    h = a_cum * h0.astype(jnp.float32)[:, None, :] + b_cum
    y = (h.astype(x.dtype) * jax.nn.gelu(gate)) @ w_out
    return y, new_buf, h[:, -1].astype(x.dtype)


def t5_relative_bias(rel, table):
    half = N_BUCKETS // 2
    max_exact = half // 2
    n = jnp.abs(rel)
    n_f = jnp.maximum(n, 1).astype(jnp.float32)
    large = max_exact + (jnp.log(n_f / max_exact) / math.log(MAX_DISTANCE / max_exact) * (half - max_exact)).astype(jnp.int32)
    large = jnp.minimum(large, half - 1)
    bucket = jnp.where(rel > 0, half, 0) + jnp.where(n < max_exact, n, large)
    return jnp.transpose(table[bucket].astype(jnp.float32), (2, 0, 1))


def swa_project(x, w_qkv):
    B, T, _ = x.shape
    hq = N_HEADS * HEAD_DIM
    hk = SWA_KV_HEADS * HEAD_DIM
    qkv = x @ w_qkv
    q = qkv[..., :hq].reshape(B, T, SWA_KV_HEADS, SWA_GROUP, HEAD_DIM)
    k = qkv[..., hq:hq + hk].reshape(B, T, SWA_KV_HEADS, HEAD_DIM)
    v = qkv[..., hq + hk:].reshape(B, T, SWA_KV_HEADS, HEAD_DIM)
    return q, k, v


def swa_attend(q, k, v, key_valid, bias, sinks):
    s = jnp.einsum('bnqhgd,bnshd->bnhgqs', q, k).astype(jnp.float32) * ATTN_SCALE
    s = s + bias.reshape(SWA_KV_HEADS, SWA_GROUP, *bias.shape[1:])
    s = jnp.where(key_valid[None, :, None, None, None, :], s, -jnp.inf)
    sink = sinks.astype(jnp.float32).reshape(SWA_KV_HEADS, SWA_GROUP)[:, :, None, None]
    m = jnp.maximum(jnp.max(s, axis=-1, keepdims=True), sink)
    p = jnp.exp(s - m)
    denom = jnp.sum(p, axis=-1, keepdims=True) + jnp.exp(sink - m)
    return jnp.einsum('bnhgqs,bnshd->bnqhgd', (p / denom).astype(v.dtype), v)


def swa_prompt(x, w_qkv, sinks, w_out, table):
    B, T, _ = x.shape
    nc = T // CHUNK
    hist = WIN_CHUNKS * CHUNK
    q, k, v = swa_project(x, w_qkv)
    q = q.reshape(B, nc, CHUNK, SWA_KV_HEADS, SWA_GROUP, HEAD_DIM)

    def band(t):
        tp = jnp.pad(t, ((0, 0), (hist, 0), (0, 0), (0, 0))).reshape(B, nc + WIN_CHUNKS, CHUNK, SWA_KV_HEADS, HEAD_DIM)
        return jnp.concatenate([tp[:, c:c + nc] for c in range(WIN_CHUNKS + 1)], axis=2)

    n_keys = (WIN_CHUNKS + 1) * CHUNK
    key_pos = jnp.arange(nc)[:, None] * CHUNK - hist + jnp.arange(n_keys)[None, :]
    rel = jnp.arange(n_keys)[None, :] - hist - jnp.arange(CHUNK)[:, None]
    o = swa_attend(q, band(k), band(v), key_pos >= 0, t5_relative_bias(rel, table), sinks)
    y = o.reshape(B, T, N_HEADS * HEAD_DIM) @ w_out
    return y, k[:, -WINDOW:], v[:, -WINDOW:]


def swa_sample(x, k_cache, v_cache, w_qkv, sinks, w_out, table):
    B, T, _ = x.shape
    hist = k_cache.shape[1]
    q, k, v = swa_project(x, w_qkv)
    k_all = jnp.concatenate([k_cache.astype(k.dtype), k], axis=1)
    v_all = jnp.concatenate([v_cache.astype(v.dtype), v], axis=1)
    rel = jnp.arange(hist + T)[None, :] - hist - jnp.arange(T)[:, None]
    valid = jnp.ones((1, hist + T), dtype=bool)
    o = swa_attend(q[:, None], k_all[:, None], v_all[:, None], valid, t5_relative_bias(rel, table), sinks)
    y = o.reshape(B, T, N_HEADS * HEAD_DIM) @ w_out
    return y, k_all[:, -hist:], v_all[:, -hist:]


def fox_project(x, w_in, b_f):
    B, T, _ = x.shape
    hd = N_HEADS * HEAD_DIM
    proj = x @ w_in
    q = proj[..., :hd].reshape(B, T, N_HEADS, HEAD_DIM)
    k = proj[..., hd:2 * hd].reshape(B, T, N_HEADS, HEAD_DIM)
    v = proj[..., 2 * hd:3 * hd].reshape(B, T, N_HEADS, HEAD_DIM)
    logf = jax.nn.log_sigmoid(proj[..., 3 * hd:].astype(jnp.float32) + b_f.astype(jnp.float32))
    return q, k, v, logf


def fox_attend(q, cq, qpos, k, v, ck, kpos):
    s = jnp.einsum('bqhd,bshd->bhqs', q, k).astype(jnp.float32) * ATTN_SCALE
    s = s + jnp.swapaxes(cq, 1, 2)[..., :, None] - jnp.swapaxes(ck, 1, 2)[..., None, :]
    s = jnp.where(kpos[None, :] <= qpos[:, None], s, -jnp.inf)
    p = jax.nn.softmax(s, axis=-1)
    return jnp.einsum('bhqs,bshd->bqhd', p.astype(v.dtype), v)


def fox_prompt(x, w_in, b_f, w_out):
    B, T, _ = x.shape
    q, k, v, logf = fox_project(x, w_in, b_f)
    c = jnp.cumsum(logf, axis=1)
    nqb = T // FOX_Q_BLOCK
    pos = jnp.arange(T)
    qb = q.reshape(B, nqb, FOX_Q_BLOCK, N_HEADS, HEAD_DIM).swapaxes(0, 1)
    cqb = c.reshape(B, nqb, FOX_Q_BLOCK, N_HEADS).swapaxes(0, 1)
    pb = pos.reshape(nqb, FOX_Q_BLOCK)
    o = lax.map(lambda blk: fox_attend(blk[0], blk[1], blk[2], k, v, c, pos), (qb, cqb, pb))
    o = o.swapaxes(0, 1).reshape(B, T, N_HEADS * HEAD_DIM)
    return o @ w_out, k, v, logf.astype(x.dtype)


def fox_sample(x, k_cache, v_cache, logf_cache, w_in, b_f, w_out):
    B, T, _ = x.shape
    L = k_cache.shape[1]
    q, k, v, logf = fox_project(x, w_in, b_f)
    k_all = jnp.concatenate([k_cache.astype(k.dtype), k], axis=1)
    v_all = jnp.concatenate([v_cache.astype(v.dtype), v], axis=1)
    c = jnp.cumsum(jnp.concatenate([logf_cache.astype(jnp.float32), logf], axis=1), axis=1)
    pos = jnp.arange(L + T)
    o = fox_attend(q, c[:, L:], pos[L:], k_all, v_all, c, pos)
    return o.reshape(B, T, N_HEADS * HEAD_DIM) @ w_out, k, v, logf.astype(x.dtype)


def setup_inputs(seed: int = 0) -> dict:
    key = jax.random.key(seed)
    ks = jax.random.split(key, 32)

    def nrm(i, shape, scale=1.0):
        return scale * jax.random.normal(ks[i], shape, jnp.float32)

    hd = N_HEADS * HEAD_DIM
    kvd = SWA_KV_HEADS * HEAD_DIM
    swa_rows = min(WINDOW, PAST_LEN)
    lam_u = jax.random.uniform(ks[18], (N_RG, LRU_WIDTH), jnp.float32, 0.9, 0.999)
    lam_s = lam_u ** (1.0 / LRU_C)
    return {
        "x_prompt": nrm(0, (BATCH, SEQ, D_MODEL)),
        "x_sample": nrm(1, (DEC_BATCH, DEC_SEQ, D_MODEL)),
        "state_rg_conv": nrm(2, (N_RG, DEC_BATCH, CONV_WIDTH - 1, LRU_WIDTH)),
        "state_rg_h": nrm(3, (N_RG, DEC_BATCH, LRU_WIDTH), 0.5),
        "cache_swa_k": nrm(4, (N_SWA, DEC_BATCH, swa_rows, SWA_KV_HEADS, HEAD_DIM)),
        "cache_swa_v": nrm(5, (N_SWA, DEC_BATCH, swa_rows, SWA_KV_HEADS, HEAD_DIM), BETA),
        "cache_fox_k": nrm(6, (N_FOX, DEC_BATCH, PAST_LEN, N_HEADS, HEAD_DIM)),
        "cache_fox_v": nrm(7, (N_FOX, DEC_BATCH, PAST_LEN, N_HEADS, HEAD_DIM), BETA),
        "cache_fox_logf": jax.nn.log_sigmoid(2.5 + nrm(8, (N_FOX, DEC_BATCH, PAST_LEN, N_HEADS))),
        "ln_gain": 1.0 + nrm(9, (DEPTH, 2, D_MODEL), 0.01),
        "ln_bias": nrm(10, (DEPTH, 2, D_MODEL), 0.01),
        "ffn_w_up": nrm(11, (DEPTH, D_MODEL, D_FF), D_MODEL ** -0.5),
        "ffn_w_down": nrm(12, (DEPTH, D_FF, D_MODEL), BETA * D_FF ** -0.5),
        "rg_w_in": nrm(13, (N_RG, D_MODEL, 2 * LRU_WIDTH), D_MODEL ** -0.5),
        "rg_conv_w": nrm(14, (N_RG, CONV_WIDTH, LRU_WIDTH), CONV_WIDTH ** -0.5),
        "rg_conv_b": nrm(15, (N_RG, LRU_WIDTH), 0.01),
        "rg_gate_w": nrm(16, (N_RG, LRU_BLOCKS, LRU_BLOCK_W, 2 * LRU_BLOCK_W), LRU_BLOCK_W ** -0.5),
        "rg_gate_b": nrm(17, (N_RG, LRU_BLOCKS, 2 * LRU_BLOCK_W), 0.01),
        "rg_lambda": jnp.log(lam_s) - jnp.log1p(-lam_s),
        "rg_w_out": nrm(19, (N_RG, LRU_WIDTH, D_MODEL), BETA * LRU_WIDTH ** -0.5),
        "swa_w_qkv": jnp.concatenate([nrm(20, (N_SWA, D_MODEL, hd), D_MODEL ** -0.5),
                                      nrm(21, (N_SWA, D_MODEL, kvd), D_MODEL ** -0.5),
                                      nrm(22, (N_SWA, D_MODEL, kvd), BETA * D_MODEL ** -0.5)], axis=-1),
        "swa_sinks": nrm(23, (N_SWA, N_HEADS), 0.5),
        "swa_w_out": nrm(24, (N_SWA, hd, D_MODEL), BETA * hd ** -0.5),
        "rel_bias_table": nrm(25, (N_BUCKETS, N_HEADS), 0.5),
        "fox_w_in": jnp.concatenate([nrm(26, (N_FOX, D_MODEL, 2 * hd), D_MODEL ** -0.5),
                                     nrm(27, (N_FOX, D_MODEL, hd), BETA * D_MODEL ** -0.5),
                                     nrm(28, (N_FOX, D_MODEL, N_HEADS), 0.5 * D_MODEL ** -0.5)], axis=-1),
        "fox_b_f": jax.random.uniform(ks[29], (N_FOX, N_HEADS), jnp.float32, 1.0, 4.0),
        "fox_w_out": nrm(30, (N_FOX, hd, D_MODEL), BETA * hd ** -0.5),
    }


def reference(x_prompt, x_sample, state_rg_conv, state_rg_h, cache_swa_k, cache_swa_v,
              cache_fox_k, cache_fox_v, cache_fox_logf, ln_gain, ln_bias, ffn_w_up, ffn_w_down,
              rg_w_in, rg_conv_w, rg_conv_b, rg_gate_w, rg_gate_b, rg_lambda, rg_w_out,
              swa_w_qkv, swa_sinks, swa_w_out, rel_bias_table, fox_w_in, fox_b_f, fox_w_out):
    xp, xs = x_prompt, x_sample
    B = xp.shape[0]
    rg_conv_p, rg_conv_s, rg_h_p, rg_h_s = [], [], [], []
    swa_k_p, swa_k_s, swa_v_p, swa_v_s = [], [], [], []
    fox_k_p, fox_k_s, fox_v_p, fox_v_s, fox_f_p, fox_f_s = [], [], [], [], [], []
    for i in range(DEPTH):
        kind, j = i % N_MIXERS, i // N_MIXERS
        if kind == 0:
            w = (rg_w_in[j], rg_conv_w[j], rg_conv_b[j], rg_gate_w[j], rg_gate_b[j], rg_lambda[j], rg_w_out[j])
            mp, cb_p, h_p = rglru_mixer(xp, jnp.zeros((B, CONV_WIDTH - 1, LRU_WIDTH), xp.dtype),
                                        jnp.zeros((B, LRU_WIDTH), xp.dtype), *w)
            ms, cb_s, h_s = rglru_mixer(xs, state_rg_conv[j], state_rg_h[j], *w)
            rg_conv_p.append(cb_p); rg_conv_s.append(cb_s); rg_h_p.append(h_p); rg_h_s.append(h_s)
        elif kind == 1:
            mp, kp, vp = swa_prompt(xp, swa_w_qkv[j], swa_sinks[j], swa_w_out[j], rel_bias_table)
            ms, kk, vv = swa_sample(xs, cache_swa_k[j], cache_swa_v[j], swa_w_qkv[j], swa_sinks[j],
                                    swa_w_out[j], rel_bias_table)
            swa_k_p.append(kp); swa_v_p.append(vp); swa_k_s.append(kk); swa_v_s.append(vv)
        else:
            mp, kp, vp, fp = fox_prompt(xp, fox_w_in[j], fox_b_f[j], fox_w_out[j])
            ms, kk, vv, fs = fox_sample(xs, cache_fox_k[j], cache_fox_v[j], cache_fox_logf[j],
                                        fox_w_in[j], fox_b_f[j], fox_w_out[j])
            fox_k_p.append(kp); fox_v_p.append(vp); fox_f_p.append(fp)
            fox_k_s.append(kk); fox_v_s.append(vv); fox_f_s.append(fs)
        xp = layer_norm(ALPHA * xp + mp, ln_gain[i, 0], ln_bias[i, 0])
        xs = layer_norm(ALPHA * xs + ms, ln_gain[i, 0], ln_bias[i, 0])
        xp = layer_norm(ALPHA * xp + sq_relu_mlp(xp, ffn_w_up[i], ffn_w_down[i]), ln_gain[i, 1], ln_bias[i, 1])
        xs = layer_norm(ALPHA * xs + sq_relu_mlp(xs, ffn_w_up[i], ffn_w_down[i]), ln_gain[i, 1], ln_bias[i, 1])
    return (xp, xs,
            jnp.stack(rg_conv_p), jnp.stack(rg_conv_s), jnp.stack(rg_h_p), jnp.stack(rg_h_s),
            jnp.stack(swa_k_p), jnp.stack(swa_k_s), jnp.stack(swa_v_p), jnp.stack(swa_v_s),
            jnp.stack(fox_k_p), jnp.stack(fox_k_s), jnp.stack(fox_v_p), jnp.stack(fox_v_s),
            jnp.stack(fox_f_p), jnp.stack(fox_f_s))
```

```python
import functools
import math

import jax
import jax.numpy as jnp
from jax import lax
from jax.experimental import pallas as pl
from jax.experimental.pallas import tpu as pltpu

F32 = jnp.float32
BF16 = jnp.bfloat16

D_MODEL = 1024
BATCH = 4
SEQ = 4096
DEPTH = 4
DEC_BATCH = 16
DEC_SEQ = 64
PAST_LEN = 1024
CHUNK = 64
D_FF = 4 * D_MODEL
HEAD_DIM = 64
N_HEADS = 16
SWA_KV_HEADS = 4
SWA_GROUP = 4
WINDOW = 128
LRU_WIDTH = D_MODEL
LRU_BLOCKS = 4
LRU_BLOCK_W = LRU_WIDTH // LRU_BLOCKS
CONV_WIDTH = 4
LRU_C = 8.0
N_BUCKETS = 32
MAX_DISTANCE = 128
ALPHA = (2.0 * DEPTH) ** 0.25
LN_EPS = 1e-5
ATTN_SCALE = HEAD_DIM ** -0.5

N_PROMPT = BATCH * SEQ
N_SAMPLE = DEC_BATCH * DEC_SEQ
N_TOK = N_PROMPT + N_SAMPLE

LANES = 128
HEADS_PER_SLAB = LANES // HEAD_DIM
N_SLABS = D_MODEL // LANES

TOKEN_TILE = 512
FF_CHUNK = 1024
RG_TILE = 512
SWA_TILE = 512
SWA_KEYS = WINDOW + CHUNK
FOX_TQ = 512
FOX_TK = 256
FOX_SAMPLE_KEYS = 1280
BIAS_TILE = 256
VMEM_LIMIT = 56 * 1024 * 1024


def _resident(shape):
    zeros = (0,) * len(shape)
    return pl.BlockSpec(shape, lambda *_: zeros, pipeline_mode=pl.Buffered(1))


def _params(semantics):
    return pltpu.CompilerParams(dimension_semantics=semantics, vmem_limit_bytes=VMEM_LIMIT)


def _softplus(x):
    return jnp.maximum(x, 0.0) + jnp.log1p(jnp.exp(-jnp.abs(x)))


def _layer_norm(z, g, b):
    mu = jnp.mean(z, axis=-1, keepdims=True)
    zc = z - mu
    var = jnp.mean(zc * zc, axis=-1, keepdims=True)
    return zc * lax.rsqrt(var + LN_EPS) * g + b


def _post_kernel(x_ref, o_ref, wout_ref, wup_ref, wdn_ref, g_ref, b_ref, out_ref):
    x = x_ref[...]
    y = jnp.dot(o_ref[...], wout_ref[...], preferred_element_type=F32)
    x1 = _layer_norm(ALPHA * x + y, g_ref[0:1, :], b_ref[0:1, :])
    x1b = x1.astype(BF16)
    acc = jnp.zeros_like(x1)
    for c in range(D_FF // FF_CHUNK):
        cols = slice(c * FF_CHUNK, (c + 1) * FF_CHUNK)
        h = jnp.dot(x1b, wup_ref[:, cols], preferred_element_type=F32)
        a = jnp.square(jnp.maximum(h, 0.0)).astype(BF16)
        acc = acc + jnp.dot(a, wdn_ref[cols, :], preferred_element_type=F32)
    out_ref[...] = _layer_norm(ALPHA * x1 + acc, g_ref[1:2, :], b_ref[1:2, :])


def _post_call(x, o, w_out, w_up, w_down, gain, bias):
    tile = pl.BlockSpec((TOKEN_TILE, D_MODEL), lambda i: (i, 0))
    return pl.pallas_call(
        _post_kernel,
        out_shape=jax.ShapeDtypeStruct((N_TOK, D_MODEL), F32),
        grid=(N_TOK // TOKEN_TILE,),
        in_specs=[tile, tile, _resident((D_MODEL, D_MODEL)), _resident((D_MODEL, D_FF)),
                  _resident((D_FF, D_MODEL)), _resident((2, D_MODEL)), _resident((2, D_MODEL))],
        out_specs=tile,
        compiler_params=_params(("parallel",)),
        name="post",
    )(x, o, w_out, w_up, w_down, gain, bias)


def _rg_kernel(x_ref, cbuf_ref, h0_ref, win_ref, cw_ref, cb_ref, gw_ref, gb_ref, lam_ref,
               o_ref, nbuf_ref, hlast_ref, u_scr, a_scr, b_scr, h_scr, hc_scr, *, tile):
    t = pl.program_id(1)
    hist = CONV_WIDTH - 1
    pad = 8

    @pl.when(t == 0)
    def _():
        u_scr[pad - hist:pad, :] = cbuf_ref[0]
        hc_scr[...] = h0_ref[0]

    xb = x_ref[...].astype(BF16)
    gu = jnp.dot(xb, win_ref[...], preferred_element_type=F32)
    gate = gu[:, :LRU_WIDTH]
    u = gu[:, LRU_WIDTH:]
    u_scr[pad:pad + tile, :] = u
    conv = cb_ref[...] + u * cw_ref[hist:hist + 1, :]
    for k in range(hist):
        conv = conv + u_scr[pad - hist + k:pad - hist + k + tile, :] * cw_ref[k:k + 1, :]
    nbuf_ref[0] = u_scr[pad + tile - hist:pad + tile, :]
    u_scr[pad - hist:pad, :] = u_scr[pad + tile - hist:pad + tile, :]

    for n in range(LRU_BLOCKS):
        cols = slice(n * LRU_BLOCK_W, (n + 1) * LRU_BLOCK_W)
        cn = conv[:, cols]
        g = jnp.dot(cn.astype(BF16), gw_ref[n], preferred_element_type=F32) + gb_ref[n]
        r = jax.nn.sigmoid(g[:, :LRU_BLOCK_W])
        ig = jax.nn.sigmoid(g[:, LRU_BLOCK_W:])
        log_a = -LRU_C * r * _softplus(-lam_ref[:, cols])
        a = jnp.exp(log_a)
        a_scr[:, cols] = a
        b_scr[:, cols] = jnp.sqrt(1.0 - a * a) * (ig * cn)

    row = lax.broadcasted_iota(jnp.int32, (8, LRU_WIDTH), 0)

    def scan_rows(j, h_prev):
        r0 = pl.multiple_of(j * 8, 8)
        a = a_scr[pl.ds(r0, 8), :]
        b = b_scr[pl.ds(r0, 8), :]
        for d in (1, 2, 4):
            a_sh = jnp.where(row >= d, pltpu.roll(a, d, 0), 1.0)
            b_sh = jnp.where(row >= d, pltpu.roll(b, d, 0), 0.0)
            b = a * b_sh + b
            a = a * a_sh
        h = a * h_prev + b
        h_scr[pl.ds(r0, 8), :] = h
        return h[7:8, :]

    h_last = lax.fori_loop(0, tile // 8, scan_rows, hc_scr[...])
    hc_scr[...] = h_last
    hlast_ref[0] = h_last
    o_ref[...] = (h_scr[...] * jax.nn.gelu(gate)).astype(BF16)


def _rg_call(x, conv_buf, h0, w_in, conv_w, conv_b, gate_w, gate_b, lam, *, n_seq, seq_len, tile,
             row_offset):
    n_t = seq_len // tile
    off = row_offset // tile
    kern = functools.partial(_rg_kernel, tile=tile)
    o, nbuf, hlast = pl.pallas_call(
        kern,
        out_shape=(jax.ShapeDtypeStruct((n_seq * seq_len, D_MODEL), BF16),
                   jax.ShapeDtypeStruct((n_seq, CONV_WIDTH - 1, LRU_WIDTH), F32),
                   jax.ShapeDtypeStruct((n_seq, 1, LRU_WIDTH), F32)),
        grid=(n_seq, n_t),
        in_specs=[pl.BlockSpec((tile, D_MODEL), lambda b, t: (off + b * n_t + t, 0)),
                  pl.BlockSpec((1, CONV_WIDTH - 1, LRU_WIDTH), lambda b, t: (b, 0, 0)),
                  pl.BlockSpec((1, 1, LRU_WIDTH), lambda b, t: (b, 0, 0)),
                  _resident((D_MODEL, 2 * LRU_WIDTH)),
                  _resident((CONV_WIDTH, LRU_WIDTH)),
                  _resident((1, LRU_WIDTH)),
                  _resident((LRU_BLOCKS, LRU_BLOCK_W, 2 * LRU_BLOCK_W)),
                  _resident((LRU_BLOCKS, 1, 2 * LRU_BLOCK_W)),
                  _resident((1, LRU_WIDTH))],
        out_specs=(pl.BlockSpec((tile, D_MODEL), lambda b, t: (b * n_t + t, 0)),
                   pl.BlockSpec((1, CONV_WIDTH - 1, LRU_WIDTH), lambda b, t: (b, 0, 0)),
                   pl.BlockSpec((1, 1, LRU_WIDTH), lambda b, t: (b, 0, 0))),
        scratch_shapes=[pltpu.VMEM((tile + 8, LRU_WIDTH), F32),
                        pltpu.VMEM((tile, LRU_WIDTH), F32),
                        pltpu.VMEM((tile, LRU_WIDTH), F32),
                        pltpu.VMEM((tile, LRU_WIDTH), F32),
                        pltpu.VMEM((1, LRU_WIDTH), F32)],
        compiler_params=_params(("parallel", "arbitrary")),
        name="rg_mixer",
    )(x, conv_buf, h0, w_in, conv_w, conv_b, gate_w, gate_b, lam)
    return o, nbuf, hlast[:, 0, :]


def _swa_proj_kernel(x_ref, w_ref, q_ref, k_ref, v_ref, kb_ref, vb_ref):
    xb = x_ref[...].astype(BF16)
    qkv = jnp.dot(xb, w_ref[...], preferred_element_type=F32)
    kvd = SWA_KV_HEADS * HEAD_DIM
    q_ref[...] = qkv[:, :D_MODEL].astype(BF16)
    k = qkv[:, D_MODEL:D_MODEL + kvd]
    v = qkv[:, D_MODEL + kvd:]
    k_ref[...] = k
    v_ref[...] = v
    kb_ref[...] = k.astype(BF16)
    vb_ref[...] = v.astype(BF16)


def _swa_proj_call(x, w):
    kvd = SWA_KV_HEADS * HEAD_DIM
    row = lambda i: (i, 0)
    return pl.pallas_call(
        _swa_proj_kernel,
        out_shape=(jax.ShapeDtypeStruct((N_TOK, D_MODEL), BF16),
                   jax.ShapeDtypeStruct((N_TOK, kvd), F32),
                   jax.ShapeDtypeStruct((N_TOK, kvd), F32),
                   jax.ShapeDtypeStruct((N_TOK, kvd), BF16),
                   jax.ShapeDtypeStruct((N_TOK, kvd), BF16)),
        grid=(N_TOK // TOKEN_TILE,),
        in_specs=[pl.BlockSpec((TOKEN_TILE, D_MODEL), row), _resident((D_MODEL, D_MODEL + 2 * kvd))],
        out_specs=(pl.BlockSpec((TOKEN_TILE, D_MODEL), row),
                   pl.BlockSpec((TOKEN_TILE, kvd), row), pl.BlockSpec((TOKEN_TILE, kvd), row),
                   pl.BlockSpec((TOKEN_TILE, kvd), row), pl.BlockSpec((TOKEN_TILE, kvd), row)),
        compiler_params=_params(("parallel",)),
        name="swa_proj",
    )(x, w)


def _swa_attn_kernel(q_ref, k_ref, v_ref, bias_ref, sink_ref, o_ref, *, tile, n_invalid):
    ti = pl.program_id(1)
    rows = SWA_GROUP * HEADS_PER_SLAB * CHUNK
    lane_q = lax.broadcasted_iota(jnp.int32, (CHUNK, LANES), 1)
    low_q = lane_q < HEAD_DIM
    key_idx = lax.broadcasted_iota(jnp.int32, (rows, SWA_KEYS), 1)
    for c in range(tile // CHUNK):
        row0 = pl.multiple_of(ti * tile + c * CHUNK, CHUNK)
        valid = key_idx + row0 >= n_invalid
        for t in range(SWA_KV_HEADS // HEADS_PER_SLAB):
            kv_cols = slice(t * LANES, (t + 1) * LANES)
            kw = k_ref[0, pl.ds(row0, SWA_KEYS), kv_cols]
            vw = v_ref[0, pl.ds(row0, SWA_KEYS), kv_cols]
            parts = []
            for g in range(SWA_GROUP):
                s_idx = SWA_GROUP * t + g
                slab = q_ref[0, c * CHUNK:(c + 1) * CHUNK, s_idx * LANES:(s_idx + 1) * LANES]
                zero = jnp.zeros_like(slab)
                parts.append(jnp.where(low_q, slab, zero))
                parts.append(jnp.where(low_q, zero, slab))
            lhs = jnp.concatenate(parts, axis=0)
            s = lax.dot_general(lhs, kw, (((1,), (1,)), ((), ())), preferred_element_type=F32)
            s = s + bias_ref[t]
            s = jnp.where(valid, s, -jnp.inf)
            sink = sink_ref[t]
            m = jnp.maximum(jnp.max(s, axis=1, keepdims=True), sink)
            p = jnp.exp(s - m)
            denom = jnp.sum(p, axis=1, keepdims=True) + jnp.exp(sink - m)
            out = jnp.dot(p.astype(BF16), vw, preferred_element_type=F32) / denom
            for g in range(SWA_GROUP):
                s_idx = SWA_GROUP * t + g
                lo = out[(2 * g) * CHUNK:(2 * g + 1) * CHUNK, :]
                hi = out[(2 * g + 1) * CHUNK:(2 * g + 2) * CHUNK, :]
                o_ref[0, c * CHUNK:(c + 1) * CHUNK, s_idx * LANES:(s_idx + 1) * LANES] = (
                    jnp.where(low_q, lo, hi).astype(BF16))


def _swa_attn_call(q, k_win, v_win, bias, sink, *, n_seq, seq_len, tile, n_invalid):
    kvd = SWA_KV_HEADS * HEAD_DIM
    n_keys = k_win.shape[1]
    rows = SWA_GROUP * HEADS_PER_SLAB * CHUNK
    n_kv_slabs = SWA_KV_HEADS // HEADS_PER_SLAB
    kern = functools.partial(_swa_attn_kernel, tile=tile, n_invalid=n_invalid)
    return pl.pallas_call(
        kern,
        out_shape=jax.ShapeDtypeStruct((n_seq, seq_len, D_MODEL), BF16),
        grid=(n_seq, seq_len // tile),
        in_specs=[pl.BlockSpec((1, tile, D_MODEL), lambda b, t: (b, t, 0)),
                  pl.BlockSpec((1, n_keys, kvd), lambda b, t: (b, 0, 0)),
                  pl.BlockSpec((1, n_keys, kvd), lambda b, t: (b, 0, 0)),
                  _resident((n_kv_slabs, rows, SWA_KEYS)),
                  _resident((n_kv_slabs, rows, 1))],
        out_specs=pl.BlockSpec((1, tile, D_MODEL), lambda b, t: (b, t, 0)),
        compiler_params=_params(("parallel", "arbitrary")),
        name="swa_attn",
    )(q, k_win, v_win, bias, sink)


def _t5_bucket(rel):
    half = N_BUCKETS // 2
    max_exact = half // 2
    n = jnp.abs(rel)
    n_f = jnp.maximum(n, 1).astype(jnp.float32)
    large = max_exact + (jnp.log(n_f / max_exact) / math.log(MAX_DISTANCE / max_exact)
                         * (half - max_exact)).astype(jnp.int32)
    large = jnp.minimum(large, half - 1)
    return jnp.where(rel > 0, half, 0) + jnp.where(n < max_exact, n, large)


_SWA_HEAD_ORDER = [8 * t + 4 * p + g for t in range(2) for g in range(4) for p in range(2)]


def _swa_row_tables(table, sinks):
    rel = jnp.arange(SWA_KEYS)[None, :] - WINDOW - jnp.arange(CHUNK)[:, None]
    bias = jnp.transpose(table[_t5_bucket(rel)].astype(F32), (2, 0, 1))
    order = jnp.array(_SWA_HEAD_ORDER)
    rows = SWA_GROUP * HEADS_PER_SLAB * CHUNK
    bias = bias[order].reshape(2, rows, SWA_KEYS)
    sink = jnp.repeat(sinks.astype(F32)[order], CHUNK).reshape(2, rows, 1)
    return bias, sink


def _fox_proj_kernel(x_ref, w_ref, bf_ref, q_ref, k_ref, v_ref, kb_ref, vb_ref, lf_ref):
    xb = x_ref[...].astype(BF16)
    proj = jnp.dot(xb, w_ref[...], preferred_element_type=F32)
    q_ref[...] = proj[:, :D_MODEL].astype(BF16)
    k = proj[:, D_MODEL:2 * D_MODEL]
    v = proj[:, 2 * D_MODEL:3 * D_MODEL]
    k_ref[...] = k
    v_ref[...] = v
    kb_ref[...] = k.astype(BF16)
    vb_ref[...] = v.astype(BF16)
    z = proj[:, 3 * D_MODEL:] + bf_ref[...]
    lf_ref[...] = -_softplus(-z)


def _fox_proj_call(x, w, b_f):
    row = lambda i: (i, 0)
    wide = pl.BlockSpec((TOKEN_TILE, D_MODEL), row)
    return pl.pallas_call(
        _fox_proj_kernel,
        out_shape=(jax.ShapeDtypeStruct((N_TOK, D_MODEL), BF16),
                   jax.ShapeDtypeStruct((N_TOK, D_MODEL), F32),
                   jax.ShapeDtypeStruct((N_TOK, D_MODEL), F32),
                   jax.ShapeDtypeStruct((N_TOK, D_MODEL), BF16),
                   jax.ShapeDtypeStruct((N_TOK, D_MODEL), BF16),
                   jax.ShapeDtypeStruct((N_TOK, LANES), F32)),
        grid=(N_TOK // TOKEN_TILE,),
        in_specs=[wide, _resident((D_MODEL, 3 * D_MODEL + LANES)), _resident((1, LANES))],
        out_specs=(wide, wide, wide, wide, wide, pl.BlockSpec((TOKEN_TILE, LANES), row)),
        compiler_params=_params(("parallel",)),
        name="fox_proj",
    )(x, w, b_f)


def _split3(x):
    hi = x.astype(BF16)
    r1 = x - hi.astype(F32)
    mid = r1.astype(BF16)
    lo = (r1 - mid.astype(F32)).astype(BF16)
    return hi, mid, lo


def _fox_bias_kernel(lf_ref, pq_ref, pk_ref, cq_ref, ck_ref, bq_ref, bk_ref, carry_scr):
    t = pl.program_id(1)

    @pl.when(t == 0)
    def _():
        carry_scr[...] = jnp.zeros_like(carry_scr)

    r = lax.broadcasted_iota(jnp.int32, (BIAS_TILE, BIAS_TILE), 0)
    cidx = lax.broadcasted_iota(jnp.int32, (BIAS_TILE, BIAS_TILE), 1)
    tri = (cidx <= r).astype(BF16)
    c = carry_scr[...]
    for part in _split3(lf_ref[0]):
        c = c + jnp.dot(tri, part, preferred_element_type=F32)
    carry_scr[...] = c[BIAS_TILE - 1:BIAS_TILE, :]
    bq = cq_ref[...]
    bk = ck_ref[...]
    for i, part in enumerate(_split3(c)):
        bq = bq + jnp.dot(part, pq_ref[i], preferred_element_type=F32)
        bk = bk + jnp.dot(part, pk_ref[i], preferred_element_type=F32)
    bq_ref[0] = bq.astype(BF16)
    bk_ref[0] = bk.astype(BF16)


def _fox_bias_tables():
    pq = [[[0.0] * LANES for _ in range(LANES)] for _ in range(3)]
    pk = [[[0.0] * LANES for _ in range(LANES)] for _ in range(3)]
    cq = [0.0] * LANES
    ck = [0.0] * LANES
    for i in range(3):
        for h in range(N_HEADS):
            pk[i][h][N_HEADS * i + h] = -1.0
            cq[N_HEADS * i + h] = 1.0
            pq[i][h][N_HEADS * (3 + i) + h] = 1.0
            ck[N_HEADS * (3 + i) + h] = 1.0
    return (jnp.array(pq, BF16), jnp.array(pk, BF16),
            jnp.array([cq], F32), jnp.array([ck], F32))


def _fox_bias_call(logf):
    n_seq, n_rows, _ = logf.shape
    pq, pk, cq, ck = _fox_bias_tables()
    blk = pl.BlockSpec((1, BIAS_TILE, LANES), lambda b, t: (b, t, 0))
    return pl.pallas_call(
        _fox_bias_kernel,
        out_shape=(jax.ShapeDtypeStruct((n_seq, n_rows, LANES), BF16),
                   jax.ShapeDtypeStruct((n_seq, n_rows, LANES), BF16)),
        grid=(n_seq, n_rows // BIAS_TILE),
        in_specs=[blk, _resident((3, LANES, LANES)), _resident((3, LANES, LANES)),
                  _resident((1, LANES)), _resident((1, LANES))],
        out_specs=(blk, blk),
        scratch_shapes=[pltpu.VMEM((1, LANES), F32)],
        compiler_params=_params(("parallel", "arbitrary")),
        name="fox_bias",
    )(logf, pq, pk, cq, ck)


def _fox_attn_kernel(q_ref, bq_ref, k_ref, bk_ref, v_ref, o_ref, *, tq, tk, q_pos0):
    hp = pl.program_id(1)
    qi = pl.program_id(2)
    first_q = q_pos0 + qi * tq
    n_full = first_q // tk
    lane = lax.broadcasted_iota(jnp.int32, (tq, LANES), 1)
    low = lane < HEAD_DIM
    q_slab = q_ref[0]
    bq = bq_ref[0]
    zero = jnp.zeros_like(q_slab)
    q_pos = first_q + lax.broadcasted_iota(jnp.int32, (tq, tk), 0)
    k_off = lax.broadcasted_iota(jnp.int32, (tq, tk), 1)
    results = []
    for par in range(HEADS_PER_SLAB):
        head = hp * HEADS_PER_SLAB + par
        q_head = jnp.where(low, q_slab, zero) if par == 0 else jnp.where(low, zero, q_slab)
        sel = jnp.logical_and((lane & (N_HEADS - 1)) == head, lane < 6 * N_HEADS)
        q_aug = jnp.concatenate([q_head, jnp.where(sel, bq, zero)], axis=1)

        def scores(kb):
            r0 = pl.multiple_of(kb * tk, tk)
            k_aug = jnp.concatenate([k_ref[0, pl.ds(r0, tk), :], bk_ref[0, pl.ds(r0, tk), :]], axis=1)
            s = lax.dot_general(q_aug, k_aug, (((1,), (1,)), ((), ())), preferred_element_type=F32)
            return s, v_ref[0, pl.ds(r0, tk), :]

        def update(carry, s, vb):
            m, l, acc = carry
            m_new = jnp.maximum(m, jnp.max(s, axis=1, keepdims=True))
            p = jnp.exp(s - m_new)
            alpha = jnp.exp(m - m_new)
            l = alpha * l + jnp.sum(p, axis=1, keepdims=True)
            acc = alpha * acc + jnp.dot(p.astype(BF16), vb, preferred_element_type=F32)
            return m_new, l, acc

        def full_block(kb, carry):
            s, vb = scores(kb)
            return update(carry, s, vb)

        init = (jnp.full((tq, 1), -jnp.inf, F32), jnp.zeros((tq, 1), F32),
                jnp.zeros((tq, LANES), F32))
        carry = lax.fori_loop(0, n_full, full_block, init)
        for d in range(pl.cdiv(tq, tk)):
            s, vb = scores(n_full + d)
            k_pos = (n_full + d) * tk + k_off
            carry = update(carry, jnp.where(k_pos <= q_pos, s, -jnp.inf), vb)
        _, l, acc = carry
        results.append(acc / l)
    o_ref[0] = jnp.where(low, results[0], results[1]).astype(BF16)


def _fox_attn_call(q, bq, k, bk, v, *, n_seq, n_q, n_keys, tq, tk, q_pos0):
    kern = functools.partial(_fox_attn_kernel, tq=tq, tk=tk, q_pos0=q_pos0)
    q_blk0 = q_pos0 // tq
    kv_spec = pl.BlockSpec((1, n_keys, LANES), lambda b, h, i: (b, 0, h))
    return pl.pallas_call(
        kern,
        out_shape=jax.ShapeDtypeStruct((n_seq, n_q, D_MODEL), BF16),
        grid=(n_seq, N_SLABS, n_q // tq),
        in_specs=[pl.BlockSpec((1, tq, LANES), lambda b, h, i: (b, i, h)),
                  pl.BlockSpec((1, tq, LANES), lambda b, h, i: (b, q_blk0 + i, 0)),
                  kv_spec,
                  pl.BlockSpec((1, n_keys, LANES), lambda b, h, i: (b, 0, 0)),
                  kv_spec],
        out_specs=pl.BlockSpec((1, tq, LANES), lambda b, h, i: (b, i, h)),
        compiler_params=_params(("parallel", "parallel", "arbitrary")),
        name="fox_attn",
    )(q, bq, k, bk, v)


def _rg_layer(x, conv_state, h_state, w_in, conv_w, conv_b, gate_w, gate_b, lam):
    w = (w_in.astype(BF16), conv_w, conv_b.reshape(1, LRU_WIDTH), gate_w.astype(BF16),
         gate_b.reshape(LRU_BLOCKS, 1, 2 * LRU_BLOCK_W), lam.reshape(1, LRU_WIDTH))
    o_p, nb_p, h_p = _rg_call(
        x, jnp.zeros((BATCH, CONV_WIDTH - 1, LRU_WIDTH), F32), jnp.zeros((BATCH, 1, LRU_WIDTH), F32),
        *w, n_seq=BATCH, seq_len=SEQ, tile=RG_TILE, row_offset=0)
    o_s, nb_s, h_s = _rg_call(
        x, conv_state, h_state.reshape(DEC_BATCH, 1, LRU_WIDTH),
        *w, n_seq=DEC_BATCH, seq_len=DEC_SEQ, tile=DEC_SEQ, row_offset=N_PROMPT)
    return jnp.concatenate([o_p, o_s], axis=0), (nb_p, nb_s, h_p, h_s)


def _swa_layer(x, k_cache, v_cache, w_qkv, sinks, table):
    kvd = SWA_KV_HEADS * HEAD_DIM
    order = jnp.array(_SWA_HEAD_ORDER)
    w_q = (w_qkv[:, :D_MODEL] * ATTN_SCALE).reshape(D_MODEL, N_HEADS, HEAD_DIM)[:, order]
    w = jnp.concatenate([w_q.reshape(D_MODEL, D_MODEL), w_qkv[:, D_MODEL:]], axis=1).astype(BF16)
    q, k, v, kb, vb = _swa_proj_call(x, w)
    bias, sink = _swa_row_tables(table, sinks)

    q_p = q[:N_PROMPT].reshape(BATCH, SEQ, D_MODEL)
    pad = ((0, 0), (WINDOW, 0), (0, 0))
    k_p = jnp.pad(kb[:N_PROMPT].reshape(BATCH, SEQ, kvd), pad)
    v_p = jnp.pad(vb[:N_PROMPT].reshape(BATCH, SEQ, kvd), pad)
    o_p = _swa_attn_call(q_p, k_p, v_p, bias, sink, n_seq=BATCH, seq_len=SEQ, tile=SWA_TILE,
                         n_invalid=WINDOW)

    q_s = q[N_PROMPT:].reshape(DEC_BATCH, DEC_SEQ, D_MODEL)
    kc = k_cache.reshape(DEC_BATCH, WINDOW, kvd)
    vc = v_cache.reshape(DEC_BATCH, WINDOW, kvd)
    k_s = jnp.concatenate([kc.astype(BF16), kb[N_PROMPT:].reshape(DEC_BATCH, DEC_SEQ, kvd)], axis=1)
    v_s = jnp.concatenate([vc.astype(BF16), vb[N_PROMPT:].reshape(DEC_BATCH, DEC_SEQ, kvd)], axis=1)
    o_s = _swa_attn_call(q_s, k_s, v_s, bias, sink, n_seq=DEC_BATCH, seq_len=DEC_SEQ, tile=DEC_SEQ,
                         n_invalid=0)
    o = jnp.concatenate([o_p.reshape(N_PROMPT, D_MODEL), o_s.reshape(N_SAMPLE, D_MODEL)], axis=0)

    def tails(new, cache):
        new_p = new[:N_PROMPT].reshape(BATCH, SEQ, SWA_KV_HEADS, HEAD_DIM)[:, -WINDOW:]
        new_s = new[N_PROMPT:].reshape(DEC_BATCH, DEC_SEQ, SWA_KV_HEADS, HEAD_DIM)
        return new_p, jnp.concatenate([cache, new_s], axis=1)[:, -WINDOW:]

    k_tp, k_ts = tails(k, k_cache)
    v_tp, v_ts = tails(v, v_cache)
    return o, (k_tp, k_ts, v_tp, v_ts)


def _fox_layer(x, k_cache, v_cache, logf_cache, w_in, b_f):
    hd = N_HEADS * HEAD_DIM
    w = jnp.concatenate([w_in[:, :hd] * ATTN_SCALE, w_in[:, hd:3 * hd],
                         jnp.pad(w_in[:, 3 * hd:], ((0, 0), (0, LANES - N_HEADS)))], axis=1).astype(BF16)
    bf = jnp.pad(b_f.astype(F32), (0, LANES - N_HEADS)).reshape(1, LANES)
    q, k, v, kb, vb, lf = _fox_proj_call(x, w, bf)

    def seqs(a, lo, n_seq, seq_len):
        return a[lo:lo + n_seq * seq_len].reshape(n_seq, seq_len, a.shape[-1])

    bq_p, bk_p = _fox_bias_call(seqs(lf, 0, BATCH, SEQ))
    o_p = _fox_attn_call(seqs(q, 0, BATCH, SEQ), bq_p, seqs(kb, 0, BATCH, SEQ), bk_p,
                         seqs(vb, 0, BATCH, SEQ), n_seq=BATCH, n_q=SEQ, n_keys=SEQ,
                         tq=FOX_TQ, tk=FOX_TK, q_pos0=0)

    tail = FOX_SAMPLE_KEYS - PAST_LEN - DEC_SEQ

    def with_cache(cache, new):
        return jnp.pad(jnp.concatenate([cache, new], axis=1), ((0, 0), (0, tail), (0, 0)))

    lf_cache = jnp.pad(logf_cache.astype(F32), ((0, 0), (0, 0), (0, LANES - N_HEADS)))
    bq_s, bk_s = _fox_bias_call(with_cache(lf_cache, seqs(lf, N_PROMPT, DEC_BATCH, DEC_SEQ)))
    k_s = with_cache(k_cache.reshape(DEC_BATCH, PAST_LEN, hd).astype(BF16),
                     seqs(kb, N_PROMPT, DEC_BATCH, DEC_SEQ))
    v_s = with_cache(v_cache.reshape(DEC_BATCH, PAST_LEN, hd).astype(BF16),
                     seqs(vb, N_PROMPT, DEC_BATCH, DEC_SEQ))
    o_s = _fox_attn_call(seqs(q, N_PROMPT, DEC_BATCH, DEC_SEQ), bq_s, k_s, bk_s, v_s,
                         n_seq=DEC_BATCH, n_q=DEC_SEQ, n_keys=FOX_SAMPLE_KEYS,
                         tq=DEC_SEQ, tk=FOX_TK, q_pos0=PAST_LEN)
    o = jnp.concatenate([o_p.reshape(N_PROMPT, D_MODEL), o_s.reshape(N_SAMPLE, D_MODEL)], axis=0)

    def heads(a, lo, n_seq, seq_len):
        return a[lo:lo + n_seq * seq_len].reshape(n_seq, seq_len, N_HEADS, HEAD_DIM)

    outs = (heads(k, 0, BATCH, SEQ), heads(k, N_PROMPT, DEC_BATCH, DEC_SEQ),
            heads(v, 0, BATCH, SEQ), heads(v, N_PROMPT, DEC_BATCH, DEC_SEQ),
            seqs(lf, 0, BATCH, SEQ)[..., :N_HEADS], seqs(lf, N_PROMPT, DEC_BATCH, DEC_SEQ)[..., :N_HEADS])
    return o, outs


def kernel(x_prompt, x_sample, state_rg_conv, state_rg_h, cache_swa_k, cache_swa_v, cache_fox_k, cache_fox_v, cache_fox_logf, ln_gain, ln_bias, ffn_w_up, ffn_w_down, rg_w_in, rg_conv_w, rg_conv_b, rg_gate_w, rg_gate_b, rg_lambda, rg_w_out, swa_w_qkv, swa_sinks, swa_w_out, rel_bias_table, fox_w_in, fox_b_f, fox_w_out):
    x = jnp.concatenate([x_prompt.reshape(N_PROMPT, D_MODEL), x_sample.reshape(N_SAMPLE, D_MODEL)], axis=0)
    rg_out, swa_out, fox_out = [], [], []
    for i in range(DEPTH):
        kind, j = i % 3, i // 3
        if kind == 0:
            o, extra = _rg_layer(x, state_rg_conv[j], state_rg_h[j], rg_w_in[j], rg_conv_w[j],
                                 rg_conv_b[j], rg_gate_w[j], rg_gate_b[j], rg_lambda[j])
            rg_out.append(extra)
            w_out = rg_w_out[j]
        elif kind == 1:
            o, extra = _swa_layer(x, cache_swa_k[j], cache_swa_v[j], swa_w_qkv[j], swa_sinks[j],
                                  rel_bias_table)
            swa_out.append(extra)
            order = jnp.array(_SWA_HEAD_ORDER)
            w_out = swa_w_out[j].reshape(N_HEADS, HEAD_DIM, D_MODEL)[order].reshape(D_MODEL, D_MODEL)
        else:
            o, extra = _fox_layer(x, cache_fox_k[j], cache_fox_v[j], cache_fox_logf[j], fox_w_in[j],
                                  fox_b_f[j])
            fox_out.append(extra)
            w_out = fox_w_out[j]
        x = _post_call(x, o, w_out.astype(BF16), ffn_w_up[i].astype(BF16), ffn_w_down[i].astype(BF16),
                       ln_gain[i], ln_bias[i])

    def stack(items, idx):
        return jnp.stack([it[idx] for it in items])

    return (x[:N_PROMPT].reshape(BATCH, SEQ, D_MODEL), x[N_PROMPT:].reshape(DEC_BATCH, DEC_SEQ, D_MODEL),
            stack(rg_out, 0), stack(rg_out, 1), stack(rg_out, 2), stack(rg_out, 3),
            stack(swa_out, 0), stack(swa_out, 1), stack(swa_out, 2), stack(swa_out, 3),
            stack(fox_out, 0), stack(fox_out, 1), stack(fox_out, 2), stack(fox_out, 3),
            stack(fox_out, 4), stack(fox_out, 5))
```

```python
import functools
import math

import jax
import jax.numpy as jnp
from jax import lax
from jax.experimental import pallas as pl
from jax.experimental.pallas import tpu as pltpu

F32 = jnp.float32
BF16 = jnp.bfloat16

D_MODEL = 1024
BATCH = 4
SEQ = 4096
DEPTH = 4
DEC_BATCH = 16
DEC_SEQ = 64
PAST_LEN = 1024
CHUNK = 64
D_FF = 4 * D_MODEL
HEAD_DIM = 64
N_HEADS = 16
SWA_KV_HEADS = 4
SWA_GROUP = 4
WINDOW = 128
LRU_WIDTH = D_MODEL
LRU_BLOCKS = 4
LRU_BLOCK_W = LRU_WIDTH // LRU_BLOCKS
CONV_WIDTH = 4
LRU_C = 8.0
N_BUCKETS = 32
MAX_DISTANCE = 128
ALPHA = (2.0 * DEPTH) ** 0.25
LN_EPS = 1e-5
ATTN_SCALE = HEAD_DIM ** -0.5

N_PROMPT = BATCH * SEQ
N_SAMPLE = DEC_BATCH * DEC_SEQ
N_TOK = N_PROMPT + N_SAMPLE

LANES = 128
HEADS_PER_SLAB = LANES // HEAD_DIM
N_SLABS = D_MODEL // LANES

TOKEN_TILE = 512
FF_CHUNK = 1024
RG_TILE = 512
SWA_TILE = 512
SWA_KEYS = WINDOW + CHUNK
FOX_TQ = 256
FOX_TK = 512
FOX_SAMPLE_KEYS = 1536
FOX_ONES_ROWS = 16
BIAS_TILE = 256
VMEM_LIMIT = 56 * 1024 * 1024


def _resident(shape):
    zeros = (0,) * len(shape)
    return pl.BlockSpec(shape, lambda *_: zeros, pipeline_mode=pl.Buffered(1))


def _params(semantics):
    return pltpu.CompilerParams(dimension_semantics=semantics, vmem_limit_bytes=VMEM_LIMIT)


def _softplus(x):
    return jnp.maximum(x, 0.0) + jnp.log1p(jnp.exp(-jnp.abs(x)))


def _layer_norm(z, g, b):
    mu = jnp.mean(z, axis=-1, keepdims=True)
    zc = z - mu
    var = jnp.mean(zc * zc, axis=-1, keepdims=True)
    return zc * lax.rsqrt(var + LN_EPS) * g + b


def _post_kernel(x_ref, o_ref, wout_ref, wup_ref, wdn_ref, g_ref, b_ref, out_ref, *, slab_major):
    x = x_ref[...]
    if slab_major:
        o = jnp.concatenate([o_ref[s] for s in range(N_SLABS)], axis=1)
    else:
        o = o_ref[...]
    y = jnp.dot(o, wout_ref[...], preferred_element_type=F32)
    x1 = _layer_norm(ALPHA * x + y, g_ref[0:1, :], b_ref[0:1, :])
    x1b = x1.astype(BF16)
    acc = jnp.zeros_like(x1)
    for c in range(D_FF // FF_CHUNK):
        cols = slice(c * FF_CHUNK, (c + 1) * FF_CHUNK)
        h = jnp.dot(x1b, wup_ref[:, cols], preferred_element_type=F32)
        a = jnp.square(jnp.maximum(h, 0.0)).astype(BF16)
        acc = acc + jnp.dot(a, wdn_ref[cols, :], preferred_element_type=F32)
    out_ref[...] = _layer_norm(ALPHA * x1 + acc, g_ref[1:2, :], b_ref[1:2, :])


def _post_call(x, o, w_out, w_up, w_down, gain, bias):
    tile = pl.BlockSpec((TOKEN_TILE, D_MODEL), lambda i: (i, 0))
    slab_major = o.ndim == 3
    o_tile = pl.BlockSpec((N_SLABS, TOKEN_TILE, LANES), lambda i: (0, i, 0)) if slab_major else tile
    return pl.pallas_call(
        functools.partial(_post_kernel, slab_major=slab_major),
        out_shape=jax.ShapeDtypeStruct((N_TOK, D_MODEL), F32),
        grid=(N_TOK // TOKEN_TILE,),
        in_specs=[tile, o_tile, _resident((D_MODEL, D_MODEL)), _resident((D_MODEL, D_FF)),
                  _resident((D_FF, D_MODEL)), _resident((2, D_MODEL)), _resident((2, D_MODEL))],
        out_specs=tile,
        compiler_params=_params(("parallel",)),
        name="post",
    )(x, o, w_out, w_up, w_down, gain, bias)


def _rg_kernel(x_ref, cbuf_ref, h0_ref, win_ref, cw_ref, cb_ref, gw_ref, gb_ref, lam_ref,
               o_ref, nbuf_ref, hlast_ref, u_scr, a_scr, b_scr, h_scr, hc_scr, *, tile):
    t = pl.program_id(1)
    hist = CONV_WIDTH - 1
    pad = 8

    @pl.when(t == 0)
    def _():
        u_scr[pad - hist:pad, :] = cbuf_ref[0]
        hc_scr[...] = h0_ref[0]

    xb = x_ref[...].astype(BF16)
    gu = jnp.dot(xb, win_ref[...], preferred_element_type=F32)
    gate = gu[:, :LRU_WIDTH]
    u = gu[:, LRU_WIDTH:]
    u_scr[pad:pad + tile, :] = u
    conv = cb_ref[...] + u * cw_ref[hist:hist + 1, :]
    for k in range(hist):
        conv = conv + u_scr[pad - hist + k:pad - hist + k + tile, :] * cw_ref[k:k + 1, :]
    nbuf_ref[0] = u_scr[pad + tile - hist:pad + tile, :]
    u_scr[pad - hist:pad, :] = u_scr[pad + tile - hist:pad + tile, :]

    for n in range(LRU_BLOCKS):
        cols = slice(n * LRU_BLOCK_W, (n + 1) * LRU_BLOCK_W)
        cn = conv[:, cols]
        g = jnp.dot(cn.astype(BF16), gw_ref[n], preferred_element_type=F32) + gb_ref[n]
        r = jax.nn.sigmoid(g[:, :LRU_BLOCK_W])
        ig = jax.nn.sigmoid(g[:, LRU_BLOCK_W:])
        log_a = -LRU_C * r * _softplus(-lam_ref[:, cols])
        a = jnp.exp(log_a)
        a_scr[:, cols] = a
        b_scr[:, cols] = jnp.sqrt(1.0 - a * a) * (ig * cn)

    row = lax.broadcasted_iota(jnp.int32, (8, LRU_WIDTH), 0)

    def scan_rows(j, h_prev):
        r0 = pl.multiple_of(j * 8, 8)
        a = a_scr[pl.ds(r0, 8), :]
        b = b_scr[pl.ds(r0, 8), :]
        for d in (1, 2, 4):
            a_sh = jnp.where(row >= d, pltpu.roll(a, d, 0), 1.0)
            b_sh = jnp.where(row >= d, pltpu.roll(b, d, 0), 0.0)
            b = a * b_sh + b
            a = a * a_sh
        h = a * h_prev + b
        h_scr[pl.ds(r0, 8), :] = h
        return h[7:8, :]

    h_last = lax.fori_loop(0, tile // 8, scan_rows, hc_scr[...])
    hc_scr[...] = h_last
    hlast_ref[0] = h_last
    o_ref[...] = (h_scr[...] * jax.nn.gelu(gate)).astype(BF16)


def _rg_call(x, conv_buf, h0, w_in, conv_w, conv_b, gate_w, gate_b, lam, *, n_seq, seq_len, tile,
             row_offset):
    n_t = seq_len // tile
    off = row_offset // tile
    kern = functools.partial(_rg_kernel, tile=tile)
    o, nbuf, hlast = pl.pallas_call(
        kern,
        out_shape=(jax.ShapeDtypeStruct((n_seq * seq_len, D_MODEL), BF16),
                   jax.ShapeDtypeStruct((n_seq, CONV_WIDTH - 1, LRU_WIDTH), F32),
                   jax.ShapeDtypeStruct((n_seq, 1, LRU_WIDTH), F32)),
        grid=(n_seq, n_t),
        in_specs=[pl.BlockSpec((tile, D_MODEL), lambda b, t: (off + b * n_t + t, 0)),
                  pl.BlockSpec((1, CONV_WIDTH - 1, LRU_WIDTH), lambda b, t: (b, 0, 0)),
                  pl.BlockSpec((1, 1, LRU_WIDTH), lambda b, t: (b, 0, 0)),
                  _resident((D_MODEL, 2 * LRU_WIDTH)),
                  _resident((CONV_WIDTH, LRU_WIDTH)),
                  _resident((1, LRU_WIDTH)),
                  _resident((LRU_BLOCKS, LRU_BLOCK_W, 2 * LRU_BLOCK_W)),
                  _resident((LRU_BLOCKS, 1, 2 * LRU_BLOCK_W)),
                  _resident((1, LRU_WIDTH))],
        out_specs=(pl.BlockSpec((tile, D_MODEL), lambda b, t: (b * n_t + t, 0)),
                   pl.BlockSpec((1, CONV_WIDTH - 1, LRU_WIDTH), lambda b, t: (b, 0, 0)),
                   pl.BlockSpec((1, 1, LRU_WIDTH), lambda b, t: (b, 0, 0))),
        scratch_shapes=[pltpu.VMEM((tile + 8, LRU_WIDTH), F32),
                        pltpu.VMEM((tile, LRU_WIDTH), F32),
                        pltpu.VMEM((tile, LRU_WIDTH), F32),
                        pltpu.VMEM((tile, LRU_WIDTH), F32),
                        pltpu.VMEM((1, LRU_WIDTH), F32)],
        compiler_params=_params(("parallel", "arbitrary")),
        name="rg_mixer",
    )(x, conv_buf, h0, w_in, conv_w, conv_b, gate_w, gate_b, lam)
    return o, nbuf, hlast[:, 0, :]


def _swa_proj_kernel(x_ref, w_ref, q_ref, k_ref, v_ref, kb_ref, vb_ref):
    xb = x_ref[...].astype(BF16)
    qkv = jnp.dot(xb, w_ref[...], preferred_element_type=F32)
    kvd = SWA_KV_HEADS * HEAD_DIM
    q_ref[...] = qkv[:, :D_MODEL].astype(BF16)
    k = qkv[:, D_MODEL:D_MODEL + kvd]
    v = qkv[:, D_MODEL + kvd:]
    k_ref[...] = k
    v_ref[...] = v
    kb_ref[...] = k.astype(BF16)
    vb_ref[...] = v.astype(BF16)


def _swa_proj_call(x, w):
    kvd = SWA_KV_HEADS * HEAD_DIM
    row = lambda i: (i, 0)
    return pl.pallas_call(
        _swa_proj_kernel,
        out_shape=(jax.ShapeDtypeStruct((N_TOK, D_MODEL), BF16),
                   jax.ShapeDtypeStruct((N_TOK, kvd), F32),
                   jax.ShapeDtypeStruct((N_TOK, kvd), F32),
                   jax.ShapeDtypeStruct((N_TOK, kvd), BF16),
                   jax.ShapeDtypeStruct((N_TOK, kvd), BF16)),
        grid=(N_TOK // TOKEN_TILE,),
        in_specs=[pl.BlockSpec((TOKEN_TILE, D_MODEL), row), _resident((D_MODEL, D_MODEL + 2 * kvd))],
        out_specs=(pl.BlockSpec((TOKEN_TILE, D_MODEL), row),
                   pl.BlockSpec((TOKEN_TILE, kvd), row), pl.BlockSpec((TOKEN_TILE, kvd), row),
                   pl.BlockSpec((TOKEN_TILE, kvd), row), pl.BlockSpec((TOKEN_TILE, kvd), row)),
        compiler_params=_params(("parallel",)),
        name="swa_proj",
    )(x, w)


def _swa_attn_kernel(q_ref, k_ref, v_ref, bias_ref, sink_ref, o_ref, *, tile, n_invalid):
    ti = pl.program_id(1)
    rows = SWA_GROUP * HEADS_PER_SLAB * CHUNK
    lane_q = lax.broadcasted_iota(jnp.int32, (CHUNK, LANES), 1)
    low_q = lane_q < HEAD_DIM
    key_idx = lax.broadcasted_iota(jnp.int32, (rows, SWA_KEYS), 1)
    for c in range(tile // CHUNK):
        row0 = pl.multiple_of(ti * tile + c * CHUNK, CHUNK)
        valid = key_idx + row0 >= n_invalid
        for t in range(SWA_KV_HEADS // HEADS_PER_SLAB):
            kv_cols = slice(t * LANES, (t + 1) * LANES)
            kw = k_ref[0, pl.ds(row0, SWA_KEYS), kv_cols]
            vw = v_ref[0, pl.ds(row0, SWA_KEYS), kv_cols]
            parts = []
            for g in range(SWA_GROUP):
                s_idx = SWA_GROUP * t + g
                slab = q_ref[0, c * CHUNK:(c + 1) * CHUNK, s_idx * LANES:(s_idx + 1) * LANES]
                zero = jnp.zeros_like(slab)
                parts.append(jnp.where(low_q, slab, zero))
                parts.append(jnp.where(low_q, zero, slab))
            lhs = jnp.concatenate(parts, axis=0)
            s = lax.dot_general(lhs, kw, (((1,), (1,)), ((), ())), preferred_element_type=F32)
            s = s + bias_ref[t]
            s = jnp.where(valid, s, -jnp.inf)
            sink = sink_ref[t]
            m = jnp.maximum(jnp.max(s, axis=1, keepdims=True), sink)
            p = jnp.exp(s - m)
            denom = jnp.sum(p, axis=1, keepdims=True) + jnp.exp(sink - m)
            out = jnp.dot(p.astype(BF16), vw, preferred_element_type=F32) / denom
            for g in range(SWA_GROUP):
                s_idx = SWA_GROUP * t + g
                lo = out[(2 * g) * CHUNK:(2 * g + 1) * CHUNK, :]
                hi = out[(2 * g + 1) * CHUNK:(2 * g + 2) * CHUNK, :]
                o_ref[0, c * CHUNK:(c + 1) * CHUNK, s_idx * LANES:(s_idx + 1) * LANES] = (
                    jnp.where(low_q, lo, hi).astype(BF16))


def _swa_attn_call(q, k_win, v_win, bias, sink, *, n_seq, seq_len, tile, n_invalid):
    kvd = SWA_KV_HEADS * HEAD_DIM
    n_keys = k_win.shape[1]
    rows = SWA_GROUP * HEADS_PER_SLAB * CHUNK
    n_kv_slabs = SWA_KV_HEADS // HEADS_PER_SLAB
    kern = functools.partial(_swa_attn_kernel, tile=tile, n_invalid=n_invalid)
    return pl.pallas_call(
        kern,
        out_shape=jax.ShapeDtypeStruct((n_seq, seq_len, D_MODEL), BF16),
        grid=(n_seq, seq_len // tile),
        in_specs=[pl.BlockSpec((1, tile, D_MODEL), lambda b, t: (b, t, 0)),
                  pl.BlockSpec((1, n_keys, kvd), lambda b, t: (b, 0, 0)),
                  pl.BlockSpec((1, n_keys, kvd), lambda b, t: (b, 0, 0)),
                  _resident((n_kv_slabs, rows, SWA_KEYS)),
                  _resident((n_kv_slabs, rows, 1))],
        out_specs=pl.BlockSpec((1, tile, D_MODEL), lambda b, t: (b, t, 0)),
        compiler_params=_params(("parallel", "arbitrary")),
        name="swa_attn",
    )(q, k_win, v_win, bias, sink)


def _t5_bucket(rel):
    half = N_BUCKETS // 2
    max_exact = half // 2
    n = jnp.abs(rel)
    n_f = jnp.maximum(n, 1).astype(jnp.float32)
    large = max_exact + (jnp.log(n_f / max_exact) / math.log(MAX_DISTANCE / max_exact)
                         * (half - max_exact)).astype(jnp.int32)
    large = jnp.minimum(large, half - 1)
    return jnp.where(rel > 0, half, 0) + jnp.where(n < max_exact, n, large)


_SWA_HEAD_ORDER = [8 * t + 4 * p + g for t in range(2) for g in range(4) for p in range(2)]


def _swa_row_tables(table, sinks):
    rel = jnp.arange(SWA_KEYS)[None, :] - WINDOW - jnp.arange(CHUNK)[:, None]
    bias = jnp.transpose(table[_t5_bucket(rel)].astype(F32), (2, 0, 1))
    order = jnp.array(_SWA_HEAD_ORDER)
    rows = SWA_GROUP * HEADS_PER_SLAB * CHUNK
    bias = bias[order].reshape(2, rows, SWA_KEYS)
    sink = jnp.repeat(sinks.astype(F32)[order], CHUNK).reshape(2, rows, 1)
    return bias, sink


def _fox_proj_kernel(x_ref, w_ref, wvt_ref, bf_ref, q_ref, k_ref, v_ref, kb_ref, vt_ref, lf_ref):
    xb = x_ref[...].astype(BF16)
    proj = jnp.dot(xb, w_ref[...], preferred_element_type=F32)
    k = proj[:, D_MODEL:2 * D_MODEL]
    k_ref[...] = k
    v_ref[...] = proj[:, 2 * D_MODEL:3 * D_MODEL]
    vt = lax.dot_general(wvt_ref[...], xb, (((1,), (1,)), ((), ())), preferred_element_type=F32)
    for s in range(N_SLABS):
        cols = slice(s * LANES, (s + 1) * LANES)
        q_ref[s] = proj[:, cols].astype(BF16)
        kb_ref[s] = k[:, cols].astype(BF16)
        for j in range(TOKEN_TILE // FOX_TK):
            vt_ref[s, j] = vt[cols, j * FOX_TK:(j + 1) * FOX_TK].astype(BF16)
    z = proj[:, 3 * D_MODEL:] + bf_ref[...]
    lf_ref[...] = -_softplus(-z)


def _fox_proj_call(x, w, w_vt, b_f):
    row = lambda i: (i, 0)
    wide = pl.BlockSpec((TOKEN_TILE, D_MODEL), row)
    slabs = pl.BlockSpec((N_SLABS, TOKEN_TILE, LANES), lambda i: (0, i, 0))
    kb_per_tile = TOKEN_TILE // FOX_TK
    return pl.pallas_call(
        _fox_proj_kernel,
        out_shape=(jax.ShapeDtypeStruct((N_SLABS, N_TOK, LANES), BF16),
                   jax.ShapeDtypeStruct((N_TOK, D_MODEL), F32),
                   jax.ShapeDtypeStruct((N_TOK, D_MODEL), F32),
                   jax.ShapeDtypeStruct((N_SLABS, N_TOK, LANES), BF16),
                   jax.ShapeDtypeStruct((N_SLABS, N_TOK // FOX_TK, LANES, FOX_TK), BF16),
                   jax.ShapeDtypeStruct((N_TOK, LANES), F32)),
        grid=(N_TOK // TOKEN_TILE,),
        in_specs=[wide, _resident((D_MODEL, 3 * D_MODEL + LANES)), _resident((D_MODEL, D_MODEL)),
                  _resident((1, LANES))],
        out_specs=(slabs, wide, wide, slabs,
                   pl.BlockSpec((N_SLABS, kb_per_tile, LANES, FOX_TK), lambda i: (0, i, 0, 0)),
                   pl.BlockSpec((TOKEN_TILE, LANES), row)),
        compiler_params=_params(("parallel",)),
        name="fox_proj",
    )(x, w, w_vt, b_f)


def _split3(x):
    hi = x.astype(BF16)
    r1 = x - hi.astype(F32)
    mid = r1.astype(BF16)
    lo = (r1 - mid.astype(F32)).astype(BF16)
    return hi, mid, lo


def _fox_bias_kernel(lf_ref, pq_ref, pk_ref, cq_ref, ck_ref, bq_ref, bk_ref, carry_scr):
    t = pl.program_id(1)

    @pl.when(t == 0)
    def _():
        carry_scr[...] = jnp.zeros_like(carry_scr)

    r = lax.broadcasted_iota(jnp.int32, (BIAS_TILE, BIAS_TILE), 0)
    cidx = lax.broadcasted_iota(jnp.int32, (BIAS_TILE, BIAS_TILE), 1)
    tri = (cidx <= r).astype(BF16)
    c = carry_scr[...]
    for part in _split3(lf_ref[0]):
        c = c + jnp.dot(tri, part, preferred_element_type=F32)
    carry_scr[...] = c[BIAS_TILE - 1:BIAS_TILE, :]
    bq = cq_ref[...]
    bk = ck_ref[...]
    for i, part in enumerate(_split3(c)):
        bq = bq + jnp.dot(part, pq_ref[i], preferred_element_type=F32)
        bk = bk + jnp.dot(part, pk_ref[i], preferred_element_type=F32)
    bq_ref[0] = bq.astype(BF16)
    bk_ref[0] = bk.astype(BF16)


def _fox_bias_tables():
    pq = [[[0.0] * LANES for _ in range(LANES)] for _ in range(3)]
    pk = [[[0.0] * LANES for _ in range(LANES)] for _ in range(3)]
    cq = [0.0] * LANES
    ck = [0.0] * LANES
    for i in range(3):
        for h in range(N_HEADS):
            pk[i][h][N_HEADS * i + h] = -1.0
            cq[N_HEADS * i + h] = 1.0
            pq[i][h][N_HEADS * (3 + i) + h] = 1.0
            ck[N_HEADS * (3 + i) + h] = 1.0
    return (jnp.array(pq, BF16), jnp.array(pk, BF16),
            jnp.array([cq], F32), jnp.array([ck], F32))


def _fox_bias_call(logf):
    n_seq, n_rows, _ = logf.shape
    pq, pk, cq, ck = _fox_bias_tables()
    blk = pl.BlockSpec((1, BIAS_TILE, LANES), lambda b, t: (b, t, 0))
    return pl.pallas_call(
        _fox_bias_kernel,
        out_shape=(jax.ShapeDtypeStruct((n_seq, n_rows, LANES), BF16),
                   jax.ShapeDtypeStruct((n_seq, n_rows, LANES), BF16)),
        grid=(n_seq, n_rows // BIAS_TILE),
        in_specs=[blk, _resident((3, LANES, LANES)), _resident((3, LANES, LANES)),
                  _resident((1, LANES)), _resident((1, LANES))],
        out_specs=(blk, blk),
        scratch_shapes=[pltpu.VMEM((1, LANES), F32)],
        compiler_params=_params(("parallel", "arbitrary")),
        name="fox_bias",
    )(logf, pq, pk, cq, ck)


def _fox_attn_kernel(q_ref, bq_ref, k_ref, bk_ref, vt_ref, o_ref,
                     qaug_scr, m_scr, acc_scr, s_even, s_odd, smax_even, smax_odd,
                     *, tq, tk, q_pos0):
    qi = pl.program_id(1)
    first_q = q_pos0 + qi * tq
    n_full = first_q // tk
    lane = lax.broadcasted_iota(jnp.int32, (tq, LANES), 1)
    low_lane = lane < HEAD_DIM
    bias_lane = lane < 6 * N_HEADS
    k_off = lax.broadcasted_iota(jnp.int32, (tk, tq), 0)
    q_pos = first_q + lax.broadcasted_iota(jnp.int32, (tk, tq), 1)
    bq = bq_ref[0]
    zero = jnp.zeros_like(bq)
    ones_rows = jnp.ones((FOX_ONES_ROWS, tk), BF16)

    for head in range(N_HEADS):
        q_slab = q_ref[head // HEADS_PER_SLAB]
        own_half = low_lane if head % HEADS_PER_SLAB == 0 else jnp.logical_not(low_lane)
        sel = jnp.logical_and((lane & (N_HEADS - 1)) == head, bias_lane)
        qaug_scr[head] = jnp.concatenate([jnp.where(own_half, q_slab, zero),
                                          jnp.where(sel, bq, zero)], axis=1)
    m_scr[...] = jnp.full_like(m_scr, -jnp.inf)
    acc_scr[...] = jnp.zeros_like(acc_scr)
    slots = ((s_even, smax_even), (s_odd, smax_odd))

    def key_block(kb, masked):
        r0 = pl.multiple_of(kb * tk, tk)
        bk_blk = bk_ref[0, pl.ds(r0, tk), :]

        def scores(hp):
            s_buf, smax_buf = slots[hp % 2]
            k_aug = jnp.concatenate([k_ref[hp, pl.ds(r0, tk), :], bk_blk], axis=1)
            for par in range(HEADS_PER_SLAB):
                s = lax.dot_general(k_aug, qaug_scr[hp * HEADS_PER_SLAB + par],
                                    (((1,), (1,)), ((), ())), preferred_element_type=F32)
                if masked:
                    s = jnp.where(kb * tk + k_off <= q_pos, s, -jnp.inf)
                s_buf[par] = s
                smax_buf[par] = jnp.max(s, axis=0, keepdims=True)

        def absorb(hp):
            s_buf, smax_buf = slots[hp % 2]
            vt = vt_ref[hp, kb]
            for par in range(HEADS_PER_SLAB):
                head = hp * HEADS_PER_SLAB + par
                m = m_scr[head]
                m_new = jnp.maximum(m, smax_buf[par])
                p = jnp.exp(s_buf[par] - m_new).astype(BF16)
                vt_aug = jnp.concatenate([vt[par * HEAD_DIM:(par + 1) * HEAD_DIM, :], ones_rows], axis=0)
                acc_scr[head] = (jnp.exp(m - m_new) * acc_scr[head]
                                 + jnp.dot(vt_aug, p, preferred_element_type=F32))
                m_scr[head] = m_new

        scores(0)
        for hp in range(N_SLABS):
            if hp + 1 < N_SLABS:
                scores(hp + 1)
            absorb(hp)

    def full_block(kb, carry):
        key_block(kb, False)
        return carry

    lax.fori_loop(0, n_full, full_block, 0)
    key_block(n_full, True)

    for hp in range(N_SLABS):
        halves = []
        for par in range(HEADS_PER_SLAB):
            acc = acc_scr[hp * HEADS_PER_SLAB + par]
            halves.append(acc[:HEAD_DIM] / acc[HEAD_DIM:HEAD_DIM + 1])
        o_ref[hp] = jnp.concatenate(halves, axis=0).T.astype(BF16)


def _fox_attn_call(q, bq, k, bk, vt, *, n_seq, n_q, n_keys, tq, tk, q_pos0, q_row0):
    assert tq <= tk and q_pos0 % tq == 0 and q_row0 % tq == 0 and n_keys % tk == 0
    kern = functools.partial(_fox_attn_kernel, tq=tq, tk=tk, q_pos0=q_pos0)
    n_qt = n_q // tq
    q_blk0 = q_pos0 // tq
    q_off = q_row0 // tq
    n_kb = n_keys // tk
    return pl.pallas_call(
        kern,
        out_shape=jax.ShapeDtypeStruct((N_SLABS, n_seq * n_q, LANES), BF16),
        grid=(n_seq, n_qt),
        in_specs=[pl.BlockSpec((N_SLABS, tq, LANES), lambda b, i: (0, q_off + b * n_qt + i, 0)),
                  pl.BlockSpec((1, tq, LANES), lambda b, i: (b, q_blk0 + i, 0)),
                  pl.BlockSpec((N_SLABS, n_keys, LANES), lambda b, i: (0, b, 0)),
                  pl.BlockSpec((1, n_keys, LANES), lambda b, i: (b, 0, 0)),
                  pl.BlockSpec((N_SLABS, n_kb, LANES, tk), lambda b, i: (0, b, 0, 0))],
        out_specs=pl.BlockSpec((N_SLABS, tq, LANES), lambda b, i: (0, b * n_qt + i, 0)),
        scratch_shapes=[pltpu.VMEM((N_HEADS, tq, 2 * LANES), BF16),
                        pltpu.VMEM((N_HEADS, 1, tq), F32),
                        pltpu.VMEM((N_HEADS, HEAD_DIM + FOX_ONES_ROWS, tq), F32),
                        pltpu.VMEM((HEADS_PER_SLAB, tk, tq), F32),
                        pltpu.VMEM((HEADS_PER_SLAB, tk, tq), F32),
                        pltpu.VMEM((HEADS_PER_SLAB, 1, tq), F32),
                        pltpu.VMEM((HEADS_PER_SLAB, 1, tq), F32)],
        compiler_params=_params(("parallel", "arbitrary")),
        name="fox_attn",
    )(q, bq, k, bk, vt)


def _rg_layer(x, conv_state, h_state, w_in, conv_w, conv_b, gate_w, gate_b, lam):
    w = (w_in.astype(BF16), conv_w, conv_b.reshape(1, LRU_WIDTH), gate_w.astype(BF16),
         gate_b.reshape(LRU_BLOCKS, 1, 2 * LRU_BLOCK_W), lam.reshape(1, LRU_WIDTH))
    o_p, nb_p, h_p = _rg_call(
        x, jnp.zeros((BATCH, CONV_WIDTH - 1, LRU_WIDTH), F32), jnp.zeros((BATCH, 1, LRU_WIDTH), F32),
        *w, n_seq=BATCH, seq_len=SEQ, tile=RG_TILE, row_offset=0)
    o_s, nb_s, h_s = _rg_call(
        x, conv_state, h_state.reshape(DEC_BATCH, 1, LRU_WIDTH),
        *w, n_seq=DEC_BATCH, seq_len=DEC_SEQ, tile=DEC_SEQ, row_offset=N_PROMPT)
    return jnp.concatenate([o_p, o_s], axis=0), (nb_p, nb_s, h_p, h_s)


def _swa_layer(x, k_cache, v_cache, w_qkv, sinks, table):
    kvd = SWA_KV_HEADS * HEAD_DIM
    order = jnp.array(_SWA_HEAD_ORDER)
    w_q = (w_qkv[:, :D_MODEL] * ATTN_SCALE).reshape(D_MODEL, N_HEADS, HEAD_DIM)[:, order]
    w = jnp.concatenate([w_q.reshape(D_MODEL, D_MODEL), w_qkv[:, D_MODEL:]], axis=1).astype(BF16)
    q, k, v, kb, vb = _swa_proj_call(x, w)
    bias, sink = _swa_row_tables(table, sinks)

    q_p = q[:N_PROMPT].reshape(BATCH, SEQ, D_MODEL)
    pad = ((0, 0), (WINDOW, 0), (0, 0))
    k_p = jnp.pad(kb[:N_PROMPT].reshape(BATCH, SEQ, kvd), pad)
    v_p = jnp.pad(vb[:N_PROMPT].reshape(BATCH, SEQ, kvd), pad)
    o_p = _swa_attn_call(q_p, k_p, v_p, bias, sink, n_seq=BATCH, seq_len=SEQ, tile=SWA_TILE,
                         n_invalid=WINDOW)

    q_s = q[N_PROMPT:].reshape(DEC_BATCH, DEC_SEQ, D_MODEL)
    kc = k_cache.reshape(DEC_BATCH, WINDOW, kvd)
    vc = v_cache.reshape(DEC_BATCH, WINDOW, kvd)
    k_s = jnp.concatenate([kc.astype(BF16), kb[N_PROMPT:].reshape(DEC_BATCH, DEC_SEQ, kvd)], axis=1)
    v_s = jnp.concatenate([vc.astype(BF16), vb[N_PROMPT:].reshape(DEC_BATCH, DEC_SEQ, kvd)], axis=1)
    o_s = _swa_attn_call(q_s, k_s, v_s, bias, sink, n_seq=DEC_BATCH, seq_len=DEC_SEQ, tile=DEC_SEQ,
                         n_invalid=0)
    o = jnp.concatenate([o_p.reshape(N_PROMPT, D_MODEL), o_s.reshape(N_SAMPLE, D_MODEL)], axis=0)

    def tails(new, cache):
        new_p = new[:N_PROMPT].reshape(BATCH, SEQ, SWA_KV_HEADS, HEAD_DIM)[:, -WINDOW:]
        new_s = new[N_PROMPT:].reshape(DEC_BATCH, DEC_SEQ, SWA_KV_HEADS, HEAD_DIM)
        return new_p, jnp.concatenate([cache, new_s], axis=1)[:, -WINDOW:]

    k_tp, k_ts = tails(k, k_cache)
    v_tp, v_ts = tails(v, v_cache)
    return o, (k_tp, k_ts, v_tp, v_ts)


def _fox_layer(x, k_cache, v_cache, logf_cache, w_in, b_f):
    hd = N_HEADS * HEAD_DIM
    w = jnp.concatenate([w_in[:, :hd] * ATTN_SCALE, w_in[:, hd:3 * hd],
                         jnp.pad(w_in[:, 3 * hd:], ((0, 0), (0, LANES - N_HEADS)))], axis=1).astype(BF16)
    bf = jnp.pad(b_f.astype(F32), (0, LANES - N_HEADS)).reshape(1, LANES)
    w_vt = w_in[:, 2 * hd:3 * hd].T.astype(BF16)
    q, k, v, kb, vt, lf = _fox_proj_call(x, w, w_vt, bf)

    def seqs(a, lo, n_seq, seq_len):
        return a[lo:lo + n_seq * seq_len].reshape(n_seq, seq_len, a.shape[-1])

    bq_p, bk_p = _fox_bias_call(seqs(lf, 0, BATCH, SEQ))
    o_p = _fox_attn_call(q, bq_p, kb, bk_p, vt, n_seq=BATCH, n_q=SEQ, n_keys=SEQ,
                         tq=FOX_TQ, tk=FOX_TK, q_pos0=0, q_row0=0)

    tail = FOX_SAMPLE_KEYS - PAST_LEN - DEC_SEQ
    n_kb = FOX_SAMPLE_KEYS // FOX_TK
    lf_cache = jnp.pad(logf_cache.astype(F32), ((0, 0), (0, 0), (0, LANES - N_HEADS)))
    lf_s = jnp.concatenate([lf_cache, seqs(lf, N_PROMPT, DEC_BATCH, DEC_SEQ)], axis=1)
    bq_s, bk_s = _fox_bias_call(jnp.pad(lf_s, ((0, 0), (0, tail), (0, 0))))
    k_old = k_cache.astype(BF16).reshape(DEC_BATCH, PAST_LEN, N_SLABS, LANES).transpose(2, 0, 1, 3)
    k_new = kb[:, N_PROMPT:].reshape(N_SLABS, DEC_BATCH, DEC_SEQ, LANES)
    k_s = jnp.pad(jnp.concatenate([k_old, k_new], axis=2), ((0, 0), (0, 0), (0, tail), (0, 0)))
    k_s = k_s.reshape(N_SLABS, DEC_BATCH * FOX_SAMPLE_KEYS, LANES)
    v_old = v_cache.astype(BF16).reshape(DEC_BATCH, PAST_LEN, N_SLABS, LANES).transpose(2, 0, 3, 1)
    per_blk = FOX_TK // DEC_SEQ
    v_new = vt[:, N_PROMPT // FOX_TK:].reshape(N_SLABS, N_SAMPLE // FOX_TK, LANES, per_blk, DEC_SEQ)
    v_new = v_new.transpose(0, 1, 3, 2, 4).reshape(N_SLABS, DEC_BATCH, LANES, DEC_SEQ)
    vt_s = jnp.pad(jnp.concatenate([v_old, v_new], axis=3), ((0, 0), (0, 0), (0, 0), (0, tail)))
    vt_s = vt_s.reshape(N_SLABS, DEC_BATCH, LANES, n_kb, FOX_TK).transpose(0, 1, 3, 2, 4)
    vt_s = vt_s.reshape(N_SLABS, DEC_BATCH * n_kb, LANES, FOX_TK)
    o_s = _fox_attn_call(q, bq_s, k_s, bk_s, vt_s, n_seq=DEC_BATCH, n_q=DEC_SEQ,
                         n_keys=FOX_SAMPLE_KEYS, tq=DEC_SEQ, tk=FOX_TK, q_pos0=PAST_LEN,
                         q_row0=N_PROMPT)
    o = jnp.concatenate([o_p, o_s], axis=1)

    def heads(a, lo, n_seq, seq_len):
        return a[lo:lo + n_seq * seq_len].reshape(n_seq, seq_len, N_HEADS, HEAD_DIM)

    outs = (heads(k, 0, BATCH, SEQ), heads(k, N_PROMPT, DEC_BATCH, DEC_SEQ),
            heads(v, 0, BATCH, SEQ), heads(v, N_PROMPT, DEC_BATCH, DEC_SEQ),
            seqs(lf, 0, BATCH, SEQ)[..., :N_HEADS], seqs(lf, N_PROMPT, DEC_BATCH, DEC_SEQ)[..., :N_HEADS])
    return o, outs


def kernel(x_prompt, x_sample, state_rg_conv, state_rg_h, cache_swa_k, cache_swa_v, cache_fox_k, cache_fox_v, cache_fox_logf, ln_gain, ln_bias, ffn_w_up, ffn_w_down, rg_w_in, rg_conv_w, rg_conv_b, rg_gate_w, rg_gate_b, rg_lambda, rg_w_out, swa_w_qkv, swa_sinks, swa_w_out, rel_bias_table, fox_w_in, fox_b_f, fox_w_out):
    x = jnp.concatenate([x_prompt.reshape(N_PROMPT, D_MODEL), x_sample.reshape(N_SAMPLE, D_MODEL)], axis=0)
    rg_out, swa_out, fox_out = [], [], []
    for i in range(DEPTH):
        kind, j = i % 3, i // 3
        if kind == 0:
            o, extra = _rg_layer(x, state_rg_conv[j], state_rg_h[j], rg_w_in[j], rg_conv_w[j],
                                 rg_conv_b[j], rg_gate_w[j], rg_gate_b[j], rg_lambda[j])
            rg_out.append(extra)
            w_out = rg_w_out[j]
        elif kind == 1:
            o, extra = _swa_layer(x, cache_swa_k[j], cache_swa_v[j], swa_w_qkv[j], swa_sinks[j],
                                  rel_bias_table)
            swa_out.append(extra)
            order = jnp.array(_SWA_HEAD_ORDER)
            w_out = swa_w_out[j].reshape(N_HEADS, HEAD_DIM, D_MODEL)[order].reshape(D_MODEL, D_MODEL)
        else:
            o, extra = _fox_layer(x, cache_fox_k[j], cache_fox_v[j], cache_fox_logf[j], fox_w_in[j],
                                  fox_b_f[j])
            fox_out.append(extra)
            w_out = fox_w_out[j]
        x = _post_call(x, o, w_out.astype(BF16), ffn_w_up[i].astype(BF16), ffn_w_down[i].astype(BF16),
                       ln_gain[i], ln_bias[i])

    def stack(items, idx):
        return jnp.stack([it[idx] for it in items])

    return (x[:N_PROMPT].reshape(BATCH, SEQ, D_MODEL), x[N_PROMPT:].reshape(DEC_BATCH, DEC_SEQ, D_MODEL),
            stack(rg_out, 0), stack(rg_out, 1), stack(rg_out, 2), stack(rg_out, 3),
            stack(swa_out, 0), stack(swa_out, 1), stack(swa_out, 2), stack(swa_out, 3),
            stack(fox_out, 0), stack(fox_out, 1), stack(fox_out, 2), stack(fox_out, 3),
            stack(fox_out, 4), stack(fox_out, 5))
```

```python
import functools
import math

import jax
import jax.numpy as jnp
from jax import lax
from jax.experimental import pallas as pl
from jax.experimental.pallas import tpu as pltpu

F32 = jnp.float32
BF16 = jnp.bfloat16

D_MODEL = 1024
BATCH = 4
SEQ = 4096
DEPTH = 4
DEC_BATCH = 16
DEC_SEQ = 64
PAST_LEN = 1024
CHUNK = 64
D_FF = 4 * D_MODEL
HEAD_DIM = 64
N_HEADS = 16
SWA_KV_HEADS = 4
SWA_GROUP = 4
WINDOW = 128
LRU_WIDTH = D_MODEL
LRU_BLOCKS = 4
LRU_BLOCK_W = LRU_WIDTH // LRU_BLOCKS
CONV_WIDTH = 4
LRU_C = 8.0
N_BUCKETS = 32
MAX_DISTANCE = 128
ALPHA = (2.0 * DEPTH) ** 0.25
LN_EPS = 1e-5
ATTN_SCALE = HEAD_DIM ** -0.5

N_PROMPT = BATCH * SEQ
N_SAMPLE = DEC_BATCH * DEC_SEQ

LANES = 128
HEADS_PER_SLAB = LANES // HEAD_DIM
N_SLABS = D_MODEL // LANES

TOKEN_TILE = 512
FF_CHUNK = 1024
RG_TILE = 512
SWA_TILE = 512
SWA_KEYS = WINDOW + CHUNK
FOX_TQ = 256
FOX_TK = 512
FOX_SAMPLE_KEYS = 1536
FOX_ONES_ROWS = 16
BIAS_TILE = 512
VMEM_LIMIT = 56 * 1024 * 1024


def _resident(shape, layer=None):
    zeros = (0,) * len(shape)
    if layer is None:
        return pl.BlockSpec(shape, lambda *_: zeros, pipeline_mode=pl.Buffered(1))
    return pl.BlockSpec((None,) + tuple(shape), lambda *_: (layer,) + zeros,
                        pipeline_mode=pl.Buffered(1))


def _params(semantics):
    return pltpu.CompilerParams(dimension_semantics=semantics, vmem_limit_bytes=VMEM_LIMIT)


def _softplus(x):
    return jnp.maximum(x, 0.0) + jnp.log1p(jnp.exp(-jnp.abs(x)))


def _layer_norm(z, g, b):
    mu = jnp.mean(z, axis=-1, keepdims=True)
    zc = z - mu
    var = jnp.mean(zc * zc, axis=-1, keepdims=True)
    return zc * lax.rsqrt(var + LN_EPS) * g + b


def _post_kernel(x_ref, o_ref, wout_ref, wup_ref, wdn_ref, g_ref, b_ref, out_ref, *, slab_major):
    x = x_ref[...]
    if slab_major:
        o = jnp.concatenate([o_ref[s] for s in range(N_SLABS)], axis=1)
    else:
        o = o_ref[...]
    y = jnp.dot(o, wout_ref[...], preferred_element_type=F32)
    x1 = _layer_norm(ALPHA * x + y, g_ref[0:1, :], b_ref[0:1, :])
    x1b = x1.astype(BF16)
    acc = jnp.zeros_like(x1)
    for c in range(D_FF // FF_CHUNK):
        cols = slice(c * FF_CHUNK, (c + 1) * FF_CHUNK)
        h = jnp.dot(x1b, wup_ref[:, cols], preferred_element_type=F32)
        a = jnp.square(jnp.maximum(h, 0.0)).astype(BF16)
        acc = acc + jnp.dot(a, wdn_ref[cols, :], preferred_element_type=F32)
    out_ref[...] = _layer_norm(ALPHA * x1 + acc, g_ref[1:2, :], b_ref[1:2, :])


def _post_call(x, o, w_out, mixer_idx, w_up, w_down, gain, bias, layer):
    n = x.shape[0]
    tile = pl.BlockSpec((TOKEN_TILE, D_MODEL), lambda i: (i, 0))
    slab_major = o.ndim == 3
    o_tile = pl.BlockSpec((N_SLABS, TOKEN_TILE, LANES), lambda i: (0, i, 0)) if slab_major else tile
    return pl.pallas_call(
        functools.partial(_post_kernel, slab_major=slab_major),
        out_shape=jax.ShapeDtypeStruct((n, D_MODEL), F32),
        grid=(n // TOKEN_TILE,),
        in_specs=[tile, o_tile, _resident((D_MODEL, D_MODEL), mixer_idx),
                  _resident((D_MODEL, D_FF), layer), _resident((D_FF, D_MODEL), layer),
                  _resident((2, D_MODEL), layer), _resident((2, D_MODEL), layer)],
        out_specs=tile,
        compiler_params=_params(("parallel",)),
        name="post",
    )(x, o, w_out, w_up, w_down, gain, bias)


def _rg_kernel(x_ref, cbuf_ref, h0_ref, win_ref, cw_ref, cb_ref, gw_ref, gb_ref, lam_ref,
               o_ref, nbuf_ref, hlast_ref, u_scr, a_scr, b_scr, h_scr, hc_scr, *, tile):
    t = pl.program_id(1)
    hist = CONV_WIDTH - 1
    pad = 8

    @pl.when(t == 0)
    def _():
        u_scr[pad - hist:pad, :] = cbuf_ref[0]
        hc_scr[...] = h0_ref[0]

    xb = x_ref[...].astype(BF16)
    gu = jnp.dot(xb, win_ref[...], preferred_element_type=F32)
    gate = gu[:, :LRU_WIDTH]
    u = gu[:, LRU_WIDTH:]
    u_scr[pad:pad + tile, :] = u
    conv = cb_ref[...] + u * cw_ref[hist:hist + 1, :]
    for k in range(hist):
        conv = conv + u_scr[pad - hist + k:pad - hist + k + tile, :] * cw_ref[k:k + 1, :]
    nbuf_ref[0] = u_scr[pad + tile - hist:pad + tile, :]
    u_scr[pad - hist:pad, :] = u_scr[pad + tile - hist:pad + tile, :]

    for n in range(LRU_BLOCKS):
        cols = slice(n * LRU_BLOCK_W, (n + 1) * LRU_BLOCK_W)
        cn = conv[:, cols]
        g = jnp.dot(cn.astype(BF16), gw_ref[n], preferred_element_type=F32) + gb_ref[n]
        r = jax.nn.sigmoid(g[:, :LRU_BLOCK_W])
        ig = jax.nn.sigmoid(g[:, LRU_BLOCK_W:])
        log_a = -LRU_C * r * _softplus(-lam_ref[:, cols])
        a = jnp.exp(log_a)
        a_scr[:, cols] = a
        b_scr[:, cols] = jnp.sqrt(1.0 - a * a) * (ig * cn)

    row = lax.broadcasted_iota(jnp.int32, (8, LRU_WIDTH), 0)

    def scan_rows(j, h_prev):
        r0 = pl.multiple_of(j * 8, 8)
        a = a_scr[pl.ds(r0, 8), :]
        b = b_scr[pl.ds(r0, 8), :]
        for d in (1, 2, 4):
            a_sh = jnp.where(row >= d, pltpu.roll(a, d, 0), 1.0)
            b_sh = jnp.where(row >= d, pltpu.roll(b, d, 0), 0.0)
            b = a * b_sh + b
            a = a * a_sh
        h = a * h_prev + b
        h_scr[pl.ds(r0, 8), :] = h
        return h[7:8, :]

    h_last = lax.fori_loop(0, tile // 8, scan_rows, hc_scr[...])
    hc_scr[...] = h_last
    hlast_ref[0] = h_last
    o_ref[...] = (h_scr[...] * jax.nn.gelu(gate)).astype(BF16)


def _rg_call(x, conv_buf, h0, w_in, gate_w, small, layer, *, n_seq, seq_len, tile):
    conv_w, conv_b, gate_b, lam = small
    n_t = seq_len // tile
    kern = functools.partial(_rg_kernel, tile=tile)
    o, nbuf, hlast = pl.pallas_call(
        kern,
        out_shape=(jax.ShapeDtypeStruct((n_seq * seq_len, D_MODEL), BF16),
                   jax.ShapeDtypeStruct((n_seq, CONV_WIDTH - 1, LRU_WIDTH), F32),
                   jax.ShapeDtypeStruct((n_seq, 1, LRU_WIDTH), F32)),
        grid=(n_seq, n_t),
        in_specs=[pl.BlockSpec((tile, D_MODEL), lambda b, t: (b * n_t + t, 0)),
                  pl.BlockSpec((1, CONV_WIDTH - 1, LRU_WIDTH), lambda b, t: (b, 0, 0)),
                  pl.BlockSpec((1, 1, LRU_WIDTH), lambda b, t: (b, 0, 0)),
                  _resident((D_MODEL, 2 * LRU_WIDTH), layer),
                  _resident((CONV_WIDTH, LRU_WIDTH)),
                  _resident((1, LRU_WIDTH)),
                  _resident((LRU_BLOCKS, LRU_BLOCK_W, 2 * LRU_BLOCK_W), layer),
                  _resident((LRU_BLOCKS, 1, 2 * LRU_BLOCK_W)),
                  _resident((1, LRU_WIDTH))],
        out_specs=(pl.BlockSpec((tile, D_MODEL), lambda b, t: (b * n_t + t, 0)),
                   pl.BlockSpec((1, CONV_WIDTH - 1, LRU_WIDTH), lambda b, t: (b, 0, 0)),
                   pl.BlockSpec((1, 1, LRU_WIDTH), lambda b, t: (b, 0, 0))),
        scratch_shapes=[pltpu.VMEM((tile + 8, LRU_WIDTH), F32),
                        pltpu.VMEM((tile, LRU_WIDTH), F32),
                        pltpu.VMEM((tile, LRU_WIDTH), F32),
                        pltpu.VMEM((tile, LRU_WIDTH), F32),
                        pltpu.VMEM((1, LRU_WIDTH), F32)],
        compiler_params=_params(("parallel", "arbitrary")),
        name="rg_mixer",
    )(x, conv_buf, h0, w_in, conv_w, conv_b, gate_w, gate_b, lam)
    return o, nbuf, hlast[:, 0, :]


def _swa_proj_kernel(x_ref, w_ref, q_ref, k_ref, v_ref, kb_ref, vb_ref):
    xb = x_ref[...].astype(BF16)
    qkv = jnp.dot(xb, w_ref[...], preferred_element_type=F32)
    kvd = SWA_KV_HEADS * HEAD_DIM
    q_ref[...] = qkv[:, :D_MODEL].astype(BF16)
    k = qkv[:, D_MODEL:D_MODEL + kvd]
    v = qkv[:, D_MODEL + kvd:]
    k_ref[...] = k
    v_ref[...] = v
    kb_ref[...] = k.astype(BF16)
    vb_ref[...] = v.astype(BF16)


def _swa_proj_call(x, w):
    n = x.shape[0]
    kvd = SWA_KV_HEADS * HEAD_DIM
    row = lambda i: (i, 0)
    return pl.pallas_call(
        _swa_proj_kernel,
        out_shape=(jax.ShapeDtypeStruct((n, D_MODEL), BF16),
                   jax.ShapeDtypeStruct((n, kvd), F32),
                   jax.ShapeDtypeStruct((n, kvd), F32),
                   jax.ShapeDtypeStruct((n, kvd), BF16),
                   jax.ShapeDtypeStruct((n, kvd), BF16)),
        grid=(n // TOKEN_TILE,),
        in_specs=[pl.BlockSpec((TOKEN_TILE, D_MODEL), row), _resident((D_MODEL, D_MODEL + 2 * kvd))],
        out_specs=(pl.BlockSpec((TOKEN_TILE, D_MODEL), row),
                   pl.BlockSpec((TOKEN_TILE, kvd), row), pl.BlockSpec((TOKEN_TILE, kvd), row),
                   pl.BlockSpec((TOKEN_TILE, kvd), row), pl.BlockSpec((TOKEN_TILE, kvd), row)),
        compiler_params=_params(("parallel",)),
        name="swa_proj",
    )(x, w)


def _swa_attn_kernel(q_ref, k_ref, v_ref, bias_ref, sink_ref, o_ref, *, tile, n_invalid):
    ti = pl.program_id(1)
    rows = SWA_GROUP * HEADS_PER_SLAB * CHUNK
    lane_q = lax.broadcasted_iota(jnp.int32, (CHUNK, LANES), 1)
    low_q = lane_q < HEAD_DIM
    key_idx = lax.broadcasted_iota(jnp.int32, (rows, SWA_KEYS), 1)
    for c in range(tile // CHUNK):
        row0 = pl.multiple_of(ti * tile + c * CHUNK, CHUNK)
        valid = key_idx + row0 >= n_invalid
        for t in range(SWA_KV_HEADS // HEADS_PER_SLAB):
            kv_cols = slice(t * LANES, (t + 1) * LANES)
            kw = k_ref[0, pl.ds(row0, SWA_KEYS), kv_cols]
            vw = v_ref[0, pl.ds(row0, SWA_KEYS), kv_cols]
            parts = []
            for g in range(SWA_GROUP):
                s_idx = SWA_GROUP * t + g
                slab = q_ref[0, c * CHUNK:(c + 1) * CHUNK, s_idx * LANES:(s_idx + 1) * LANES]
                zero = jnp.zeros_like(slab)
                parts.append(jnp.where(low_q, slab, zero))
                parts.append(jnp.where(low_q, zero, slab))
            lhs = jnp.concatenate(parts, axis=0)
            s = lax.dot_general(lhs, kw, (((1,), (1,)), ((), ())), preferred_element_type=F32)
            s = s + bias_ref[t]
            s = jnp.where(valid, s, -jnp.inf)
            sink = sink_ref[t]
            m = jnp.maximum(jnp.max(s, axis=1, keepdims=True), sink)
            p = jnp.exp(s - m)
            denom = jnp.sum(p, axis=1, keepdims=True) + jnp.exp(sink - m)
            out = jnp.dot(p.astype(BF16), vw, preferred_element_type=F32) / denom
            for g in range(SWA_GROUP):
                s_idx = SWA_GROUP * t + g
                lo = out[(2 * g) * CHUNK:(2 * g + 1) * CHUNK, :]
                hi = out[(2 * g + 1) * CHUNK:(2 * g + 2) * CHUNK, :]
                o_ref[0, c * CHUNK:(c + 1) * CHUNK, s_idx * LANES:(s_idx + 1) * LANES] = (
                    jnp.where(low_q, lo, hi).astype(BF16))


def _swa_attn_call(q, k_win, v_win, bias, sink, *, n_seq, seq_len, tile, n_invalid):
    kvd = SWA_KV_HEADS * HEAD_DIM
    n_keys = k_win.shape[1]
    rows = SWA_GROUP * HEADS_PER_SLAB * CHUNK
    n_kv_slabs = SWA_KV_HEADS // HEADS_PER_SLAB
    kern = functools.partial(_swa_attn_kernel, tile=tile, n_invalid=n_invalid)
    return pl.pallas_call(
        kern,
        out_shape=jax.ShapeDtypeStruct((n_seq, seq_len, D_MODEL), BF16),
        grid=(n_seq, seq_len // tile),
        in_specs=[pl.BlockSpec((1, tile, D_MODEL), lambda b, t: (b, t, 0)),
                  pl.BlockSpec((1, n_keys, kvd), lambda b, t: (b, 0, 0)),
                  pl.BlockSpec((1, n_keys, kvd), lambda b, t: (b, 0, 0)),
                  _resident((n_kv_slabs, rows, SWA_KEYS)),
                  _resident((n_kv_slabs, rows, 1))],
        out_specs=pl.BlockSpec((1, tile, D_MODEL), lambda b, t: (b, t, 0)),
        compiler_params=_params(("parallel", "arbitrary")),
        name="swa_attn",
    )(q, k_win, v_win, bias, sink)


def _t5_bucket(rel):
    half = N_BUCKETS // 2
    max_exact = half // 2
    n = jnp.abs(rel)
    n_f = jnp.maximum(n, 1).astype(jnp.float32)
    large = max_exact + (jnp.log(n_f / max_exact) / math.log(MAX_DISTANCE / max_exact)
                         * (half - max_exact)).astype(jnp.int32)
    large = jnp.minimum(large, half - 1)
    return jnp.where(rel > 0, half, 0) + jnp.where(n < max_exact, n, large)


_SWA_HEAD_ORDER = [8 * t + 4 * p + g for t in range(2) for g in range(4) for p in range(2)]


def _swa_row_tables(table, sinks):
    rel = jnp.arange(SWA_KEYS)[None, :] - WINDOW - jnp.arange(CHUNK)[:, None]
    bucket = _t5_bucket(rel)
    order = jnp.array(_SWA_HEAD_ORDER)
    tbl = table.astype(F32)[:, order]
    hit = bucket[None, None] == jnp.arange(N_BUCKETS)[:, None, None, None]
    bias = jnp.sum(jnp.where(hit, tbl[:, :, None, None], 0.0), axis=0)
    rows = SWA_GROUP * HEADS_PER_SLAB * CHUNK
    bias = bias.reshape(2, rows, SWA_KEYS)
    sink = jnp.repeat(sinks.astype(F32)[order], CHUNK).reshape(2, rows, 1)
    return bias, sink


def _fox_proj_kernel(x_ref, w_ref, wvt_ref, bf_ref, q_ref, k_ref, v_ref, kb_ref, vt_ref, lf_ref):
    xb = x_ref[...].astype(BF16)
    proj = jnp.dot(xb, w_ref[...], preferred_element_type=F32)
    k = proj[:, D_MODEL:2 * D_MODEL]
    k_ref[...] = k
    v_ref[...] = proj[:, 2 * D_MODEL:3 * D_MODEL]
    vt = lax.dot_general(wvt_ref[...], xb, (((1,), (1,)), ((), ())), preferred_element_type=F32)
    for s in range(N_SLABS):
        cols = slice(s * LANES, (s + 1) * LANES)
        q_ref[s] = proj[:, cols].astype(BF16)
        kb_ref[s] = k[:, cols].astype(BF16)
        for j in range(TOKEN_TILE // FOX_TK):
            vt_ref[s, j] = vt[cols, j * FOX_TK:(j + 1) * FOX_TK].astype(BF16)
    z = proj[:, 3 * D_MODEL:] + bf_ref[...]
    lf_ref[...] = -_softplus(-z)


def _fox_proj_call(x, w, w_vt, b_f):
    n = x.shape[0]
    row = lambda i: (i, 0)
    wide = pl.BlockSpec((TOKEN_TILE, D_MODEL), row)
    slabs = pl.BlockSpec((N_SLABS, TOKEN_TILE, LANES), lambda i: (0, i, 0))
    kb_per_tile = TOKEN_TILE // FOX_TK
    return pl.pallas_call(
        _fox_proj_kernel,
        out_shape=(jax.ShapeDtypeStruct((N_SLABS, n, LANES), BF16),
                   jax.ShapeDtypeStruct((n, D_MODEL), F32),
                   jax.ShapeDtypeStruct((n, D_MODEL), F32),
                   jax.ShapeDtypeStruct((N_SLABS, n, LANES), BF16),
                   jax.ShapeDtypeStruct((N_SLABS, n // FOX_TK, LANES, FOX_TK), BF16),
                   jax.ShapeDtypeStruct((n, LANES), F32)),
        grid=(n // TOKEN_TILE,),
        in_specs=[wide, _resident((D_MODEL, 3 * D_MODEL + LANES)), _resident((D_MODEL, D_MODEL)),
                  _resident((1, LANES))],
        out_specs=(slabs, wide, wide, slabs,
                   pl.BlockSpec((N_SLABS, kb_per_tile, LANES, FOX_TK), lambda i: (0, i, 0, 0)),
                   pl.BlockSpec((TOKEN_TILE, LANES), row)),
        compiler_params=_params(("parallel",)),
        name="fox_proj",
    )(x, w, w_vt, b_f)


def _split3(x):
    hi = x.astype(BF16)
    r1 = x - hi.astype(F32)
    mid = r1.astype(BF16)
    lo = (r1 - mid.astype(F32)).astype(BF16)
    return hi, mid, lo


def _fox_bias_kernel(lf_ref, place_ref, const_ref, bq_ref, bk_ref, carry_scr):
    t = pl.program_id(1)

    @pl.when(t == 0)
    def _():
        carry_scr[...] = jnp.zeros_like(carry_scr)

    r = lax.broadcasted_iota(jnp.int32, (BIAS_TILE, BIAS_TILE), 0)
    cidx = lax.broadcasted_iota(jnp.int32, (BIAS_TILE, BIAS_TILE), 1)
    tri = (cidx <= r).astype(BF16)
    tri3 = jnp.concatenate([tri, tri, tri], axis=1)
    parts = jnp.concatenate(_split3(lf_ref[0]), axis=0)
    c = carry_scr[...] + jnp.dot(tri3, parts, preferred_element_type=F32)
    carry_scr[...] = c[BIAS_TILE - 1:BIAS_TILE, :]
    c3 = jnp.concatenate(_split3(c), axis=1)
    both = const_ref[...] + jnp.dot(c3, place_ref[...], preferred_element_type=F32)
    bq_ref[0] = both[:, :LANES].astype(BF16)
    bk_ref[0] = both[:, LANES:].astype(BF16)


def _fox_bias_tables():
    place = [[0.0] * (2 * LANES) for _ in range(3 * LANES)]
    const = [0.0] * (2 * LANES)
    for i in range(3):
        for h in range(N_HEADS):
            const[N_HEADS * i + h] = 1.0
            place[LANES * i + h][N_HEADS * (3 + i) + h] = 1.0
            place[LANES * i + h][LANES + N_HEADS * i + h] = -1.0
            const[LANES + N_HEADS * (3 + i) + h] = 1.0
    return jnp.array(place, BF16), jnp.array([const], F32)


def _fox_bias_call(logf):
    n_seq, n_rows, _ = logf.shape
    place, const = _fox_bias_tables()
    blk = pl.BlockSpec((1, BIAS_TILE, LANES), lambda b, t: (b, t, 0))
    return pl.pallas_call(
        _fox_bias_kernel,
        out_shape=(jax.ShapeDtypeStruct((n_seq, n_rows, LANES), BF16),
                   jax.ShapeDtypeStruct((n_seq, n_rows, LANES), BF16)),
        grid=(n_seq, n_rows // BIAS_TILE),
        in_specs=[blk, _resident((3 * LANES, 2 * LANES)), _resident((1, 2 * LANES))],
        out_specs=(blk, blk),
        scratch_shapes=[pltpu.VMEM((1, LANES), F32)],
        compiler_params=_params(("parallel", "arbitrary")),
        name="fox_bias",
    )(logf, place, const)


def _fox_attn_kernel(q_ref, bq_ref, k_ref, bk_ref, vt_ref, o_ref,
                     qaug_scr, m_scr, acc_scr, s_even, s_odd, smax_even, smax_odd,
                     *, tq, tk, q_pos0):
    qi = pl.program_id(1)
    first_q = q_pos0 + qi * tq
    n_full = first_q // tk
    lane = lax.broadcasted_iota(jnp.int32, (tq, LANES), 1)
    low_lane = lane < HEAD_DIM
    bias_lane = lane < 6 * N_HEADS
    k_off = lax.broadcasted_iota(jnp.int32, (tk, tq), 0)
    q_pos = first_q + lax.broadcasted_iota(jnp.int32, (tk, tq), 1)
    bq = bq_ref[0]
    zero = jnp.zeros_like(bq)
    ones_rows = jnp.ones((FOX_ONES_ROWS, tk), BF16)

    for head in range(N_HEADS):
        q_slab = q_ref[head // HEADS_PER_SLAB]
        own_half = low_lane if head % HEADS_PER_SLAB == 0 else jnp.logical_not(low_lane)
        sel = jnp.logical_and((lane & (N_HEADS - 1)) == head, bias_lane)
        qaug_scr[head] = jnp.concatenate([jnp.where(own_half, q_slab, zero),
                                          jnp.where(sel, bq, zero)], axis=1)
    m_scr[...] = jnp.full_like(m_scr, -jnp.inf)
    acc_scr[...] = jnp.zeros_like(acc_scr)
    slots = ((s_even, smax_even), (s_odd, smax_odd))

    def key_block(kb, masked):
        r0 = pl.multiple_of(kb * tk, tk)
        bk_blk = bk_ref[0, pl.ds(r0, tk), :]

        def scores(hp):
            s_buf, smax_buf = slots[hp % 2]
            k_aug = jnp.concatenate([k_ref[hp, pl.ds(r0, tk), :], bk_blk], axis=1)
            for par in range(HEADS_PER_SLAB):
                s = lax.dot_general(k_aug, qaug_scr[hp * HEADS_PER_SLAB + par],
                                    (((1,), (1,)), ((), ())), preferred_element_type=F32)
                if masked:
                    s = jnp.where(kb * tk + k_off <= q_pos, s, -jnp.inf)
                s_buf[par] = s
                smax_buf[par] = jnp.max(s, axis=0, keepdims=True)

        def absorb(hp):
            s_buf, smax_buf = slots[hp % 2]
            vt = vt_ref[hp, kb]
            for par in range(HEADS_PER_SLAB):
                head = hp * HEADS_PER_SLAB + par
                m = m_scr[head]
                m_new = jnp.maximum(m, smax_buf[par])
                p = jnp.exp(s_buf[par] - m_new).astype(BF16)
                vt_aug = jnp.concatenate([vt[par * HEAD_DIM:(par + 1) * HEAD_DIM, :], ones_rows], axis=0)
                acc_scr[head] = (jnp.exp(m - m_new) * acc_scr[head]
                                 + jnp.dot(vt_aug, p, preferred_element_type=F32))
                m_scr[head] = m_new

        scores(0)
        for hp in range(N_SLABS):
            if hp + 1 < N_SLABS:
                scores(hp + 1)
            absorb(hp)

    def full_block(kb, carry):
        key_block(kb, False)
        return carry

    lax.fori_loop(0, n_full, full_block, 0)
    key_block(n_full, True)

    for hp in range(N_SLABS):
        halves = []
        for par in range(HEADS_PER_SLAB):
            acc = acc_scr[hp * HEADS_PER_SLAB + par]
            halves.append(acc[:HEAD_DIM] / acc[HEAD_DIM:HEAD_DIM + 1])
        o_ref[hp] = jnp.concatenate(halves, axis=0).T.astype(BF16)


def _fox_attn_call(q, bq, k, bk, vt, *, n_seq, n_q, n_keys, tq, tk, q_pos0):
    assert tq <= tk and tk % tq == 0 and q_pos0 % tq == 0 and n_keys % tk == 0
    kern = functools.partial(_fox_attn_kernel, tq=tq, tk=tk, q_pos0=q_pos0)
    n_qt = n_q // tq
    q_blk0 = q_pos0 // tq
    n_kb = n_keys // tk
    return pl.pallas_call(
        kern,
        out_shape=jax.ShapeDtypeStruct((N_SLABS, n_seq * n_q, LANES), BF16),
        grid=(n_seq, n_qt),
        in_specs=[pl.BlockSpec((N_SLABS, tq, LANES), lambda b, i: (0, b * n_qt + i, 0)),
                  pl.BlockSpec((1, tq, LANES), lambda b, i: (b, q_blk0 + i, 0)),
                  pl.BlockSpec((N_SLABS, n_keys, LANES), lambda b, i: (0, b, 0)),
                  pl.BlockSpec((1, n_keys, LANES), lambda b, i: (b, 0, 0)),
                  pl.BlockSpec((N_SLABS, n_kb, LANES, tk), lambda b, i: (0, b, 0, 0))],
        out_specs=pl.BlockSpec((N_SLABS, tq, LANES), lambda b, i: (0, b * n_qt + i, 0)),
        scratch_shapes=[pltpu.VMEM((N_HEADS, tq, 2 * LANES), BF16),
                        pltpu.VMEM((N_HEADS, 1, tq), F32),
                        pltpu.VMEM((N_HEADS, HEAD_DIM + FOX_ONES_ROWS, tq), F32),
                        pltpu.VMEM((HEADS_PER_SLAB, tk, tq), F32),
                        pltpu.VMEM((HEADS_PER_SLAB, tk, tq), F32),
                        pltpu.VMEM((HEADS_PER_SLAB, 1, tq), F32),
                        pltpu.VMEM((HEADS_PER_SLAB, 1, tq), F32)],
        compiler_params=_params(("parallel", "arbitrary")),
        name="fox_attn",
    )(q, bq, k, bk, vt)


def _rg_layer(xs, conv_state, h_state, w_in, gate_w, small, layer):
    x_p, x_s = xs
    o_p, nb_p, h_p = _rg_call(
        x_p, jnp.zeros((BATCH, CONV_WIDTH - 1, LRU_WIDTH), F32), jnp.zeros((BATCH, 1, LRU_WIDTH), F32),
        w_in, gate_w, small, layer, n_seq=BATCH, seq_len=SEQ, tile=RG_TILE)
    o_s, nb_s, h_s = _rg_call(
        x_s, conv_state, h_state.reshape(DEC_BATCH, 1, LRU_WIDTH),
        w_in, gate_w, small, layer, n_seq=DEC_BATCH, seq_len=DEC_SEQ, tile=DEC_SEQ)
    return (o_p, o_s), (nb_p, nb_s, h_p, h_s)


def _swa_layer(xs, k_cache, v_cache, w_qkv, sinks, table):
    kvd = SWA_KV_HEADS * HEAD_DIM
    order = jnp.array(_SWA_HEAD_ORDER)
    w_q = (w_qkv[:, :D_MODEL] * ATTN_SCALE).reshape(D_MODEL, N_HEADS, HEAD_DIM)[:, order]
    w = jnp.concatenate([w_q.reshape(D_MODEL, D_MODEL), w_qkv[:, D_MODEL:]], axis=1).astype(BF16)
    bias, sink = _swa_row_tables(table, sinks)

    q_p, k_p, v_p, kb_p, vb_p = _swa_proj_call(xs[0], w)
    pad = ((0, 0), (WINDOW, 0), (0, 0))
    o_p = _swa_attn_call(q_p.reshape(BATCH, SEQ, D_MODEL),
                         jnp.pad(kb_p.reshape(BATCH, SEQ, kvd), pad),
                         jnp.pad(vb_p.reshape(BATCH, SEQ, kvd), pad),
                         bias, sink, n_seq=BATCH, seq_len=SEQ, tile=SWA_TILE, n_invalid=WINDOW)

    q_s, k_s, v_s, kb_s, vb_s = _swa_proj_call(xs[1], w)
    kc = k_cache.reshape(DEC_BATCH, WINDOW, kvd)
    vc = v_cache.reshape(DEC_BATCH, WINDOW, kvd)
    o_s = _swa_attn_call(q_s.reshape(DEC_BATCH, DEC_SEQ, D_MODEL),
                         jnp.concatenate([kc.astype(BF16), kb_s.reshape(DEC_BATCH, DEC_SEQ, kvd)], axis=1),
                         jnp.concatenate([vc.astype(BF16), vb_s.reshape(DEC_BATCH, DEC_SEQ, kvd)], axis=1),
                         bias, sink, n_seq=DEC_BATCH, seq_len=DEC_SEQ, tile=DEC_SEQ, n_invalid=0)

    def tails(new_p, new_s, cache):
        tail_p = new_p.reshape(BATCH, SEQ, kvd)[:, SEQ - WINDOW:]
        tail_s = jnp.concatenate([cache.reshape(DEC_BATCH, WINDOW, kvd)[:, DEC_SEQ:],
                                  new_s.reshape(DEC_BATCH, DEC_SEQ, kvd)], axis=1)
        return (tail_p.reshape(BATCH, WINDOW, SWA_KV_HEADS, HEAD_DIM),
                tail_s.reshape(DEC_BATCH, WINDOW, SWA_KV_HEADS, HEAD_DIM))

    k_tp, k_ts = tails(k_p, k_s, k_cache)
    v_tp, v_ts = tails(v_p, v_s, v_cache)
    return ((o_p.reshape(N_PROMPT, D_MODEL), o_s.reshape(N_SAMPLE, D_MODEL)),
            (k_tp, k_ts, v_tp, v_ts))


def _fox_layer(xs, k_cache, v_cache, logf_cache, w_in, b_f):
    hd = N_HEADS * HEAD_DIM
    w = jnp.concatenate([w_in[:, :hd] * ATTN_SCALE, w_in[:, hd:3 * hd],
                         jnp.pad(w_in[:, 3 * hd:], ((0, 0), (0, LANES - N_HEADS)))], axis=1).astype(BF16)
    bf = jnp.pad(b_f.astype(F32), (0, LANES - N_HEADS)).reshape(1, LANES)
    w_vt = w_in[:, 2 * hd:3 * hd].T.astype(BF16)

    q_p, k_p, v_p, kb_p, vt_p, lf_p = _fox_proj_call(xs[0], w, w_vt, bf)
    lf_p = lf_p.reshape(BATCH, SEQ, LANES)
    bq_p, bk_p = _fox_bias_call(lf_p)
    o_p = _fox_attn_call(q_p, bq_p, kb_p, bk_p, vt_p, n_seq=BATCH, n_q=SEQ, n_keys=SEQ,
                         tq=FOX_TQ, tk=FOX_TK, q_pos0=0)

    q_s, k_s, v_s, kb_s, vt_s, lf_s = _fox_proj_call(xs[1], w, w_vt, bf)
    lf_s = lf_s.reshape(DEC_BATCH, DEC_SEQ, LANES)
    tail = FOX_SAMPLE_KEYS - PAST_LEN - DEC_SEQ
    n_kb = FOX_SAMPLE_KEYS // FOX_TK
    lf_cache = jnp.pad(logf_cache.astype(F32), ((0, 0), (0, 0), (0, LANES - N_HEADS)))
    bq_s, bk_s = _fox_bias_call(jnp.pad(jnp.concatenate([lf_cache, lf_s], axis=1),
                                        ((0, 0), (0, tail), (0, 0))))
    k_old = k_cache.astype(BF16).reshape(DEC_BATCH, PAST_LEN, N_SLABS, LANES).transpose(2, 0, 1, 3)
    k_new = kb_s.reshape(N_SLABS, DEC_BATCH, DEC_SEQ, LANES)
    k_all = jnp.pad(jnp.concatenate([k_old, k_new], axis=2), ((0, 0), (0, 0), (0, tail), (0, 0)))
    k_all = k_all.reshape(N_SLABS, DEC_BATCH * FOX_SAMPLE_KEYS, LANES)
    v_old = v_cache.astype(BF16).reshape(DEC_BATCH, PAST_LEN, N_SLABS, LANES).transpose(2, 0, 3, 1)
    per_blk = FOX_TK // DEC_SEQ
    v_new = vt_s.reshape(N_SLABS, N_SAMPLE // FOX_TK, LANES, per_blk, DEC_SEQ)
    v_new = v_new.transpose(0, 1, 3, 2, 4).reshape(N_SLABS, DEC_BATCH, LANES, DEC_SEQ)
    vt_all = jnp.pad(jnp.concatenate([v_old, v_new], axis=3), ((0, 0), (0, 0), (0, 0), (0, tail)))
    vt_all = vt_all.reshape(N_SLABS, DEC_BATCH, LANES, n_kb, FOX_TK).transpose(0, 1, 3, 2, 4)
    vt_all = vt_all.reshape(N_SLABS, DEC_BATCH * n_kb, LANES, FOX_TK)
    o_s = _fox_attn_call(q_s, bq_s, k_all, bk_s, vt_all, n_seq=DEC_BATCH, n_q=DEC_SEQ,
                         n_keys=FOX_SAMPLE_KEYS, tq=DEC_SEQ, tk=FOX_TK, q_pos0=PAST_LEN)

    outs = (k_p.reshape(BATCH, SEQ, N_HEADS, HEAD_DIM), k_s.reshape(DEC_BATCH, DEC_SEQ, N_HEADS, HEAD_DIM),
            v_p.reshape(BATCH, SEQ, N_HEADS, HEAD_DIM), v_s.reshape(DEC_BATCH, DEC_SEQ, N_HEADS, HEAD_DIM),
            lf_p[..., :N_HEADS], lf_s[..., :N_HEADS])
    return (o_p, o_s), outs


def kernel(x_prompt, x_sample, state_rg_conv, state_rg_h, cache_swa_k, cache_swa_v, cache_fox_k, cache_fox_v, cache_fox_logf, ln_gain, ln_bias, ffn_w_up, ffn_w_down, rg_w_in, rg_conv_w, rg_conv_b, rg_gate_w, rg_gate_b, rg_lambda, rg_w_out, swa_w_qkv, swa_sinks, swa_w_out, rel_bias_table, fox_w_in, fox_b_f, fox_w_out):
    xs = (x_prompt.reshape(N_PROMPT, D_MODEL), x_sample.reshape(N_SAMPLE, D_MODEL))
    w_up = ffn_w_up.astype(BF16)
    w_down = ffn_w_down.astype(BF16)
    rg_in = rg_w_in.astype(BF16)
    rg_gate = rg_gate_w.astype(BF16)
    swa_order = jnp.array(_SWA_HEAD_ORDER)
    n_swa = swa_w_out.shape[0]
    w_out = {0: rg_w_out.astype(BF16),
             1: swa_w_out.reshape(n_swa, N_HEADS, HEAD_DIM, D_MODEL)[:, swa_order]
                         .reshape(n_swa, D_MODEL, D_MODEL).astype(BF16),
             2: fox_w_out.astype(BF16)}
    rg_out, swa_out, fox_out = [], [], []
    for i in range(DEPTH):
        kind, j = i % 3, i // 3
        if kind == 0:
            small = (rg_conv_w[j], rg_conv_b[j].reshape(1, LRU_WIDTH),
                     rg_gate_b[j].reshape(LRU_BLOCKS, 1, 2 * LRU_BLOCK_W), rg_lambda[j].reshape(1, LRU_WIDTH))
            os_, extra = _rg_layer(xs, state_rg_conv[j], state_rg_h[j], rg_in, rg_gate, small, j)
            rg_out.append(extra)
        elif kind == 1:
            os_, extra = _swa_layer(xs, cache_swa_k[j], cache_swa_v[j], swa_w_qkv[j], swa_sinks[j],
                                    rel_bias_table)
            swa_out.append(extra)
        else:
            os_, extra = _fox_layer(xs, cache_fox_k[j], cache_fox_v[j], cache_fox_logf[j], fox_w_in[j],
                                    fox_b_f[j])
            fox_out.append(extra)
        xs = tuple(_post_call(x, o, w_out[kind], j, w_up, w_down, ln_gain, ln_bias, i)
                   for x, o in zip(xs, os_))

    def stack(items, idx):
        return jnp.stack([it[idx] for it in items])

    return (xs[0].reshape(BATCH, SEQ, D_MODEL), xs[1].reshape(DEC_BATCH, DEC_SEQ, D_MODEL),
            stack(rg_out, 0), stack(rg_out, 1), stack(rg_out, 2), stack(rg_out, 3),
            stack(swa_out, 0), stack(swa_out, 1), stack(swa_out, 2), stack(swa_out, 3),
            stack(fox_out, 0), stack(fox_out, 1), stack(fox_out, 2), stack(fox_out, 3),
            stack(fox_out, 4), stack(fox_out, 5))
```

```python
import functools
import math

import jax
import jax.numpy as jnp
from jax import lax
from jax.experimental import pallas as pl
from jax.experimental.pallas import tpu as pltpu

F32 = jnp.float32
BF16 = jnp.bfloat16

D_MODEL = 1024
BATCH = 4
SEQ = 4096
DEPTH = 4
DEC_BATCH = 16
DEC_SEQ = 64
PAST_LEN = 1024
CHUNK = 64
D_FF = 4 * D_MODEL
HEAD_DIM = 64
N_HEADS = 16
SWA_KV_HEADS = 4
SWA_GROUP = 4
WINDOW = 128
LRU_WIDTH = D_MODEL
LRU_BLOCKS = 4
LRU_BLOCK_W = LRU_WIDTH // LRU_BLOCKS
CONV_WIDTH = 4
LRU_C = 8.0
N_BUCKETS = 32
MAX_DISTANCE = 128
ALPHA = (2.0 * DEPTH) ** 0.25
LN_EPS = 1e-5
ATTN_SCALE = HEAD_DIM ** -0.5

N_PROMPT = BATCH * SEQ
N_SAMPLE = DEC_BATCH * DEC_SEQ

LANES = 128
HEADS_PER_SLAB = LANES // HEAD_DIM
N_SLABS = D_MODEL // LANES

TOKEN_TILE = 512
POST_TILE = 512
FF_CHUNK = 1024
RG_TILE = 512
SWA_TILE = 512
SWA_KEYS = WINDOW + CHUNK
FOX_TQ = 512
FOX_TK = 512
FOX_SAMPLE_KEYS = 1536
FOX_ONES_ROWS = 16
BIAS_TILE = 512
VMEM_LIMIT = 56 * 1024 * 1024


def _resident(shape, layer=None):
    zeros = (0,) * len(shape)
    if layer is None:
        return pl.BlockSpec(shape, lambda *_: zeros, pipeline_mode=pl.Buffered(1))
    return pl.BlockSpec((None,) + tuple(shape), lambda *_: (layer,) + zeros,
                        pipeline_mode=pl.Buffered(1))


def _params(semantics):
    return pltpu.CompilerParams(dimension_semantics=semantics, vmem_limit_bytes=VMEM_LIMIT)


def _softplus(x):
    return jnp.maximum(x, 0.0) + jnp.log1p(jnp.exp(-jnp.abs(x)))


def _layer_norm(z, g, b):
    mu = jnp.mean(z, axis=-1, keepdims=True)
    zc = z - mu
    var = jnp.mean(zc * zc, axis=-1, keepdims=True)
    return zc * lax.rsqrt(var + LN_EPS) * g + b


def _post_kernel(x_ref, o_ref, wout_ref, wup_ref, wdn_ref, g_ref, b_ref, out_ref, *, slab_major):
    x = x_ref[...]
    if slab_major:
        o = jnp.concatenate([o_ref[s] for s in range(N_SLABS)], axis=1)
    else:
        o = o_ref[...]
    y = jnp.dot(o, wout_ref[...], preferred_element_type=F32)
    x1 = _layer_norm(ALPHA * x + y, g_ref[0:1, :], b_ref[0:1, :])
    x1b = x1.astype(BF16)
    acc = jnp.zeros_like(x1)
    for c in range(D_FF // FF_CHUNK):
        cols = slice(c * FF_CHUNK, (c + 1) * FF_CHUNK)
        h = jnp.dot(x1b, wup_ref[:, cols], preferred_element_type=F32)
        a = jnp.square(jnp.maximum(h, 0.0)).astype(BF16)
        acc = acc + jnp.dot(a, wdn_ref[cols, :], preferred_element_type=F32)
    out_ref[...] = _layer_norm(ALPHA * x1 + acc, g_ref[1:2, :], b_ref[1:2, :])


def _post_call(x, o, w_out, mixer_idx, w_up, w_down, gain, bias, layer):
    n = x.shape[0]
    tile = pl.BlockSpec((POST_TILE, D_MODEL), lambda i: (i, 0))
    slab_major = o.ndim == 3
    o_tile = pl.BlockSpec((N_SLABS, POST_TILE, LANES), lambda i: (0, i, 0)) if slab_major else tile
    return pl.pallas_call(
        functools.partial(_post_kernel, slab_major=slab_major),
        out_shape=jax.ShapeDtypeStruct((n, D_MODEL), F32),
        grid=(n // POST_TILE,),
        in_specs=[tile, o_tile, _resident((D_MODEL, D_MODEL), mixer_idx),
                  _resident((D_MODEL, D_FF), layer), _resident((D_FF, D_MODEL), layer),
                  _resident((2, D_MODEL), layer), _resident((2, D_MODEL), layer)],
        out_specs=tile,
        compiler_params=_params(("parallel",)),
        name="post",
    )(x, o, w_out, w_up, w_down, gain, bias)


def _rg_kernel(x_ref, cbuf_ref, h0_ref, win_ref, cw_ref, cb_ref, gw_ref, gb_ref, lam_ref,
               o_ref, nbuf_ref, hlast_ref, u_scr, h_scr, hc_scr, *, tile):
    t = pl.program_id(1)
    hist = CONV_WIDTH - 1
    pad = 8

    @pl.when(t == 0)
    def _():
        u_scr[pad - hist:pad, :] = cbuf_ref[0]
        hc_scr[...] = h0_ref[0]

    xb = x_ref[...].astype(BF16)
    gu = jnp.dot(xb, win_ref[...], preferred_element_type=F32)
    gate = gu[:, :LRU_WIDTH]
    u = gu[:, LRU_WIDTH:]
    u_scr[pad:pad + tile, :] = u
    conv = cb_ref[...] + u * cw_ref[hist:hist + 1, :]
    for k in range(hist):
        conv = conv + u_scr[pad - hist + k:pad - hist + k + tile, :] * cw_ref[k:k + 1, :]
    nbuf_ref[0] = u_scr[pad + tile - hist:pad + tile, :]
    u_scr[pad - hist:pad, :] = u_scr[pad + tile - hist:pad + tile, :]

    row = lax.broadcasted_iota(jnp.int32, (8, LRU_BLOCK_W), 0)
    for n in range(LRU_BLOCKS):
        cols = slice(n * LRU_BLOCK_W, (n + 1) * LRU_BLOCK_W)
        cn = conv[:, cols]
        g = jnp.dot(cn.astype(BF16), gw_ref[n], preferred_element_type=F32) + gb_ref[n]
        r = jax.nn.sigmoid(g[:, :LRU_BLOCK_W])
        ig = jax.nn.sigmoid(g[:, LRU_BLOCK_W:])
        log_a = -LRU_C * r * _softplus(-lam_ref[:, cols])
        a_blk = jnp.exp(log_a)
        b_blk = jnp.sqrt(1.0 - a_blk * a_blk) * (ig * cn)
        h_prev = hc_scr[:, cols]
        for j in range(tile // 8):
            a = a_blk[8 * j:8 * j + 8, :]
            b = b_blk[8 * j:8 * j + 8, :]
            for d in (1, 2, 4):
                a_sh = jnp.where(row >= d, pltpu.roll(a, d, 0), 1.0)
                b_sh = jnp.where(row >= d, pltpu.roll(b, d, 0), 0.0)
                b = a * b_sh + b
                a = a * a_sh
            h = a * h_prev + b
            h_scr[8 * j:8 * j + 8, cols] = h
            h_prev = h[7:8, :]
        hc_scr[:, cols] = h_prev

    hlast_ref[0] = hc_scr[...]
    o_ref[...] = (h_scr[...] * jax.nn.gelu(gate)).astype(BF16)


def _rg_call(x, conv_buf, h0, w_in, gate_w, small, layer, *, n_seq, seq_len, tile):
    conv_w, conv_b, gate_b, lam = small
    n_t = seq_len // tile
    kern = functools.partial(_rg_kernel, tile=tile)
    o, nbuf, hlast = pl.pallas_call(
        kern,
        out_shape=(jax.ShapeDtypeStruct((n_seq * seq_len, D_MODEL), BF16),
                   jax.ShapeDtypeStruct((n_seq, CONV_WIDTH - 1, LRU_WIDTH), F32),
                   jax.ShapeDtypeStruct((n_seq, 1, LRU_WIDTH), F32)),
        grid=(n_seq, n_t),
        in_specs=[pl.BlockSpec((tile, D_MODEL), lambda b, t: (b * n_t + t, 0)),
                  pl.BlockSpec((1, CONV_WIDTH - 1, LRU_WIDTH), lambda b, t: (b, 0, 0)),
                  pl.BlockSpec((1, 1, LRU_WIDTH), lambda b, t: (b, 0, 0)),
                  _resident((D_MODEL, 2 * LRU_WIDTH), layer),
                  _resident((CONV_WIDTH, LRU_WIDTH)),
                  _resident((1, LRU_WIDTH)),
                  _resident((LRU_BLOCKS, LRU_BLOCK_W, 2 * LRU_BLOCK_W), layer),
                  _resident((LRU_BLOCKS, 1, 2 * LRU_BLOCK_W)),
                  _resident((1, LRU_WIDTH))],
        out_specs=(pl.BlockSpec((tile, D_MODEL), lambda b, t: (b * n_t + t, 0)),
                   pl.BlockSpec((1, CONV_WIDTH - 1, LRU_WIDTH), lambda b, t: (b, 0, 0)),
                   pl.BlockSpec((1, 1, LRU_WIDTH), lambda b, t: (b, 0, 0))),
        scratch_shapes=[pltpu.VMEM((tile + 8, LRU_WIDTH), F32),
                        pltpu.VMEM((tile, LRU_WIDTH), F32),
                        pltpu.VMEM((1, LRU_WIDTH), F32)],
        compiler_params=_params(("parallel", "arbitrary")),
        name="rg_mixer",
    )(x, conv_buf, h0, w_in, conv_w, conv_b, gate_w, gate_b, lam)
    return o, nbuf, hlast[:, 0, :]


def _swa_proj_kernel(x_ref, w_ref, q_ref, k_ref, v_ref, kb_ref, vb_ref):
    xb = x_ref[...].astype(BF16)
    qkv = jnp.dot(xb, w_ref[...], preferred_element_type=F32)
    kvd = SWA_KV_HEADS * HEAD_DIM
    q_ref[...] = qkv[:, :D_MODEL].astype(BF16)
    k = qkv[:, D_MODEL:D_MODEL + kvd]
    v = qkv[:, D_MODEL + kvd:]
    k_ref[...] = k
    v_ref[...] = v
    kb_ref[...] = k.astype(BF16)
    vb_ref[...] = v.astype(BF16)


def _swa_proj_call(x, w):
    n = x.shape[0]
    kvd = SWA_KV_HEADS * HEAD_DIM
    row = lambda i: (i, 0)
    return pl.pallas_call(
        _swa_proj_kernel,
        out_shape=(jax.ShapeDtypeStruct((n, D_MODEL), BF16),
                   jax.ShapeDtypeStruct((n, kvd), F32),
                   jax.ShapeDtypeStruct((n, kvd), F32),
                   jax.ShapeDtypeStruct((n, kvd), BF16),
                   jax.ShapeDtypeStruct((n, kvd), BF16)),
        grid=(n // TOKEN_TILE,),
        in_specs=[pl.BlockSpec((TOKEN_TILE, D_MODEL), row), _resident((D_MODEL, D_MODEL + 2 * kvd))],
        out_specs=(pl.BlockSpec((TOKEN_TILE, D_MODEL), row),
                   pl.BlockSpec((TOKEN_TILE, kvd), row), pl.BlockSpec((TOKEN_TILE, kvd), row),
                   pl.BlockSpec((TOKEN_TILE, kvd), row), pl.BlockSpec((TOKEN_TILE, kvd), row)),
        compiler_params=_params(("parallel",)),
        name="swa_proj",
    )(x, w)


def _swa_attn_kernel(q_ref, k_ref, v_ref, bias_ref, sink_ref, o_ref, *, tile, n_invalid):
    ti = pl.program_id(1)
    cols = SWA_GROUP * HEADS_PER_SLAB * CHUNK
    half_cols = cols // HEADS_PER_SLAB
    lane_q = lax.broadcasted_iota(jnp.int32, (CHUNK, LANES), 1)
    low_q = lane_q < HEAD_DIM
    key_idx = lax.broadcasted_iota(jnp.int32, (SWA_KEYS, cols), 0)
    for c in range(tile // CHUNK):
        row0 = pl.multiple_of(ti * tile + c * CHUNK, CHUNK)
        masked = n_invalid > 0 and c * CHUNK < n_invalid
        for t in range(SWA_KV_HEADS // HEADS_PER_SLAB):
            kv_cols = slice(t * LANES, (t + 1) * LANES)
            kw = k_ref[0, pl.ds(row0, SWA_KEYS), kv_cols]
            vw = v_ref[0, pl.ds(row0, SWA_KEYS), kv_cols]
            slabs = [q_ref[0, c * CHUNK:(c + 1) * CHUNK,
                           (SWA_GROUP * t + g) * LANES:(SWA_GROUP * t + g + 1) * LANES]
                     for g in range(SWA_GROUP)]
            zero = jnp.zeros_like(slabs[0])
            q_stack = jnp.concatenate([jnp.where(low_q, s_, zero) for s_ in slabs]
                                      + [jnp.where(low_q, zero, s_) for s_ in slabs], axis=0)
            s = lax.dot_general(kw, q_stack, (((1,), (1,)), ((), ())), preferred_element_type=F32)
            s = s + bias_ref[t]
            if masked:
                s = jnp.where(key_idx + row0 >= n_invalid, s, -jnp.inf)
            sink = sink_ref[t]
            m = jnp.maximum(jnp.max(s, axis=0, keepdims=True), sink)
            p = jnp.exp(s - m)
            denom = jnp.sum(p, axis=0, keepdims=True) + jnp.exp(sink - m)
            out_t = lax.dot_general(vw, p.astype(BF16), (((0,), (0,)), ((), ())),
                                    preferred_element_type=F32) / denom
            pair = jnp.concatenate([out_t[:HEAD_DIM, :half_cols], out_t[HEAD_DIM:, half_cols:]], axis=0)
            pair = pair.T.astype(BF16)
            for g in range(SWA_GROUP):
                s_idx = SWA_GROUP * t + g
                o_ref[0, c * CHUNK:(c + 1) * CHUNK, s_idx * LANES:(s_idx + 1) * LANES] = (
                    pair[g * CHUNK:(g + 1) * CHUNK, :])


def _swa_attn_call(q, k_win, v_win, bias, sink, *, n_seq, seq_len, tile, n_invalid):
    kvd = SWA_KV_HEADS * HEAD_DIM
    n_keys = k_win.shape[1]
    cols = SWA_GROUP * HEADS_PER_SLAB * CHUNK
    n_kv_slabs = SWA_KV_HEADS // HEADS_PER_SLAB
    kern = functools.partial(_swa_attn_kernel, tile=tile, n_invalid=n_invalid)
    return pl.pallas_call(
        kern,
        out_shape=jax.ShapeDtypeStruct((n_seq, seq_len, D_MODEL), BF16),
        grid=(n_seq, seq_len // tile),
        in_specs=[pl.BlockSpec((1, tile, D_MODEL), lambda b, t: (b, t, 0)),
                  pl.BlockSpec((1, n_keys, kvd), lambda b, t: (b, 0, 0)),
                  pl.BlockSpec((1, n_keys, kvd), lambda b, t: (b, 0, 0)),
                  _resident((n_kv_slabs, SWA_KEYS, cols)),
                  _resident((n_kv_slabs, 1, cols))],
        out_specs=pl.BlockSpec((1, tile, D_MODEL), lambda b, t: (b, t, 0)),
        compiler_params=_params(("parallel", "arbitrary")),
        name="swa_attn",
    )(q, k_win, v_win, bias, sink)


def _t5_bucket(rel):
    half = N_BUCKETS // 2
    max_exact = half // 2
    n = jnp.abs(rel)
    n_f = jnp.maximum(n, 1).astype(jnp.float32)
    large = max_exact + (jnp.log(n_f / max_exact) / math.log(MAX_DISTANCE / max_exact)
                         * (half - max_exact)).astype(jnp.int32)
    large = jnp.minimum(large, half - 1)
    return jnp.where(rel > 0, half, 0) + jnp.where(n < max_exact, n, large)


_SWA_HEAD_ORDER = [8 * t + 4 * p + g for t in range(2) for g in range(4) for p in range(2)]


def _swa_row_tables(table, sinks):
    rel = jnp.arange(SWA_KEYS)[None, :] - WINDOW - jnp.arange(CHUNK)[:, None]
    bucket = _t5_bucket(rel)
    tbl = table.astype(F32)
    hit = bucket[None, None] == jnp.arange(N_BUCKETS)[:, None, None, None]
    bias = jnp.sum(jnp.where(hit, tbl[:, :, None, None], 0.0), axis=0)
    n_kv_slabs = SWA_KV_HEADS // HEADS_PER_SLAB
    cols = SWA_GROUP * HEADS_PER_SLAB * CHUNK
    bias = bias.reshape(n_kv_slabs, cols, SWA_KEYS).transpose(0, 2, 1)
    sink = jnp.repeat(sinks.astype(F32), CHUNK).reshape(n_kv_slabs, 1, cols)
    return bias, sink


def _fox_proj_kernel(x_ref, w_ref, wvt_ref, bf_ref, q_ref, k_ref, v_ref, kb_ref, vt_ref, lf_ref):
    xb = x_ref[...].astype(BF16)
    proj = jnp.dot(xb, w_ref[...], preferred_element_type=F32)
    k = proj[:, D_MODEL:2 * D_MODEL]
    k_ref[...] = k
    v_ref[...] = proj[:, 2 * D_MODEL:3 * D_MODEL]
    vt = lax.dot_general(wvt_ref[...], xb, (((1,), (1,)), ((), ())), preferred_element_type=F32)
    for s in range(N_SLABS):
        cols = slice(s * LANES, (s + 1) * LANES)
        q_ref[s] = proj[:, cols].astype(BF16)
        kb_ref[s] = k[:, cols].astype(BF16)
        for j in range(TOKEN_TILE // FOX_TK):
            vt_ref[s, j] = vt[cols, j * FOX_TK:(j + 1) * FOX_TK].astype(BF16)
    z = proj[:, 3 * D_MODEL:] + bf_ref[...]
    lf_ref[...] = -_softplus(-z)


def _fox_proj_call(x, w, w_vt, b_f):
    n = x.shape[0]
    row = lambda i: (i, 0)
    wide = pl.BlockSpec((TOKEN_TILE, D_MODEL), row)
    slabs = pl.BlockSpec((N_SLABS, TOKEN_TILE, LANES), lambda i: (0, i, 0))
    kb_per_tile = TOKEN_TILE // FOX_TK
    return pl.pallas_call(
        _fox_proj_kernel,
        out_shape=(jax.ShapeDtypeStruct((N_SLABS, n, LANES), BF16),
                   jax.ShapeDtypeStruct((n, D_MODEL), F32),
                   jax.ShapeDtypeStruct((n, D_MODEL), F32),
                   jax.ShapeDtypeStruct((N_SLABS, n, LANES), BF16),
                   jax.ShapeDtypeStruct((N_SLABS, n // FOX_TK, LANES, FOX_TK), BF16),
                   jax.ShapeDtypeStruct((n, LANES), F32)),
        grid=(n // TOKEN_TILE,),
        in_specs=[wide, _resident((D_MODEL, 3 * D_MODEL + LANES)), _resident((D_MODEL, D_MODEL)),
                  _resident((1, LANES))],
        out_specs=(slabs, wide, wide, slabs,
                   pl.BlockSpec((N_SLABS, kb_per_tile, LANES, FOX_TK), lambda i: (0, i, 0, 0)),
                   pl.BlockSpec((TOKEN_TILE, LANES), row)),
        compiler_params=_params(("parallel",)),
        name="fox_proj",
    )(x, w, w_vt, b_f)


def _split3(x):
    hi = x.astype(BF16)
    r1 = x - hi.astype(F32)
    mid = r1.astype(BF16)
    lo = (r1 - mid.astype(F32)).astype(BF16)
    return hi, mid, lo


def _fox_bias_kernel(lf_ref, place_ref, const_ref, bq_ref, bk_ref, carry_scr):
    t = pl.program_id(1)

    @pl.when(t == 0)
    def _():
        carry_scr[...] = jnp.zeros_like(carry_scr)

    r = lax.broadcasted_iota(jnp.int32, (BIAS_TILE, BIAS_TILE), 0)
    cidx = lax.broadcasted_iota(jnp.int32, (BIAS_TILE, BIAS_TILE), 1)
    tri = (cidx <= r).astype(BF16)
    tri3 = jnp.concatenate([tri, tri, tri], axis=1)
    parts = jnp.concatenate(_split3(lf_ref[0]), axis=0)
    c = carry_scr[...] + jnp.dot(tri3, parts, preferred_element_type=F32)
    carry_scr[...] = c[BIAS_TILE - 1:BIAS_TILE, :]
    c3 = jnp.concatenate(_split3(c), axis=1)
    both = const_ref[...] + jnp.dot(c3, place_ref[...], preferred_element_type=F32)
    bq_ref[0] = both[:, :LANES].astype(BF16)
    bk_ref[0] = both[:, LANES:].astype(BF16)


def _fox_bias_tables():
    place = [[0.0] * (2 * LANES) for _ in range(3 * LANES)]
    const = [0.0] * (2 * LANES)
    for i in range(3):
        for h in range(N_HEADS):
            const[N_HEADS * i + h] = 1.0
            place[LANES * i + h][N_HEADS * (3 + i) + h] = 1.0
            place[LANES * i + h][LANES + N_HEADS * i + h] = -1.0
            const[LANES + N_HEADS * (3 + i) + h] = 1.0
    return jnp.array(place, BF16), jnp.array([const], F32)


def _fox_bias_call(logf):
    n_seq, n_rows, _ = logf.shape
    place, const = _fox_bias_tables()
    blk = pl.BlockSpec((1, BIAS_TILE, LANES), lambda b, t: (b, t, 0))
    return pl.pallas_call(
        _fox_bias_kernel,
        out_shape=(jax.ShapeDtypeStruct((n_seq, n_rows, LANES), BF16),
                   jax.ShapeDtypeStruct((n_seq, n_rows, LANES), BF16)),
        grid=(n_seq, n_rows // BIAS_TILE),
        in_specs=[blk, _resident((3 * LANES, 2 * LANES)), _resident((1, 2 * LANES))],
        out_specs=(blk, blk),
        scratch_shapes=[pltpu.VMEM((1, LANES), F32)],
        compiler_params=_params(("parallel", "arbitrary")),
        name="fox_bias",
    )(logf, place, const)


def _fox_attn_kernel(q_ref, bq_ref, k_ref, bk_ref, vt_ref, o_ref,
                     qaug_scr, m_scr, acc_scr, s_even, s_odd, smax_even, smax_odd,
                     *, tq, tk, q_pos0):
    qi = pl.program_id(1)
    first_q = q_pos0 + qi * tq
    n_full = first_q // tk
    lane = lax.broadcasted_iota(jnp.int32, (tq, LANES), 1)
    low_lane = lane < HEAD_DIM
    bias_lane = lane < 6 * N_HEADS
    k_off = lax.broadcasted_iota(jnp.int32, (tk, tq), 0)
    q_pos = first_q + lax.broadcasted_iota(jnp.int32, (tk, tq), 1)
    bq = bq_ref[0]
    zero = jnp.zeros_like(bq)
    ones_rows = jnp.ones((FOX_ONES_ROWS, tk), BF16)

    for head in range(N_HEADS):
        q_slab = q_ref[head // HEADS_PER_SLAB]
        own_half = low_lane if head % HEADS_PER_SLAB == 0 else jnp.logical_not(low_lane)
        sel = jnp.logical_and((lane & (N_HEADS - 1)) == head, bias_lane)
        qaug_scr[head] = jnp.concatenate([jnp.where(own_half, q_slab, zero),
                                          jnp.where(sel, bq, zero)], axis=1)
    m_scr[...] = jnp.full_like(m_scr, -jnp.inf)
    acc_scr[...] = jnp.zeros_like(acc_scr)
    slots = ((s_even, smax_even), (s_odd, smax_odd))

    def key_block(kb, masked):
        r0 = pl.multiple_of(kb * tk, tk)
        bk_blk = bk_ref[0, pl.ds(r0, tk), :]

        def scores(hp):
            s_buf, smax_buf = slots[hp % 2]
            k_aug = jnp.concatenate([k_ref[hp, pl.ds(r0, tk), :], bk_blk], axis=1)
            for par in range(HEADS_PER_SLAB):
                s = lax.dot_general(k_aug, qaug_scr[hp * HEADS_PER_SLAB + par],
                                    (((1,), (1,)), ((), ())), preferred_element_type=F32)
                if masked:
                    s = jnp.where(kb * tk + k_off <= q_pos, s, -jnp.inf)
                s_buf[par] = s
                smax_buf[par] = jnp.max(s, axis=0, keepdims=True)

        def absorb(hp):
            s_buf, smax_buf = slots[hp % 2]
            vt = vt_ref[hp, kb]
            for par in range(HEADS_PER_SLAB):
                head = hp * HEADS_PER_SLAB + par
                m = m_scr[head]
                m_new = jnp.maximum(m, smax_buf[par])
                p = jnp.exp(s_buf[par] - m_new).astype(BF16)
                vt_aug = jnp.concatenate([vt[par * HEAD_DIM:(par + 1) * HEAD_DIM, :], ones_rows], axis=0)
                acc_scr[head] = (jnp.exp(m - m_new) * acc_scr[head]
                                 + jnp.dot(vt_aug, p, preferred_element_type=F32))
                m_scr[head] = m_new

        scores(0)
        for hp in range(N_SLABS):
            if hp + 1 < N_SLABS:
                scores(hp + 1)
            absorb(hp)

    def full_block(kb, carry):
        key_block(kb, False)
        return carry

    lax.fori_loop(0, n_full, full_block, 0)
    key_block(n_full, True)

    for hp in range(N_SLABS):
        halves = []
        for par in range(HEADS_PER_SLAB):
            acc = acc_scr[hp * HEADS_PER_SLAB + par]
            halves.append(acc[:HEAD_DIM] / acc[HEAD_DIM:HEAD_DIM + 1])
        o_ref[hp] = jnp.concatenate(halves, axis=0).T.astype(BF16)


def _fox_attn_call(q, bq, k, bk, vt, *, n_seq, n_q, n_keys, tq, tk, q_pos0):
    assert tq <= tk and tk % tq == 0 and q_pos0 % tq == 0 and n_keys % tk == 0
    kern = functools.partial(_fox_attn_kernel, tq=tq, tk=tk, q_pos0=q_pos0)
    n_qt = n_q // tq
    q_blk0 = q_pos0 // tq
    n_kb = n_keys // tk
    return pl.pallas_call(
        kern,
        out_shape=jax.ShapeDtypeStruct((N_SLABS, n_seq * n_q, LANES), BF16),
        grid=(n_seq, n_qt),
        in_specs=[pl.BlockSpec((N_SLABS, tq, LANES), lambda b, i: (0, b * n_qt + i, 0)),
                  pl.BlockSpec((1, tq, LANES), lambda b, i: (b, q_blk0 + i, 0)),
                  pl.BlockSpec((N_SLABS, n_keys, LANES), lambda b, i: (0, b, 0)),
                  pl.BlockSpec((1, n_keys, LANES), lambda b, i: (b, 0, 0)),
                  pl.BlockSpec((N_SLABS, n_kb, LANES, tk), lambda b, i: (0, b, 0, 0))],
        out_specs=pl.BlockSpec((N_SLABS, tq, LANES), lambda b, i: (0, b * n_qt + i, 0)),
        scratch_shapes=[pltpu.VMEM((N_HEADS, tq, 2 * LANES), BF16),
                        pltpu.VMEM((N_HEADS, 1, tq), F32),
                        pltpu.VMEM((N_HEADS, HEAD_DIM + FOX_ONES_ROWS, tq), F32),
                        pltpu.VMEM((HEADS_PER_SLAB, tk, tq), F32),
                        pltpu.VMEM((HEADS_PER_SLAB, tk, tq), F32),
                        pltpu.VMEM((HEADS_PER_SLAB, 1, tq), F32),
                        pltpu.VMEM((HEADS_PER_SLAB, 1, tq), F32)],
        compiler_params=_params(("parallel", "arbitrary")),
        name="fox_attn",
    )(q, bq, k, bk, vt)


def _rg_layer(xs, conv_state, h_state, w_in, gate_w, small, layer):
    x_p, x_s = xs
    o_p, nb_p, h_p = _rg_call(
        x_p, jnp.zeros((BATCH, CONV_WIDTH - 1, LRU_WIDTH), F32), jnp.zeros((BATCH, 1, LRU_WIDTH), F32),
        w_in, gate_w, small, layer, n_seq=BATCH, seq_len=SEQ, tile=RG_TILE)
    o_s, nb_s, h_s = _rg_call(
        x_s, conv_state, h_state.reshape(DEC_BATCH, 1, LRU_WIDTH),
        w_in, gate_w, small, layer, n_seq=DEC_BATCH, seq_len=DEC_SEQ, tile=DEC_SEQ)
    return (o_p, o_s), (nb_p, nb_s, h_p, h_s)


def _swa_layer(xs, k_cache, v_cache, w_qkv, sinks, table):
    kvd = SWA_KV_HEADS * HEAD_DIM
    order = jnp.array(_SWA_HEAD_ORDER)
    w_q = (w_qkv[:, :D_MODEL] * ATTN_SCALE).reshape(D_MODEL, N_HEADS, HEAD_DIM)[:, order]
    w = jnp.concatenate([w_q.reshape(D_MODEL, D_MODEL), w_qkv[:, D_MODEL:]], axis=1).astype(BF16)
    bias, sink = _swa_row_tables(table, sinks)

    q_p, k_p, v_p, kb_p, vb_p = _swa_proj_call(xs[0], w)
    pad = ((0, 0), (WINDOW, 0), (0, 0))
    o_p = _swa_attn_call(q_p.reshape(BATCH, SEQ, D_MODEL),
                         jnp.pad(kb_p.reshape(BATCH, SEQ, kvd), pad),
                         jnp.pad(vb_p.reshape(BATCH, SEQ, kvd), pad),
                         bias, sink, n_seq=BATCH, seq_len=SEQ, tile=SWA_TILE, n_invalid=WINDOW)

    q_s, k_s, v_s, kb_s, vb_s = _swa_proj_call(xs[1], w)
    kc = k_cache.reshape(DEC_BATCH, WINDOW, kvd)
    vc = v_cache.reshape(DEC_BATCH, WINDOW, kvd)
    o_s = _swa_attn_call(q_s.reshape(DEC_BATCH, DEC_SEQ, D_MODEL),
                         jnp.concatenate([kc.astype(BF16), kb_s.reshape(DEC_BATCH, DEC_SEQ, kvd)], axis=1),
                         jnp.concatenate([vc.astype(BF16), vb_s.reshape(DEC_BATCH, DEC_SEQ, kvd)], axis=1),
                         bias, sink, n_seq=DEC_BATCH, seq_len=DEC_SEQ, tile=DEC_SEQ, n_invalid=0)

    def tails(new_p, new_s, cache):
        tail_p = new_p.reshape(BATCH, SEQ, kvd)[:, SEQ - WINDOW:]
        tail_s = jnp.concatenate([cache.reshape(DEC_BATCH, WINDOW, kvd)[:, DEC_SEQ:],
                                  new_s.reshape(DEC_BATCH, DEC_SEQ, kvd)], axis=1)
        return (tail_p.reshape(BATCH, WINDOW, SWA_KV_HEADS, HEAD_DIM),
                tail_s.reshape(DEC_BATCH, WINDOW, SWA_KV_HEADS, HEAD_DIM))

    k_tp, k_ts = tails(k_p, k_s, k_cache)
    v_tp, v_ts = tails(v_p, v_s, v_cache)
    return ((o_p.reshape(N_PROMPT, D_MODEL), o_s.reshape(N_SAMPLE, D_MODEL)),
            (k_tp, k_ts, v_tp, v_ts))


def _fox_layer(xs, k_cache, v_cache, logf_cache, w_in, b_f):
    hd = N_HEADS * HEAD_DIM
    w = jnp.concatenate([w_in[:, :hd] * ATTN_SCALE, w_in[:, hd:3 * hd],
                         jnp.pad(w_in[:, 3 * hd:], ((0, 0), (0, LANES - N_HEADS)))], axis=1).astype(BF16)
    bf = jnp.pad(b_f.astype(F32), (0, LANES - N_HEADS)).reshape(1, LANES)
    w_vt = w_in[:, 2 * hd:3 * hd].T.astype(BF16)

    q_p, k_p, v_p, kb_p, vt_p, lf_p = _fox_proj_call(xs[0], w, w_vt, bf)
    lf_p = lf_p.reshape(BATCH, SEQ, LANES)
    bq_p, bk_p = _fox_bias_call(lf_p)
    o_p = _fox_attn_call(q_p, bq_p, kb_p, bk_p, vt_p, n_seq=BATCH, n_q=SEQ, n_keys=SEQ,
                         tq=FOX_TQ, tk=FOX_TK, q_pos0=0)

    q_s, k_s, v_s, kb_s, vt_s, lf_s = _fox_proj_call(xs[1], w, w_vt, bf)
    lf_s = lf_s.reshape(DEC_BATCH, DEC_SEQ, LANES)
    tail = FOX_SAMPLE_KEYS - PAST_LEN - DEC_SEQ
    n_kb = FOX_SAMPLE_KEYS // FOX_TK
    lf_cache = jnp.pad(logf_cache.astype(F32), ((0, 0), (0, 0), (0, LANES - N_HEADS)))
    bq_s, bk_s = _fox_bias_call(jnp.pad(jnp.concatenate([lf_cache, lf_s], axis=1),
                                        ((0, 0), (0, tail), (0, 0))))
    k_old = k_cache.astype(BF16).reshape(DEC_BATCH, PAST_LEN, N_SLABS, LANES).transpose(2, 0, 1, 3)
    k_new = kb_s.reshape(N_SLABS, DEC_BATCH, DEC_SEQ, LANES)
    k_all = jnp.pad(jnp.concatenate([k_old, k_new], axis=2), ((0, 0), (0, 0), (0, tail), (0, 0)))
    k_all = k_all.reshape(N_SLABS, DEC_BATCH * FOX_SAMPLE_KEYS, LANES)
    v_old = v_cache.astype(BF16).reshape(DEC_BATCH, PAST_LEN, N_SLABS, LANES).transpose(2, 0, 3, 1)
    per_blk = FOX_TK // DEC_SEQ
    v_new = vt_s.reshape(N_SLABS, N_SAMPLE // FOX_TK, LANES, per_blk, DEC_SEQ)
    v_new = v_new.transpose(0, 1, 3, 2, 4).reshape(N_SLABS, DEC_BATCH, LANES, DEC_SEQ)
    vt_all = jnp.pad(jnp.concatenate([v_old, v_new], axis=3), ((0, 0), (0, 0), (0, 0), (0, tail)))
    vt_all = vt_all.reshape(N_SLABS, DEC_BATCH, LANES, n_kb, FOX_TK).transpose(0, 1, 3, 2, 4)
    vt_all = vt_all.reshape(N_SLABS, DEC_BATCH * n_kb, LANES, FOX_TK)
    o_s = _fox_attn_call(q_s, bq_s, k_all, bk_s, vt_all, n_seq=DEC_BATCH, n_q=DEC_SEQ,
                         n_keys=FOX_SAMPLE_KEYS, tq=DEC_SEQ, tk=FOX_TK, q_pos0=PAST_LEN)

    outs = (k_p.reshape(BATCH, SEQ, N_HEADS, HEAD_DIM), k_s.reshape(DEC_BATCH, DEC_SEQ, N_HEADS, HEAD_DIM),
            v_p.reshape(BATCH, SEQ, N_HEADS, HEAD_DIM), v_s.reshape(DEC_BATCH, DEC_SEQ, N_HEADS, HEAD_DIM),
            lf_p[..., :N_HEADS], lf_s[..., :N_HEADS])
    return (o_p, o_s), outs


def kernel(x_prompt, x_sample, state_rg_conv, state_rg_h, cache_swa_k, cache_swa_v, cache_fox_k, cache_fox_v, cache_fox_logf, ln_gain, ln_bias, ffn_w_up, ffn_w_down, rg_w_in, rg_conv_w, rg_conv_b, rg_gate_w, rg_gate_b, rg_lambda, rg_w_out, swa_w_qkv, swa_sinks, swa_w_out, rel_bias_table, fox_w_in, fox_b_f, fox_w_out):
    xs = (x_prompt.reshape(N_PROMPT, D_MODEL), x_sample.reshape(N_SAMPLE, D_MODEL))
    w_up = ffn_w_up.astype(BF16)
    w_down = ffn_w_down.astype(BF16)
    rg_in = rg_w_in.astype(BF16)
    rg_gate = rg_gate_w.astype(BF16)
    swa_order = jnp.array(_SWA_HEAD_ORDER)
    n_swa = swa_w_out.shape[0]
    w_out = {0: rg_w_out.astype(BF16),
             1: swa_w_out.reshape(n_swa, N_HEADS, HEAD_DIM, D_MODEL)[:, swa_order]
                         .reshape(n_swa, D_MODEL, D_MODEL).astype(BF16),
             2: fox_w_out.astype(BF16)}
    rg_out, swa_out, fox_out = [], [], []
    for i in range(DEPTH):
        kind, j = i % 3, i // 3
        if kind == 0:
            small = (rg_conv_w[j], rg_conv_b[j].reshape(1, LRU_WIDTH),
                     rg_gate_b[j].reshape(LRU_BLOCKS, 1, 2 * LRU_BLOCK_W), rg_lambda[j].reshape(1, LRU_WIDTH))
            os_, extra = _rg_layer(xs, state_rg_conv[j], state_rg_h[j], rg_in, rg_gate, small, j)
            rg_out.append(extra)
        elif kind == 1:
            os_, extra = _swa_layer(xs, cache_swa_k[j], cache_swa_v[j], swa_w_qkv[j], swa_sinks[j],
                                    rel_bias_table)
            swa_out.append(extra)
        else:
            os_, extra = _fox_layer(xs, cache_fox_k[j], cache_fox_v[j], cache_fox_logf[j], fox_w_in[j],
                                    fox_b_f[j])
            fox_out.append(extra)
        xs = tuple(_post_call(x, o, w_out[kind], j, w_up, w_down, ln_gain, ln_bias, i)
                   for x, o in zip(xs, os_))

    def stack(items, idx):
        return jnp.stack([it[idx] for it in items])

    return (xs[0].reshape(BATCH, SEQ, D_MODEL), xs[1].reshape(DEC_BATCH, DEC_SEQ, D_MODEL),
            stack(rg_out, 0), stack(rg_out, 1), stack(rg_out, 2), stack(rg_out, 3),
            stack(swa_out, 0), stack(swa_out, 1), stack(swa_out, 2), stack(swa_out, 3),
            stack(fox_out, 0), stack(fox_out, 1), stack(fox_out, 2), stack(fox_out, 3),
            stack(fox_out, 4), stack(fox_out, 5))
```

```python
import functools
import math

import jax
import jax.numpy as jnp
from jax import lax
from jax.experimental import pallas as pl
from jax.experimental.pallas import tpu as pltpu

F32 = jnp.float32
BF16 = jnp.bfloat16

D_MODEL = 1024
BATCH = 4
SEQ = 4096
DEPTH = 4
DEC_BATCH = 16
DEC_SEQ = 64
PAST_LEN = 1024
CHUNK = 64
D_FF = 4 * D_MODEL
HEAD_DIM = 64
N_HEADS = 16
SWA_KV_HEADS = 4
SWA_GROUP = 4
WINDOW = 128
LRU_WIDTH = D_MODEL
LRU_BLOCKS = 4
LRU_BLOCK_W = LRU_WIDTH // LRU_BLOCKS
CONV_WIDTH = 4
LRU_C = 8.0
N_BUCKETS = 32
MAX_DISTANCE = 128
ALPHA = (2.0 * DEPTH) ** 0.25
LN_EPS = 1e-5
ATTN_SCALE = HEAD_DIM ** -0.5

N_PROMPT = BATCH * SEQ
N_SAMPLE = DEC_BATCH * DEC_SEQ

LANES = 128
HEADS_PER_SLAB = LANES // HEAD_DIM
N_SLABS = D_MODEL // LANES

TOKEN_TILE = 512
POST_TILE = 512
FF_CHUNK = 1024
RG_TILE = 512
SWA_TILE = 512
SWA_KEYS = WINDOW + CHUNK
FOX_TQ = 512
FOX_TK = 512
FOX_SAMPLE_KEYS = 1536
FOX_ONES_ROWS = 16
BIAS_TILE = 512
VMEM_LIMIT = 56 * 1024 * 1024


def _resident(shape, layer=None):
    zeros = (0,) * len(shape)
    if layer is None:
        return pl.BlockSpec(shape, lambda *_: zeros, pipeline_mode=pl.Buffered(1))
    return pl.BlockSpec((None,) + tuple(shape), lambda *_: (layer,) + zeros,
                        pipeline_mode=pl.Buffered(1))


def _params(semantics):
    return pltpu.CompilerParams(dimension_semantics=semantics, vmem_limit_bytes=VMEM_LIMIT)


def _softplus(x):
    return jnp.maximum(x, 0.0) + jnp.log1p(jnp.exp(-jnp.abs(x)))


def _layer_norm(z, g, b):
    mu = jnp.mean(z, axis=-1, keepdims=True)
    zc = z - mu
    var = jnp.mean(zc * zc, axis=-1, keepdims=True)
    return zc * lax.rsqrt(var + LN_EPS) * g + b


def _post_kernel(x_ref, o_ref, wout_ref, wup_ref, wdn_ref, g_ref, b_ref, out_ref, *, slab_major):
    x = x_ref[...]
    if slab_major:
        o = jnp.concatenate([o_ref[s] for s in range(N_SLABS)], axis=1)
    else:
        o = o_ref[...]
    y = jnp.dot(o, wout_ref[...], preferred_element_type=F32)
    x1 = _layer_norm(ALPHA * x + y, g_ref[0:1, :], b_ref[0:1, :])
    x1b = x1.astype(BF16)
    acc = jnp.zeros_like(x1)
    for c in range(D_FF // FF_CHUNK):
        cols = slice(c * FF_CHUNK, (c + 1) * FF_CHUNK)
        h = jnp.dot(x1b, wup_ref[:, cols], preferred_element_type=F32)
        a = jnp.square(jnp.maximum(h, 0.0)).astype(BF16)
        acc = acc + jnp.dot(a, wdn_ref[cols, :], preferred_element_type=F32)
    out_ref[...] = _layer_norm(ALPHA * x1 + acc, g_ref[1:2, :], b_ref[1:2, :])


def _post_call(x, o, w_out, mixer_idx, w_up, w_down, gain, bias, layer):
    n = x.shape[0]
    tile = pl.BlockSpec((POST_TILE, D_MODEL), lambda i: (i, 0))
    slab_major = o.ndim == 3
    o_tile = pl.BlockSpec((N_SLABS, POST_TILE, LANES), lambda i: (0, i, 0)) if slab_major else tile
    return pl.pallas_call(
        functools.partial(_post_kernel, slab_major=slab_major),
        out_shape=jax.ShapeDtypeStruct((n, D_MODEL), F32),
        grid=(n // POST_TILE,),
        in_specs=[tile, o_tile, _resident((D_MODEL, D_MODEL), mixer_idx),
                  _resident((D_MODEL, D_FF), layer), _resident((D_FF, D_MODEL), layer),
                  _resident((2, D_MODEL), layer), _resident((2, D_MODEL), layer)],
        out_specs=tile,
        compiler_params=_params(("parallel",)),
        name="post",
    )(x, o, w_out, w_up, w_down, gain, bias)


def _rg_kernel(x_ref, cbuf_ref, h0_ref, win_ref, cw_ref, cb_ref, gw_ref, gb_ref, lam_ref,
               o_ref, nbuf_ref, hlast_ref, u_scr, h_scr, hc_scr, *, tile):
    t = pl.program_id(1)
    hist = CONV_WIDTH - 1
    pad = 8

    @pl.when(t == 0)
    def _():
        u_scr[pad - hist:pad, :] = cbuf_ref[0]
        hc_scr[...] = h0_ref[0]

    xb = x_ref[...].astype(BF16)
    gu = jnp.dot(xb, win_ref[...], preferred_element_type=F32)
    gate = gu[:, :LRU_WIDTH]
    u = gu[:, LRU_WIDTH:]
    u_scr[pad:pad + tile, :] = u
    conv = cb_ref[...] + u * cw_ref[hist:hist + 1, :]
    for k in range(hist):
        conv = conv + u_scr[pad - hist + k:pad - hist + k + tile, :] * cw_ref[k:k + 1, :]
    nbuf_ref[0] = u_scr[pad + tile - hist:pad + tile, :]
    u_scr[pad - hist:pad, :] = u_scr[pad + tile - hist:pad + tile, :]

    row = lax.broadcasted_iota(jnp.int32, (8, LRU_BLOCK_W), 0)
    for n in range(LRU_BLOCKS):
        cols = slice(n * LRU_BLOCK_W, (n + 1) * LRU_BLOCK_W)
        cn = conv[:, cols]
        g = jnp.dot(cn.astype(BF16), gw_ref[n], preferred_element_type=F32) + gb_ref[n]
        r = jax.nn.sigmoid(g[:, :LRU_BLOCK_W])
        ig = jax.nn.sigmoid(g[:, LRU_BLOCK_W:])
        log_a = -LRU_C * r * _softplus(-lam_ref[:, cols])
        a_blk = jnp.exp(log_a)
        b_blk = jnp.sqrt(1.0 - a_blk * a_blk) * (ig * cn)
        h_prev = hc_scr[:, cols]
        for j in range(tile // 8):
            a = a_blk[8 * j:8 * j + 8, :]
            b = b_blk[8 * j:8 * j + 8, :]
            for d in (1, 2, 4):
                a_sh = jnp.where(row >= d, pltpu.roll(a, d, 0), 1.0)
                b_sh = jnp.where(row >= d, pltpu.roll(b, d, 0), 0.0)
                b = a * b_sh + b
                a = a * a_sh
            h = a * h_prev + b
            h_scr[8 * j:8 * j + 8, cols] = h
            h_prev = h[7:8, :]
        hc_scr[:, cols] = h_prev

    hlast_ref[0] = hc_scr[...]
    o_ref[...] = (h_scr[...] * jax.nn.gelu(gate)).astype(BF16)


def _rg_call(x, conv_buf, h0, w_in, gate_w, small, layer, *, n_seq, seq_len, tile):
    conv_w, conv_b, gate_b, lam = small
    n_t = seq_len // tile
    kern = functools.partial(_rg_kernel, tile=tile)
    o, nbuf, hlast = pl.pallas_call(
        kern,
        out_shape=(jax.ShapeDtypeStruct((n_seq * seq_len, D_MODEL), BF16),
                   jax.ShapeDtypeStruct((n_seq, CONV_WIDTH - 1, LRU_WIDTH), F32),
                   jax.ShapeDtypeStruct((n_seq, 1, LRU_WIDTH), F32)),
        grid=(n_seq, n_t),
        in_specs=[pl.BlockSpec((tile, D_MODEL), lambda b, t: (b * n_t + t, 0)),
                  pl.BlockSpec((1, CONV_WIDTH - 1, LRU_WIDTH), lambda b, t: (b, 0, 0)),
                  pl.BlockSpec((1, 1, LRU_WIDTH), lambda b, t: (b, 0, 0)),
                  _resident((D_MODEL, 2 * LRU_WIDTH), layer),
                  _resident((CONV_WIDTH, LRU_WIDTH)),
                  _resident((1, LRU_WIDTH)),
                  _resident((LRU_BLOCKS, LRU_BLOCK_W, 2 * LRU_BLOCK_W), layer),
                  _resident((LRU_BLOCKS, 1, 2 * LRU_BLOCK_W)),
                  _resident((1, LRU_WIDTH))],
        out_specs=(pl.BlockSpec((tile, D_MODEL), lambda b, t: (b * n_t + t, 0)),
                   pl.BlockSpec((1, CONV_WIDTH - 1, LRU_WIDTH), lambda b, t: (b, 0, 0)),
                   pl.BlockSpec((1, 1, LRU_WIDTH), lambda b, t: (b, 0, 0))),
        scratch_shapes=[pltpu.VMEM((tile + 8, LRU_WIDTH), F32),
                        pltpu.VMEM((tile, LRU_WIDTH), F32),
                        pltpu.VMEM((1, LRU_WIDTH), F32)],
        compiler_params=_params(("parallel", "arbitrary")),
        name="rg_mixer",
    )(x, conv_buf, h0, w_in, conv_w, conv_b, gate_w, gate_b, lam)
    return o, nbuf, hlast[:, 0, :]


def _swa_proj_kernel(x_ref, w_ref, q_ref, k_ref, v_ref, kb_ref, vb_ref):
    xb = x_ref[...].astype(BF16)
    qkv = jnp.dot(xb, w_ref[...], preferred_element_type=F32)
    kvd = SWA_KV_HEADS * HEAD_DIM
    q_ref[...] = qkv[:, :D_MODEL].astype(BF16)
    k = qkv[:, D_MODEL:D_MODEL + kvd]
    v = qkv[:, D_MODEL + kvd:]
    k_ref[...] = k
    v_ref[...] = v
    kb_ref[...] = k.astype(BF16)
    vb_ref[...] = v.astype(BF16)


def _swa_proj_call(x, w):
    n = x.shape[0]
    kvd = SWA_KV_HEADS * HEAD_DIM
    row = lambda i: (i, 0)
    return pl.pallas_call(
        _swa_proj_kernel,
        out_shape=(jax.ShapeDtypeStruct((n, D_MODEL), BF16),
                   jax.ShapeDtypeStruct((n, kvd), F32),
                   jax.ShapeDtypeStruct((n, kvd), F32),
                   jax.ShapeDtypeStruct((n, kvd), BF16),
                   jax.ShapeDtypeStruct((n, kvd), BF16)),
        grid=(n // TOKEN_TILE,),
        in_specs=[pl.BlockSpec((TOKEN_TILE, D_MODEL), row), _resident((D_MODEL, D_MODEL + 2 * kvd))],
        out_specs=(pl.BlockSpec((TOKEN_TILE, D_MODEL), row),
                   pl.BlockSpec((TOKEN_TILE, kvd), row), pl.BlockSpec((TOKEN_TILE, kvd), row),
                   pl.BlockSpec((TOKEN_TILE, kvd), row), pl.BlockSpec((TOKEN_TILE, kvd), row)),
        compiler_params=_params(("parallel",)),
        name="swa_proj",
    )(x, w)


def _swa_attn_kernel(q_ref, k_ref, v_ref, bias_ref, sink_ref, o_ref, s_even, s_odd,
                     *, tile, n_invalid):
    ti = pl.program_id(1)
    cols = SWA_GROUP * HEADS_PER_SLAB * CHUNK
    half_cols = cols // HEADS_PER_SLAB
    lane_q = lax.broadcasted_iota(jnp.int32, (CHUNK, LANES), 1)
    low_q = lane_q < HEAD_DIM
    key_idx = lax.broadcasted_iota(jnp.int32, (SWA_KEYS, cols), 0)
    n_kv_slabs = SWA_KV_HEADS // HEADS_PER_SLAB
    units = [(c, t) for c in range(tile // CHUNK) for t in range(n_kv_slabs)]
    s_bufs = (s_even, s_odd)

    def window_start(c):
        return pl.multiple_of(ti * tile + c * CHUNK, CHUNK)

    def scores(u):
        c, t = units[u]
        row0 = window_start(c)
        kw = k_ref[0, pl.ds(row0, SWA_KEYS), t * LANES:(t + 1) * LANES]
        slabs = [q_ref[0, c * CHUNK:(c + 1) * CHUNK,
                       (SWA_GROUP * t + g) * LANES:(SWA_GROUP * t + g + 1) * LANES]
                 for g in range(SWA_GROUP)]
        zero = jnp.zeros_like(slabs[0])
        q_stack = jnp.concatenate([jnp.where(low_q, s_, zero) for s_ in slabs]
                                  + [jnp.where(low_q, zero, s_) for s_ in slabs], axis=0)
        s = lax.dot_general(kw, q_stack, (((1,), (1,)), ((), ())), preferred_element_type=F32)
        s = s + bias_ref[t]
        if n_invalid > 0 and c * CHUNK < n_invalid:
            s = jnp.where(key_idx + row0 >= n_invalid, s, -jnp.inf)
        s_bufs[u % 2][...] = s

    def attend(u):
        c, t = units[u]
        s = s_bufs[u % 2][...]
        vw = v_ref[0, pl.ds(window_start(c), SWA_KEYS), t * LANES:(t + 1) * LANES]
        sink = sink_ref[t]
        m = jnp.maximum(jnp.max(s, axis=0, keepdims=True), sink)
        p = jnp.exp(s - m)
        denom = jnp.sum(p, axis=0, keepdims=True) + jnp.exp(sink - m)
        out_t = lax.dot_general(vw, p.astype(BF16), (((0,), (0,)), ((), ())),
                                preferred_element_type=F32) / denom
        pair = jnp.concatenate([out_t[:HEAD_DIM, :half_cols], out_t[HEAD_DIM:, half_cols:]], axis=0)
        pair = pair.T.astype(BF16)
        for g in range(SWA_GROUP):
            s_idx = SWA_GROUP * t + g
            o_ref[0, c * CHUNK:(c + 1) * CHUNK, s_idx * LANES:(s_idx + 1) * LANES] = (
                pair[g * CHUNK:(g + 1) * CHUNK, :])

    scores(0)
    for u in range(len(units)):
        if u + 1 < len(units):
            scores(u + 1)
        attend(u)


def _swa_attn_call(q, k_win, v_win, bias, sink, *, n_seq, seq_len, tile, n_invalid):
    kvd = SWA_KV_HEADS * HEAD_DIM
    n_keys = k_win.shape[1]
    cols = SWA_GROUP * HEADS_PER_SLAB * CHUNK
    n_kv_slabs = SWA_KV_HEADS // HEADS_PER_SLAB
    kern = functools.partial(_swa_attn_kernel, tile=tile, n_invalid=n_invalid)
    return pl.pallas_call(
        kern,
        out_shape=jax.ShapeDtypeStruct((n_seq, seq_len, D_MODEL), BF16),
        grid=(n_seq, seq_len // tile),
        in_specs=[pl.BlockSpec((1, tile, D_MODEL), lambda b, t: (b, t, 0)),
                  pl.BlockSpec((1, n_keys, kvd), lambda b, t: (b, 0, 0)),
                  pl.BlockSpec((1, n_keys, kvd), lambda b, t: (b, 0, 0)),
                  _resident((n_kv_slabs, SWA_KEYS, cols)),
                  _resident((n_kv_slabs, 1, cols))],
        out_specs=pl.BlockSpec((1, tile, D_MODEL), lambda b, t: (b, t, 0)),
        scratch_shapes=[pltpu.VMEM((SWA_KEYS, cols), F32), pltpu.VMEM((SWA_KEYS, cols), F32)],
        compiler_params=_params(("parallel", "arbitrary")),
        name="swa_attn",
    )(q, k_win, v_win, bias, sink)


def _t5_bucket(rel):
    half = N_BUCKETS // 2
    max_exact = half // 2
    n = jnp.abs(rel)
    n_f = jnp.maximum(n, 1).astype(jnp.float32)
    large = max_exact + (jnp.log(n_f / max_exact) / math.log(MAX_DISTANCE / max_exact)
                         * (half - max_exact)).astype(jnp.int32)
    large = jnp.minimum(large, half - 1)
    return jnp.where(rel > 0, half, 0) + jnp.where(n < max_exact, n, large)


_SWA_HEAD_ORDER = [8 * t + 4 * p + g for t in range(2) for g in range(4) for p in range(2)]


def _swa_row_tables(table, sinks):
    rel = jnp.arange(SWA_KEYS)[None, :] - WINDOW - jnp.arange(CHUNK)[:, None]
    bucket = _t5_bucket(rel)
    tbl = table.astype(F32)
    hit = bucket[None, None] == jnp.arange(N_BUCKETS)[:, None, None, None]
    bias = jnp.sum(jnp.where(hit, tbl[:, :, None, None], 0.0), axis=0)
    n_kv_slabs = SWA_KV_HEADS // HEADS_PER_SLAB
    cols = SWA_GROUP * HEADS_PER_SLAB * CHUNK
    bias = bias.reshape(n_kv_slabs, cols, SWA_KEYS).transpose(0, 2, 1)
    sink = jnp.repeat(sinks.astype(F32), CHUNK).reshape(n_kv_slabs, 1, cols)
    return bias, sink


def _fox_proj_kernel(x_ref, w_ref, wvt_ref, bf_ref, q_ref, k_ref, v_ref, kb_ref, vt_ref, lf_ref,
                     *, time_minor):
    xb = x_ref[...].astype(BF16)
    proj = jnp.dot(xb, w_ref[...], preferred_element_type=F32)
    k = proj[:, D_MODEL:2 * D_MODEL]
    vt = lax.dot_general(wvt_ref[...], xb, (((1,), (1,)), ((), ())), preferred_element_type=F32)
    if time_minor:
        k_ref[0] = k.T
        v_ref[0] = vt
        z = proj[:, 2 * D_MODEL:]
    else:
        k_ref[...] = k.reshape(TOKEN_TILE, N_HEADS, HEAD_DIM)
        v_ref[...] = proj[:, 2 * D_MODEL:3 * D_MODEL].reshape(TOKEN_TILE, N_HEADS, HEAD_DIM)
        z = proj[:, 3 * D_MODEL:]
    for s in range(N_SLABS):
        cols = slice(s * LANES, (s + 1) * LANES)
        q_ref[s] = proj[:, cols].astype(BF16)
        kb_ref[s] = k[:, cols].astype(BF16)
        for j in range(TOKEN_TILE // FOX_TK):
            vt_ref[s, j] = vt[cols, j * FOX_TK:(j + 1) * FOX_TK].astype(BF16)
    lf_ref[...] = -_softplus(-(z + bf_ref[...]))


def _fox_proj_call(x, w, w_vt, b_f, *, seq_len, time_minor):
    n = x.shape[0]
    row = lambda i: (i, 0)
    wide = pl.BlockSpec((TOKEN_TILE, D_MODEL), row)
    slabs = pl.BlockSpec((N_SLABS, TOKEN_TILE, LANES), lambda i: (0, i, 0))
    kb_per_tile = TOKEN_TILE // FOX_TK
    if time_minor:
        t_per_seq = seq_len // TOKEN_TILE
        kv_shape = jax.ShapeDtypeStruct((n // seq_len, D_MODEL, seq_len), F32)
        kv_spec = pl.BlockSpec((1, D_MODEL, TOKEN_TILE), lambda i: (i // t_per_seq, 0, i % t_per_seq))
    else:
        kv_shape = jax.ShapeDtypeStruct((n, N_HEADS, HEAD_DIM), F32)
        kv_spec = pl.BlockSpec((TOKEN_TILE, N_HEADS, HEAD_DIM), lambda i: (i, 0, 0))
    return pl.pallas_call(
        functools.partial(_fox_proj_kernel, time_minor=time_minor),
        out_shape=(jax.ShapeDtypeStruct((N_SLABS, n, LANES), BF16), kv_shape, kv_shape,
                   jax.ShapeDtypeStruct((N_SLABS, n, LANES), BF16),
                   jax.ShapeDtypeStruct((N_SLABS, n // FOX_TK, LANES, FOX_TK), BF16),
                   jax.ShapeDtypeStruct((n, LANES), F32)),
        grid=(n // TOKEN_TILE,),
        in_specs=[wide, _resident(w.shape), _resident((D_MODEL, D_MODEL)), _resident((1, LANES))],
        out_specs=(slabs, kv_spec, kv_spec, slabs,
                   pl.BlockSpec((N_SLABS, kb_per_tile, LANES, FOX_TK), lambda i: (0, i, 0, 0)),
                   pl.BlockSpec((TOKEN_TILE, LANES), row)),
        compiler_params=_params(("parallel",)),
        name="fox_proj",
    )(x, w, w_vt, b_f)


def _split3(x):
    hi = x.astype(BF16)
    r1 = x - hi.astype(F32)
    mid = r1.astype(BF16)
    lo = (r1 - mid.astype(F32)).astype(BF16)
    return hi, mid, lo


def _fox_bias_kernel(lf_ref, tri3_ref, place_ref, const_ref, bq_ref, bk_ref, carry_scr):
    t = pl.program_id(1)

    @pl.when(t == 0)
    def _():
        carry_scr[...] = jnp.zeros_like(carry_scr)

    parts = jnp.concatenate(_split3(lf_ref[0]), axis=0)
    c = carry_scr[...] + jnp.dot(tri3_ref[...], parts, preferred_element_type=F32)
    carry_scr[...] = c[BIAS_TILE - 1:BIAS_TILE, :]
    c3 = jnp.concatenate(_split3(c), axis=1)
    both = const_ref[...] + jnp.dot(c3, place_ref[...], preferred_element_type=F32)
    bq_ref[0] = both[:, :LANES].astype(BF16)
    bk_ref[0] = both[:, LANES:].astype(BF16)


def _fox_bias_tables():
    place = [[0.0] * (2 * LANES) for _ in range(3 * LANES)]
    const = [0.0] * (2 * LANES)
    for i in range(3):
        for h in range(N_HEADS):
            const[N_HEADS * i + h] = 1.0
            place[LANES * i + h][N_HEADS * (3 + i) + h] = 1.0
            place[LANES * i + h][LANES + N_HEADS * i + h] = -1.0
            const[LANES + N_HEADS * (3 + i) + h] = 1.0
    return jnp.array(place, BF16), jnp.array([const], F32)


def _fox_bias_call(logf):
    n_seq, n_rows, _ = logf.shape
    place, const = _fox_bias_tables()
    tri = jnp.tril(jnp.ones((BIAS_TILE, BIAS_TILE), BF16))
    tri3 = jnp.concatenate([tri, tri, tri], axis=1)
    blk = pl.BlockSpec((1, BIAS_TILE, LANES), lambda b, t: (b, t, 0))
    return pl.pallas_call(
        _fox_bias_kernel,
        out_shape=(jax.ShapeDtypeStruct((n_seq, n_rows, LANES), BF16),
                   jax.ShapeDtypeStruct((n_seq, n_rows, LANES), BF16)),
        grid=(n_seq, n_rows // BIAS_TILE),
        in_specs=[blk, _resident((BIAS_TILE, 3 * BIAS_TILE)), _resident((3 * LANES, 2 * LANES)),
                  _resident((1, 2 * LANES))],
        out_specs=(blk, blk),
        scratch_shapes=[pltpu.VMEM((1, LANES), F32)],
        compiler_params=_params(("parallel", "arbitrary")),
        name="fox_bias",
    )(logf, tri3, place, const)


def _fox_attn_kernel(q_ref, bq_ref, k_ref, bk_ref, vt_ref, o_ref,
                     qaug_scr, m_scr, acc_scr, s_even, s_odd, smax_even, smax_odd,
                     *, tq, tk, q_pos0):
    qi = pl.program_id(1)
    first_q = q_pos0 + qi * tq
    n_full = first_q // tk
    lane = lax.broadcasted_iota(jnp.int32, (tq, LANES), 1)
    low_lane = lane < HEAD_DIM
    bias_lane = lane < 6 * N_HEADS
    k_off = lax.broadcasted_iota(jnp.int32, (tk, tq), 0)
    q_pos = first_q + lax.broadcasted_iota(jnp.int32, (tk, tq), 1)
    bq = bq_ref[0]
    zero = jnp.zeros_like(bq)
    ones_rows = jnp.ones((FOX_ONES_ROWS, tk), BF16)

    for head in range(N_HEADS):
        q_slab = q_ref[head // HEADS_PER_SLAB]
        own_half = low_lane if head % HEADS_PER_SLAB == 0 else jnp.logical_not(low_lane)
        sel = jnp.logical_and((lane & (N_HEADS - 1)) == head, bias_lane)
        qaug_scr[head] = jnp.concatenate([jnp.where(own_half, q_slab, zero),
                                          jnp.where(sel, bq, zero)], axis=1)
    m_scr[...] = jnp.full_like(m_scr, -jnp.inf)
    acc_scr[...] = jnp.zeros_like(acc_scr)
    slots = ((s_even, smax_even), (s_odd, smax_odd))

    def scores(hp, kb, masked):
        s_buf, smax_buf = slots[hp % 2]
        r0 = pl.multiple_of(kb * tk, tk)
        k_aug = jnp.concatenate([k_ref[hp, pl.ds(r0, tk), :], bk_ref[0, pl.ds(r0, tk), :]], axis=1)
        for par in range(HEADS_PER_SLAB):
            s = lax.dot_general(k_aug, qaug_scr[hp * HEADS_PER_SLAB + par],
                                (((1,), (1,)), ((), ())), preferred_element_type=F32)
            if masked:
                s = jnp.where(kb * tk + k_off <= q_pos, s, -jnp.inf)
            s_buf[par] = s
            smax_buf[par] = jnp.max(s, axis=0, keepdims=True)

    def absorb(hp, kb):
        s_buf, smax_buf = slots[hp % 2]
        vt = vt_ref[hp, kb]
        for par in range(HEADS_PER_SLAB):
            head = hp * HEADS_PER_SLAB + par
            m = m_scr[head]
            m_new = jnp.maximum(m, smax_buf[par])
            p = jnp.exp(s_buf[par] - m_new).astype(BF16)
            vt_aug = jnp.concatenate([vt[par * HEAD_DIM:(par + 1) * HEAD_DIM, :], ones_rows], axis=0)
            acc_scr[head] = (jnp.exp(m - m_new) * acc_scr[head]
                             + jnp.dot(vt_aug, p, preferred_element_type=F32))
            m_scr[head] = m_new

    def key_block(kb, masked):
        scores(0, kb, masked)
        for hp in range(N_SLABS):
            if hp + 1 < N_SLABS:
                scores(hp + 1, kb, masked)
            absorb(hp, kb)

    def full_block(kb, carry):
        key_block(kb, False)
        return carry

    lax.fori_loop(0, n_full, full_block, 0)
    key_block(n_full, True)

    for hp in range(N_SLABS):
        halves = []
        for par in range(HEADS_PER_SLAB):
            acc = acc_scr[hp * HEADS_PER_SLAB + par]
            halves.append(acc[:HEAD_DIM] / acc[HEAD_DIM:HEAD_DIM + 1])
        o_ref[hp] = jnp.concatenate(halves, axis=0).T.astype(BF16)


def _fox_attn_call(q, bq, k, bk, vt, *, n_seq, n_q, n_keys, tq, tk, q_pos0):
    assert tq <= tk and tk % tq == 0 and q_pos0 % tq == 0 and n_keys % tk == 0
    kern = functools.partial(_fox_attn_kernel, tq=tq, tk=tk, q_pos0=q_pos0)
    n_qt = n_q // tq
    q_blk0 = q_pos0 // tq
    n_kb = n_keys // tk
    return pl.pallas_call(
        kern,
        out_shape=jax.ShapeDtypeStruct((N_SLABS, n_seq * n_q, LANES), BF16),
        grid=(n_seq, n_qt),
        in_specs=[pl.BlockSpec((N_SLABS, tq, LANES), lambda b, i: (0, b * n_qt + i, 0)),
                  pl.BlockSpec((1, tq, LANES), lambda b, i: (b, q_blk0 + i, 0)),
                  pl.BlockSpec((N_SLABS, n_keys, LANES), lambda b, i: (0, b, 0)),
                  pl.BlockSpec((1, n_keys, LANES), lambda b, i: (b, 0, 0)),
                  pl.BlockSpec((N_SLABS, n_kb, LANES, tk), lambda b, i: (0, b, 0, 0))],
        out_specs=pl.BlockSpec((N_SLABS, tq, LANES), lambda b, i: (0, b * n_qt + i, 0)),
        scratch_shapes=[pltpu.VMEM((N_HEADS, tq, 2 * LANES), BF16),
                        pltpu.VMEM((N_HEADS, 1, tq), F32),
                        pltpu.VMEM((N_HEADS, HEAD_DIM + FOX_ONES_ROWS, tq), F32),
                        pltpu.VMEM((HEADS_PER_SLAB, tk, tq), F32),
                        pltpu.VMEM((HEADS_PER_SLAB, tk, tq), F32),
                        pltpu.VMEM((HEADS_PER_SLAB, 1, tq), F32),
                        pltpu.VMEM((HEADS_PER_SLAB, 1, tq), F32)],
        compiler_params=_params(("parallel", "arbitrary")),
        name="fox_attn",
    )(q, bq, k, bk, vt)


def _rg_layer(xs, conv_state, h_state, w_in, gate_w, small, layer):
    x_p, x_s = xs
    o_p, nb_p, h_p = _rg_call(
        x_p, jnp.zeros((BATCH, CONV_WIDTH - 1, LRU_WIDTH), F32), jnp.zeros((BATCH, 1, LRU_WIDTH), F32),
        w_in, gate_w, small, layer, n_seq=BATCH, seq_len=SEQ, tile=RG_TILE)
    o_s, nb_s, h_s = _rg_call(
        x_s, conv_state, h_state.reshape(DEC_BATCH, 1, LRU_WIDTH),
        w_in, gate_w, small, layer, n_seq=DEC_BATCH, seq_len=DEC_SEQ, tile=DEC_SEQ)
    return (o_p, o_s), (nb_p, nb_s, h_p, h_s)


def _swa_layer(xs, k_cache, v_cache, w_qkv, sinks, table):
    kvd = SWA_KV_HEADS * HEAD_DIM
    order = jnp.array(_SWA_HEAD_ORDER)
    w_q = (w_qkv[:, :D_MODEL] * ATTN_SCALE).reshape(D_MODEL, N_HEADS, HEAD_DIM)[:, order]
    w = jnp.concatenate([w_q.reshape(D_MODEL, D_MODEL), w_qkv[:, D_MODEL:]], axis=1).astype(BF16)
    bias, sink = _swa_row_tables(table, sinks)

    q_p, k_p, v_p, kb_p, vb_p = _swa_proj_call(xs[0], w)
    pad = ((0, 0), (WINDOW, 0), (0, 0))
    o_p = _swa_attn_call(q_p.reshape(BATCH, SEQ, D_MODEL),
                         jnp.pad(kb_p.reshape(BATCH, SEQ, kvd), pad),
                         jnp.pad(vb_p.reshape(BATCH, SEQ, kvd), pad),
                         bias, sink, n_seq=BATCH, seq_len=SEQ, tile=SWA_TILE, n_invalid=WINDOW)

    q_s, k_s, v_s, kb_s, vb_s = _swa_proj_call(xs[1], w)
    kc = k_cache.reshape(DEC_BATCH, WINDOW, kvd)
    vc = v_cache.reshape(DEC_BATCH, WINDOW, kvd)
    o_s = _swa_attn_call(q_s.reshape(DEC_BATCH, DEC_SEQ, D_MODEL),
                         jnp.concatenate([kc.astype(BF16), kb_s.reshape(DEC_BATCH, DEC_SEQ, kvd)], axis=1),
                         jnp.concatenate([vc.astype(BF16), vb_s.reshape(DEC_BATCH, DEC_SEQ, kvd)], axis=1),
                         bias, sink, n_seq=DEC_BATCH, seq_len=DEC_SEQ, tile=DEC_SEQ, n_invalid=0)

    def tails(new_p, new_s, cache):
        tail_p = new_p.reshape(BATCH, SEQ, kvd)[:, SEQ - WINDOW:]
        tail_s = jnp.concatenate([cache.reshape(DEC_BATCH, WINDOW, kvd)[:, DEC_SEQ:],
                                  new_s.reshape(DEC_BATCH, DEC_SEQ, kvd)], axis=1)
        return (tail_p.reshape(BATCH, WINDOW, SWA_KV_HEADS, HEAD_DIM),
                tail_s.reshape(DEC_BATCH, WINDOW, SWA_KV_HEADS, HEAD_DIM))

    k_tp, k_ts = tails(k_p, k_s, k_cache)
    v_tp, v_ts = tails(v_p, v_s, v_cache)
    return ((o_p.reshape(N_PROMPT, D_MODEL), o_s.reshape(N_SAMPLE, D_MODEL)),
            (k_tp, k_ts, v_tp, v_ts))


def _fox_layer(xs, k_cache, v_cache, logf_cache, w_in, b_f):
    hd = N_HEADS * HEAD_DIM
    w_q = w_in[:, :hd] * ATTN_SCALE
    w_f = jnp.pad(w_in[:, 3 * hd:], ((0, 0), (0, LANES - N_HEADS)))
    w_qkf = jnp.concatenate([w_q, w_in[:, hd:2 * hd], w_f], axis=1).astype(BF16)
    w = jnp.concatenate([w_q, w_in[:, hd:3 * hd], w_f], axis=1).astype(BF16)
    bf = jnp.pad(b_f.astype(F32), (0, LANES - N_HEADS)).reshape(1, LANES)
    w_vt = w_in[:, 2 * hd:3 * hd].T.astype(BF16)

    q_p, kt_p, vt32_p, kb_p, vt_p, lf_p = _fox_proj_call(xs[0], w_qkf, w_vt, bf, seq_len=SEQ,
                                                         time_minor=True)
    lf_p = lf_p.reshape(BATCH, SEQ, LANES)
    bq_p, bk_p = _fox_bias_call(lf_p)
    o_p = _fox_attn_call(q_p, bq_p, kb_p, bk_p, vt_p, n_seq=BATCH, n_q=SEQ, n_keys=SEQ,
                         tq=FOX_TQ, tk=FOX_TK, q_pos0=0)

    q_s, k_s, v_s, kb_s, vt_s, lf_s = _fox_proj_call(xs[1], w, w_vt, bf, seq_len=DEC_SEQ,
                                                     time_minor=False)
    lf_s = lf_s.reshape(DEC_BATCH, DEC_SEQ, LANES)
    tail = FOX_SAMPLE_KEYS - PAST_LEN - DEC_SEQ
    n_kb = FOX_SAMPLE_KEYS // FOX_TK
    lf_cache = jnp.pad(logf_cache.astype(F32), ((0, 0), (0, 0), (0, LANES - N_HEADS)))
    bq_s, bk_s = _fox_bias_call(jnp.pad(jnp.concatenate([lf_cache, lf_s], axis=1),
                                        ((0, 0), (0, tail), (0, 0))))
    k_old = k_cache.astype(BF16).reshape(DEC_BATCH, PAST_LEN, N_SLABS, LANES).transpose(2, 0, 1, 3)
    k_new = kb_s.reshape(N_SLABS, DEC_BATCH, DEC_SEQ, LANES)
    k_all = jnp.pad(jnp.concatenate([k_old, k_new], axis=2), ((0, 0), (0, 0), (0, tail), (0, 0)))
    k_all = k_all.reshape(N_SLABS, DEC_BATCH * FOX_SAMPLE_KEYS, LANES)
    v_old = v_cache.astype(BF16).reshape(DEC_BATCH, PAST_LEN, N_SLABS, LANES).transpose(2, 0, 3, 1)
    per_blk = FOX_TK // DEC_SEQ
    v_new = vt_s.reshape(N_SLABS, N_SAMPLE // FOX_TK, LANES, per_blk, DEC_SEQ)
    v_new = v_new.transpose(0, 1, 3, 2, 4).reshape(N_SLABS, DEC_BATCH, LANES, DEC_SEQ)
    vt_all = jnp.pad(jnp.concatenate([v_old, v_new], axis=3), ((0, 0), (0, 0), (0, 0), (0, tail)))
    vt_all = vt_all.reshape(N_SLABS, DEC_BATCH, LANES, n_kb, FOX_TK).transpose(0, 1, 3, 2, 4)
    vt_all = vt_all.reshape(N_SLABS, DEC_BATCH * n_kb, LANES, FOX_TK)
    o_s = _fox_attn_call(q_s, bq_s, k_all, bk_s, vt_all, n_seq=DEC_BATCH, n_q=DEC_SEQ,
                         n_keys=FOX_SAMPLE_KEYS, tq=DEC_SEQ, tk=FOX_TK, q_pos0=PAST_LEN)

    def heads_last(a):
        return a.reshape(BATCH, N_HEADS, HEAD_DIM, SEQ).transpose(0, 3, 1, 2)

    outs = (heads_last(kt_p), k_s.reshape(DEC_BATCH, DEC_SEQ, N_HEADS, HEAD_DIM),
            heads_last(vt32_p), v_s.reshape(DEC_BATCH, DEC_SEQ, N_HEADS, HEAD_DIM),
            lf_p[..., :N_HEADS], lf_s[..., :N_HEADS])
    return (o_p, o_s), outs


def kernel(x_prompt, x_sample, state_rg_conv, state_rg_h, cache_swa_k, cache_swa_v, cache_fox_k, cache_fox_v, cache_fox_logf, ln_gain, ln_bias, ffn_w_up, ffn_w_down, rg_w_in, rg_conv_w, rg_conv_b, rg_gate_w, rg_gate_b, rg_lambda, rg_w_out, swa_w_qkv, swa_sinks, swa_w_out, rel_bias_table, fox_w_in, fox_b_f, fox_w_out):
    xs = (x_prompt.reshape(N_PROMPT, D_MODEL), x_sample.reshape(N_SAMPLE, D_MODEL))
    w_up = ffn_w_up.astype(BF16)
    w_down = ffn_w_down.astype(BF16)
    rg_in = rg_w_in.astype(BF16)
    rg_gate = rg_gate_w.astype(BF16)
    swa_order = jnp.array(_SWA_HEAD_ORDER)
    n_swa = swa_w_out.shape[0]
    w_out = {0: rg_w_out.astype(BF16),
             1: swa_w_out.reshape(n_swa, N_HEADS, HEAD_DIM, D_MODEL)[:, swa_order]
                         .reshape(n_swa, D_MODEL, D_MODEL).astype(BF16),
             2: fox_w_out.astype(BF16)}
    rg_out, swa_out, fox_out = [], [], []
    for i in range(DEPTH):
        kind, j = i % 3, i // 3
        if kind == 0:
            small = (rg_conv_w[j], rg_conv_b[j].reshape(1, LRU_WIDTH),
                     rg_gate_b[j].reshape(LRU_BLOCKS, 1, 2 * LRU_BLOCK_W), rg_lambda[j].reshape(1, LRU_WIDTH))
            os_, extra = _rg_layer(xs, state_rg_conv[j], state_rg_h[j], rg_in, rg_gate, small, j)
            rg_out.append(extra)
        elif kind == 1:
            os_, extra = _swa_layer(xs, cache_swa_k[j], cache_swa_v[j], swa_w_qkv[j], swa_sinks[j],
                                    rel_bias_table)
            swa_out.append(extra)
        else:
            os_, extra = _fox_layer(xs, cache_fox_k[j], cache_fox_v[j], cache_fox_logf[j], fox_w_in[j],
                                    fox_b_f[j])
            fox_out.append(extra)
        xs = tuple(_post_call(x, o, w_out[kind], j, w_up, w_down, ln_gain, ln_bias, i)
                   for x, o in zip(xs, os_))

    def stack(items, idx):
        return jnp.stack([it[idx] for it in items])

    return (xs[0].reshape(BATCH, SEQ, D_MODEL), xs[1].reshape(DEC_BATCH, DEC_SEQ, D_MODEL),
            stack(rg_out, 0), stack(rg_out, 1), stack(rg_out, 2), stack(rg_out, 3),
            stack(swa_out, 0), stack(swa_out, 1), stack(swa_out, 2), stack(swa_out, 3),
            stack(fox_out, 0), stack(fox_out, 1), stack(fox_out, 2), stack(fox_out, 3),
            stack(fox_out, 4), stack(fox_out, 5))
```

```python
import functools
import math

import jax
import jax.numpy as jnp
from jax import lax
from jax.experimental import pallas as pl
from jax.experimental.pallas import tpu as pltpu

F32 = jnp.float32
BF16 = jnp.bfloat16

D_MODEL = 1024
BATCH = 4
SEQ = 4096
DEPTH = 4
DEC_BATCH = 16
DEC_SEQ = 64
PAST_LEN = 1024
CHUNK = 64
D_FF = 4 * D_MODEL
HEAD_DIM = 64
N_HEADS = 16
SWA_KV_HEADS = 4
SWA_GROUP = 4
WINDOW = 128
LRU_WIDTH = D_MODEL
LRU_BLOCKS = 4
LRU_BLOCK_W = LRU_WIDTH // LRU_BLOCKS
CONV_WIDTH = 4
LRU_C = 8.0
N_BUCKETS = 32
MAX_DISTANCE = 128
ALPHA = (2.0 * DEPTH) ** 0.25
LN_EPS = 1e-5
ATTN_SCALE = HEAD_DIM ** -0.5

N_PROMPT = BATCH * SEQ
N_SAMPLE = DEC_BATCH * DEC_SEQ

LANES = 128
HEADS_PER_SLAB = LANES // HEAD_DIM
N_SLABS = D_MODEL // LANES

TOKEN_TILE = 512
POST_TILE = 512
POST_ROWS = 256
FF_CHUNK = 1024
RG_TILE = 512
SWA_TILE = 512
SWA_KEYS = WINDOW + CHUNK
FOX_TQ = 512
FOX_TK = 512
FOX_ONES_ROWS = 16
BIAS_TILE = 512
VMEM_LIMIT = 56 * 1024 * 1024


def _resident(shape, layer=None):
    zeros = (0,) * len(shape)
    if layer is None:
        return pl.BlockSpec(shape, lambda *_: zeros, pipeline_mode=pl.Buffered(1))
    return pl.BlockSpec((None,) + tuple(shape), lambda *_: (layer,) + zeros,
                        pipeline_mode=pl.Buffered(1))


def _params(semantics):
    return pltpu.CompilerParams(dimension_semantics=semantics, vmem_limit_bytes=VMEM_LIMIT)


def _softplus(x):
    return jnp.maximum(x, 0.0) + jnp.log1p(jnp.exp(-jnp.abs(x)))


def _layer_norm(z, g, b):
    mu = jnp.mean(z, axis=-1, keepdims=True)
    zc = z - mu
    var = jnp.mean(zc * zc, axis=-1, keepdims=True)
    return zc * lax.rsqrt(var + LN_EPS) * g + b


def _post_kernel(x_ref, o_ref, wout_ref, wup_ref, wdn_ref, g_ref, b_ref, out_ref, *, slab_major):
    groups = [slice(r * POST_ROWS, (r + 1) * POST_ROWS) for r in range(POST_TILE // POST_ROWS)]
    y = []
    for rows in groups:
        if slab_major:
            o = jnp.concatenate([o_ref[s, rows, :] for s in range(N_SLABS)], axis=1)
        else:
            o = o_ref[rows, :]
        y.append(jnp.dot(o, wout_ref[...], preferred_element_type=F32))
    x1 = [_layer_norm(ALPHA * x_ref[rows, :] + y_r, g_ref[0:1, :], b_ref[0:1, :])
          for rows, y_r in zip(groups, y)]
    for rows, x1_r in zip(groups, x1):
        x1b = x1_r.astype(BF16)
        acc = jnp.zeros_like(x1_r)
        for c in range(D_FF // FF_CHUNK):
            cols = slice(c * FF_CHUNK, (c + 1) * FF_CHUNK)
            h = jnp.dot(x1b, wup_ref[:, cols], preferred_element_type=F32)
            a = jnp.square(jnp.maximum(h, 0.0)).astype(BF16)
            acc = acc + jnp.dot(a, wdn_ref[cols, :], preferred_element_type=F32)
        out_ref[rows, :] = _layer_norm(ALPHA * x1_r + acc, g_ref[1:2, :], b_ref[1:2, :])


def _post_call(x, o, w_out, mixer_idx, w_up, w_down, gain, bias, layer):
    n = x.shape[0]
    tile = pl.BlockSpec((POST_TILE, D_MODEL), lambda i: (i, 0))
    slab_major = o.ndim == 3
    o_tile = pl.BlockSpec((N_SLABS, POST_TILE, LANES), lambda i: (0, i, 0)) if slab_major else tile
    return pl.pallas_call(
        functools.partial(_post_kernel, slab_major=slab_major),
        out_shape=jax.ShapeDtypeStruct((n, D_MODEL), F32),
        grid=(n // POST_TILE,),
        in_specs=[tile, o_tile, _resident((D_MODEL, D_MODEL), mixer_idx),
                  _resident((D_MODEL, D_FF), layer), _resident((D_FF, D_MODEL), layer),
                  _resident((2, D_MODEL), layer), _resident((2, D_MODEL), layer)],
        out_specs=tile,
        compiler_params=_params(("parallel",)),
        name="post",
    )(x, o, w_out, w_up, w_down, gain, bias)


def _rg_kernel(x_ref, cbuf_ref, h0_ref, win_ref, cw_ref, cb_ref, gw_ref, gb_ref, lam_ref,
               o_ref, nbuf_ref, hlast_ref, u_scr, h_scr, hc_scr, *, tile):
    t = pl.program_id(1)
    hist = CONV_WIDTH - 1
    pad = 8

    @pl.when(t == 0)
    def _():
        u_scr[pad - hist:pad, :] = cbuf_ref[0]
        hc_scr[...] = h0_ref[0]

    xb = x_ref[...].astype(BF16)
    gu = jnp.dot(xb, win_ref[...], preferred_element_type=F32)
    gate = gu[:, :LRU_WIDTH]
    u = gu[:, LRU_WIDTH:]
    u_scr[pad:pad + tile, :] = u
    conv = cb_ref[...] + u * cw_ref[hist:hist + 1, :]
    for k in range(hist):
        conv = conv + u_scr[pad - hist + k:pad - hist + k + tile, :] * cw_ref[k:k + 1, :]
    nbuf_ref[0] = u_scr[pad + tile - hist:pad + tile, :]
    u_scr[pad - hist:pad, :] = u_scr[pad + tile - hist:pad + tile, :]

    row = lax.broadcasted_iota(jnp.int32, (8, LRU_BLOCK_W), 0)
    for n in range(LRU_BLOCKS):
        cols = slice(n * LRU_BLOCK_W, (n + 1) * LRU_BLOCK_W)
        cn = conv[:, cols]
        g = jnp.dot(cn.astype(BF16), gw_ref[n], preferred_element_type=F32) + gb_ref[n]
        r = jax.nn.sigmoid(g[:, :LRU_BLOCK_W])
        ig = jax.nn.sigmoid(g[:, LRU_BLOCK_W:])
        log_a = -LRU_C * r * _softplus(-lam_ref[:, cols])
        a_blk = jnp.exp(log_a)
        var = 1.0 - a_blk * a_blk
        root = jnp.where(var > 0.0, var * lax.rsqrt(var), 0.0)
        b_blk = root * (ig * cn)
        h_prev = hc_scr[:, cols]
        for j in range(tile // 8):
            a = a_blk[8 * j:8 * j + 8, :]
            b = b_blk[8 * j:8 * j + 8, :]
            for d in (1, 2, 4):
                a_sh = jnp.where(row >= d, pltpu.roll(a, d, 0), 1.0)
                b_sh = jnp.where(row >= d, pltpu.roll(b, d, 0), 0.0)
                b = a * b_sh + b
                a = a * a_sh
            h = a * h_prev + b
            h_scr[8 * j:8 * j + 8, cols] = h
            h_prev = h[7:8, :]
        hc_scr[:, cols] = h_prev

    hlast_ref[0] = hc_scr[...]
    o_ref[...] = (h_scr[...] * jax.nn.gelu(gate)).astype(BF16)


def _rg_call(x, conv_buf, h0, w_in, gate_w, small, layer, *, n_seq, seq_len, tile):
    conv_w, conv_b, gate_b, lam = small
    n_t = seq_len // tile
    kern = functools.partial(_rg_kernel, tile=tile)
    o, nbuf, hlast = pl.pallas_call(
        kern,
        out_shape=(jax.ShapeDtypeStruct((n_seq * seq_len, D_MODEL), BF16),
                   jax.ShapeDtypeStruct((n_seq, CONV_WIDTH - 1, LRU_WIDTH), F32),
                   jax.ShapeDtypeStruct((n_seq, 1, LRU_WIDTH), F32)),
        grid=(n_seq, n_t),
        in_specs=[pl.BlockSpec((tile, D_MODEL), lambda b, t: (b * n_t + t, 0)),
                  pl.BlockSpec((1, CONV_WIDTH - 1, LRU_WIDTH), lambda b, t: (b, 0, 0)),
                  pl.BlockSpec((1, 1, LRU_WIDTH), lambda b, t: (b, 0, 0)),
                  _resident((D_MODEL, 2 * LRU_WIDTH), layer),
                  _resident((CONV_WIDTH, LRU_WIDTH)),
                  _resident((1, LRU_WIDTH)),
                  _resident((LRU_BLOCKS, LRU_BLOCK_W, 2 * LRU_BLOCK_W), layer),
                  _resident((LRU_BLOCKS, 1, 2 * LRU_BLOCK_W)),
                  _resident((1, LRU_WIDTH))],
        out_specs=(pl.BlockSpec((tile, D_MODEL), lambda b, t: (b * n_t + t, 0)),
                   pl.BlockSpec((1, CONV_WIDTH - 1, LRU_WIDTH), lambda b, t: (b, 0, 0)),
                   pl.BlockSpec((1, 1, LRU_WIDTH), lambda b, t: (b, 0, 0))),
        scratch_shapes=[pltpu.VMEM((tile + 8, LRU_WIDTH), F32),
                        pltpu.VMEM((tile, LRU_WIDTH), F32),
                        pltpu.VMEM((1, LRU_WIDTH), F32)],
        compiler_params=_params(("parallel", "arbitrary")),
        name="rg_mixer",
    )(x, conv_buf, h0, w_in, conv_w, conv_b, gate_w, gate_b, lam)
    return o, nbuf, hlast[:, 0, :]


def _swa_proj_kernel(x_ref, w_ref, q_ref, k_ref, v_ref, kb_ref, vb_ref):
    xb = x_ref[...].astype(BF16)
    qkv = jnp.dot(xb, w_ref[...], preferred_element_type=F32)
    kvd = SWA_KV_HEADS * HEAD_DIM
    q_ref[...] = qkv[:, :D_MODEL].astype(BF16)
    k = qkv[:, D_MODEL:D_MODEL + kvd]
    v = qkv[:, D_MODEL + kvd:]
    k_ref[...] = k
    v_ref[...] = v
    kb_ref[...] = k.astype(BF16)
    vb_ref[...] = v.astype(BF16)


def _swa_proj_call(x, w):
    n = x.shape[0]
    kvd = SWA_KV_HEADS * HEAD_DIM
    row = lambda i: (i, 0)
    return pl.pallas_call(
        _swa_proj_kernel,
        out_shape=(jax.ShapeDtypeStruct((n, D_MODEL), BF16),
                   jax.ShapeDtypeStruct((n, kvd), F32),
                   jax.ShapeDtypeStruct((n, kvd), F32),
                   jax.ShapeDtypeStruct((n, kvd), BF16),
                   jax.ShapeDtypeStruct((n, kvd), BF16)),
        grid=(n // TOKEN_TILE,),
        in_specs=[pl.BlockSpec((TOKEN_TILE, D_MODEL), row), _resident((D_MODEL, D_MODEL + 2 * kvd))],
        out_specs=(pl.BlockSpec((TOKEN_TILE, D_MODEL), row),
                   pl.BlockSpec((TOKEN_TILE, kvd), row), pl.BlockSpec((TOKEN_TILE, kvd), row),
                   pl.BlockSpec((TOKEN_TILE, kvd), row), pl.BlockSpec((TOKEN_TILE, kvd), row)),
        compiler_params=_params(("parallel",)),
        name="swa_proj",
    )(x, w)


def _swa_attn_kernel(q_ref, k_ref, v_ref, bias_ref, sink_ref, o_ref, s_even, s_odd,
                     *, tile, n_invalid):
    ti = pl.program_id(1)
    cols = SWA_GROUP * HEADS_PER_SLAB * CHUNK
    half_cols = cols // HEADS_PER_SLAB
    lane_q = lax.broadcasted_iota(jnp.int32, (CHUNK, LANES), 1)
    low_q = lane_q < HEAD_DIM
    key_idx = lax.broadcasted_iota(jnp.int32, (SWA_KEYS, cols), 0)
    n_kv_slabs = SWA_KV_HEADS // HEADS_PER_SLAB
    units = [(c, t) for c in range(tile // CHUNK) for t in range(n_kv_slabs)]
    s_bufs = (s_even, s_odd)

    def window_start(c):
        return pl.multiple_of(ti * tile + c * CHUNK, CHUNK)

    def scores(u):
        c, t = units[u]
        row0 = window_start(c)
        kw = k_ref[0, pl.ds(row0, SWA_KEYS), t * LANES:(t + 1) * LANES]
        slabs = [q_ref[0, c * CHUNK:(c + 1) * CHUNK,
                       (SWA_GROUP * t + g) * LANES:(SWA_GROUP * t + g + 1) * LANES]
                 for g in range(SWA_GROUP)]
        zero = jnp.zeros_like(slabs[0])
        q_stack = jnp.concatenate([jnp.where(low_q, s_, zero) for s_ in slabs]
                                  + [jnp.where(low_q, zero, s_) for s_ in slabs], axis=0)
        s = lax.dot_general(kw, q_stack, (((1,), (1,)), ((), ())), preferred_element_type=F32)
        s = s + bias_ref[t]
        if n_invalid > 0 and c * CHUNK < n_invalid:
            s = jnp.where(key_idx + row0 >= n_invalid, s, -jnp.inf)
        s_bufs[u % 2][...] = s

    def attend(u):
        c, t = units[u]
        s = s_bufs[u % 2][...]
        vw = v_ref[0, pl.ds(window_start(c), SWA_KEYS), t * LANES:(t + 1) * LANES]
        sink = sink_ref[t]
        m = jnp.maximum(jnp.max(s, axis=0, keepdims=True), sink)
        p = jnp.exp(s - m)
        denom = jnp.sum(p, axis=0, keepdims=True) + jnp.exp(sink - m)
        out_t = lax.dot_general(vw, p.astype(BF16), (((0,), (0,)), ((), ())),
                                preferred_element_type=F32) / denom
        pair = jnp.concatenate([out_t[:HEAD_DIM, :half_cols], out_t[HEAD_DIM:, half_cols:]], axis=0)
        pair = pair.T.astype(BF16)
        for g in range(SWA_GROUP):
            s_idx = SWA_GROUP * t + g
            o_ref[0, c * CHUNK:(c + 1) * CHUNK, s_idx * LANES:(s_idx + 1) * LANES] = (
                pair[g * CHUNK:(g + 1) * CHUNK, :])

    scores(0)
    for u in range(len(units)):
        if u + 1 < len(units):
            scores(u + 1)
        attend(u)


def _swa_attn_call(q, k_win, v_win, bias, sink, *, n_seq, seq_len, tile, n_invalid):
    kvd = SWA_KV_HEADS * HEAD_DIM
    n_keys = k_win.shape[1]
    cols = SWA_GROUP * HEADS_PER_SLAB * CHUNK
    n_kv_slabs = SWA_KV_HEADS // HEADS_PER_SLAB
    kern = functools.partial(_swa_attn_kernel, tile=tile, n_invalid=n_invalid)
    return pl.pallas_call(
        kern,
        out_shape=jax.ShapeDtypeStruct((n_seq, seq_len, D_MODEL), BF16),
        grid=(n_seq, seq_len // tile),
        in_specs=[pl.BlockSpec((1, tile, D_MODEL), lambda b, t: (b, t, 0)),
                  pl.BlockSpec((1, n_keys, kvd), lambda b, t: (b, 0, 0)),
                  pl.BlockSpec((1, n_keys, kvd), lambda b, t: (b, 0, 0)),
                  _resident((n_kv_slabs, SWA_KEYS, cols)),
                  _resident((n_kv_slabs, 1, cols))],
        out_specs=pl.BlockSpec((1, tile, D_MODEL), lambda b, t: (b, t, 0)),
        scratch_shapes=[pltpu.VMEM((SWA_KEYS, cols), F32), pltpu.VMEM((SWA_KEYS, cols), F32)],
        compiler_params=_params(("parallel", "arbitrary")),
        name="swa_attn",
    )(q, k_win, v_win, bias, sink)


def _t5_bucket(rel):
    half = N_BUCKETS // 2
    max_exact = half // 2
    n = jnp.abs(rel)
    n_f = jnp.maximum(n, 1).astype(jnp.float32)
    large = max_exact + (jnp.log(n_f / max_exact) / math.log(MAX_DISTANCE / max_exact)
                         * (half - max_exact)).astype(jnp.int32)
    large = jnp.minimum(large, half - 1)
    return jnp.where(rel > 0, half, 0) + jnp.where(n < max_exact, n, large)


_SWA_HEAD_ORDER = [8 * t + 4 * p + g for t in range(2) for g in range(4) for p in range(2)]


def _swa_row_tables(table, sinks):
    rel = jnp.arange(SWA_KEYS)[None, :] - WINDOW - jnp.arange(CHUNK)[:, None]
    bucket = _t5_bucket(rel)
    tbl = table.astype(F32)
    hit = bucket[None, None] == jnp.arange(N_BUCKETS)[:, None, None, None]
    bias = jnp.sum(jnp.where(hit, tbl[:, :, None, None], 0.0), axis=0)
    n_kv_slabs = SWA_KV_HEADS // HEADS_PER_SLAB
    cols = SWA_GROUP * HEADS_PER_SLAB * CHUNK
    bias = bias.reshape(n_kv_slabs, cols, SWA_KEYS).transpose(0, 2, 1)
    sink = jnp.repeat(sinks.astype(F32), CHUNK).reshape(n_kv_slabs, 1, cols)
    return bias, sink


def _fox_proj_kernel(x_ref, w_ref, wvt_ref, bf_ref, q_ref, k_ref, v_ref, kb_ref, vt_ref, lf_ref,
                     *, time_minor):
    xb = x_ref[...].astype(BF16)
    proj = jnp.dot(xb, w_ref[...], preferred_element_type=F32)
    k = proj[:, D_MODEL:2 * D_MODEL]
    vt = lax.dot_general(wvt_ref[...], xb, (((1,), (1,)), ((), ())), preferred_element_type=F32)
    if time_minor:
        k_ref[0] = k.T
        v_ref[0] = vt
        z = proj[:, 2 * D_MODEL:]
    else:
        k_ref[...] = k.reshape(TOKEN_TILE, N_HEADS, HEAD_DIM)
        v_ref[...] = proj[:, 2 * D_MODEL:3 * D_MODEL].reshape(TOKEN_TILE, N_HEADS, HEAD_DIM)
        z = proj[:, 3 * D_MODEL:]
    for s in range(N_SLABS):
        cols = slice(s * LANES, (s + 1) * LANES)
        q_ref[s] = proj[:, cols].astype(BF16)
        kb_ref[s] = k[:, cols].astype(BF16)
        for j in range(TOKEN_TILE // FOX_TK):
            vt_ref[s, j] = vt[cols, j * FOX_TK:(j + 1) * FOX_TK].astype(BF16)
    lf_ref[...] = -_softplus(-(z + bf_ref[...]))


def _fox_proj_call(x, w, w_vt, b_f, *, seq_len, time_minor):
    n = x.shape[0]
    row = lambda i: (i, 0)
    wide = pl.BlockSpec((TOKEN_TILE, D_MODEL), row)
    slabs = pl.BlockSpec((N_SLABS, TOKEN_TILE, LANES), lambda i: (0, i, 0))
    kb_per_tile = TOKEN_TILE // FOX_TK
    if time_minor:
        t_per_seq = seq_len // TOKEN_TILE
        kv_shape = jax.ShapeDtypeStruct((n // seq_len, D_MODEL, seq_len), F32)
        kv_spec = pl.BlockSpec((1, D_MODEL, TOKEN_TILE), lambda i: (i // t_per_seq, 0, i % t_per_seq))
    else:
        kv_shape = jax.ShapeDtypeStruct((n, N_HEADS, HEAD_DIM), F32)
        kv_spec = pl.BlockSpec((TOKEN_TILE, N_HEADS, HEAD_DIM), lambda i: (i, 0, 0))
    return pl.pallas_call(
        functools.partial(_fox_proj_kernel, time_minor=time_minor),
        out_shape=(jax.ShapeDtypeStruct((N_SLABS, n, LANES), BF16), kv_shape, kv_shape,
                   jax.ShapeDtypeStruct((N_SLABS, n, LANES), BF16),
                   jax.ShapeDtypeStruct((N_SLABS, n // FOX_TK, LANES, FOX_TK), BF16),
                   jax.ShapeDtypeStruct((n, LANES), F32)),
        grid=(n // TOKEN_TILE,),
        in_specs=[wide, _resident(w.shape), _resident((D_MODEL, D_MODEL)), _resident((1, LANES))],
        out_specs=(slabs, kv_spec, kv_spec, slabs,
                   pl.BlockSpec((N_SLABS, kb_per_tile, LANES, FOX_TK), lambda i: (0, i, 0, 0)),
                   pl.BlockSpec((TOKEN_TILE, LANES), row)),
        compiler_params=_params(("parallel",)),
        name="fox_proj",
    )(x, w, w_vt, b_f)


def _lane_split3(x, lane):
    hi = x.astype(BF16)
    r1 = x - hi.astype(F32)
    mid = r1.astype(BF16)
    lo = (r1 - mid.astype(F32)).astype(BF16)
    return jnp.where(lane < N_HEADS, hi, jnp.where(lane < 2 * N_HEADS, mid, lo))


def _fox_bias_kernel(lf_ref, tri_ref, place_ref, const_ref, bq_ref, bk_ref, carry_scr):
    t = pl.program_id(1)

    @pl.when(t == 0)
    def _():
        carry_scr[...] = jnp.zeros_like(carry_scr)

    lane = lax.broadcasted_iota(jnp.int32, (BIAS_TILE, LANES), 1)
    sums = carry_scr[...] + jnp.dot(tri_ref[...], _lane_split3(lf_ref[0], lane),
                                    preferred_element_type=F32)
    carry_scr[...] = sums[BIAS_TILE - 1:BIAS_TILE, :]
    c = sums + pltpu.roll(sums, LANES - N_HEADS, 1) + pltpu.roll(sums, LANES - 2 * N_HEADS, 1)
    c = jnp.where(lane < N_HEADS, c, 0.0)
    c = c + pltpu.roll(c, N_HEADS, 1) + pltpu.roll(c, 2 * N_HEADS, 1)
    both = const_ref[...] + jnp.dot(_lane_split3(c, lane), place_ref[...],
                                    preferred_element_type=F32)
    bq_ref[0] = both[:, :LANES].astype(BF16)
    bk_ref[0] = both[:, LANES:].astype(BF16)


def _fox_bias_tables():
    place = [[0.0] * (2 * LANES) for _ in range(LANES)]
    const = [0.0] * (2 * LANES)
    for i in range(3):
        for h in range(N_HEADS):
            const[N_HEADS * i + h] = 1.0
            place[N_HEADS * i + h][N_HEADS * (3 + i) + h] = 1.0
            place[N_HEADS * i + h][LANES + N_HEADS * i + h] = -1.0
            const[LANES + N_HEADS * (3 + i) + h] = 1.0
    return jnp.array(place, BF16), jnp.array([const], F32)


def _fox_bias_call(logf):
    n_seq, n_rows, _ = logf.shape
    place, const = _fox_bias_tables()
    tri = jnp.tril(jnp.ones((BIAS_TILE, BIAS_TILE), BF16))
    blk = pl.BlockSpec((1, BIAS_TILE, LANES), lambda b, t: (b, t, 0))
    return pl.pallas_call(
        _fox_bias_kernel,
        out_shape=(jax.ShapeDtypeStruct((n_seq, n_rows, LANES), BF16),
                   jax.ShapeDtypeStruct((n_seq, n_rows, LANES), BF16)),
        grid=(n_seq, n_rows // BIAS_TILE),
        in_specs=[blk, _resident((BIAS_TILE, BIAS_TILE)), _resident((LANES, 2 * LANES)),
                  _resident((1, 2 * LANES))],
        out_specs=(blk, blk),
        scratch_shapes=[pltpu.VMEM((1, LANES), F32)],
        compiler_params=_params(("parallel", "arbitrary")),
        name="fox_bias",
    )(logf, tri, place, const)


def _fox_attn_core(q_ref, bq_ref, bk_ref, o_ref, scratch, *, tq, tk, first_q, run_blocks):
    qaug_scr, m_scr, acc_scr, s_even, s_odd, smax_even, smax_odd = scratch
    lane = lax.broadcasted_iota(jnp.int32, (tq, LANES), 1)
    low_lane = lane < HEAD_DIM
    bias_lane = lane < 6 * N_HEADS
    k_off = lax.broadcasted_iota(jnp.int32, (tk, tq), 0)
    q_pos = first_q + lax.broadcasted_iota(jnp.int32, (tk, tq), 1)
    bq = bq_ref[0]
    zero = jnp.zeros_like(bq)
    ones_rows = jnp.ones((FOX_ONES_ROWS, tk), BF16)

    for head in range(N_HEADS):
        q_slab = q_ref[head // HEADS_PER_SLAB]
        own_half = low_lane if head % HEADS_PER_SLAB == 0 else jnp.logical_not(low_lane)
        sel = jnp.logical_and((lane & (N_HEADS - 1)) == head, bias_lane)
        qaug_scr[head] = jnp.concatenate([jnp.where(own_half, q_slab, zero),
                                          jnp.where(sel, bq, zero)], axis=1)
    m_scr[...] = jnp.full_like(m_scr, -jnp.inf)
    acc_scr[...] = jnp.zeros_like(acc_scr)
    slots = ((s_even, smax_even), (s_odd, smax_odd))

    def key_block(kb, masked, k_of, vt_of):
        bk_blk = bk_ref[0, pl.ds(pl.multiple_of(kb * tk, tk), tk), :]

        def scores(hp):
            s_buf, smax_buf = slots[hp % 2]
            k_aug = jnp.concatenate([k_of(hp), bk_blk], axis=1)
            for par in range(HEADS_PER_SLAB):
                s = lax.dot_general(k_aug, qaug_scr[hp * HEADS_PER_SLAB + par],
                                    (((1,), (1,)), ((), ())), preferred_element_type=F32)
                if masked:
                    s = jnp.where(kb * tk + k_off <= q_pos, s, -jnp.inf)
                s_buf[par] = s
                smax_buf[par] = jnp.max(s, axis=0, keepdims=True)

        def absorb(hp):
            s_buf, smax_buf = slots[hp % 2]
            vt = vt_of(hp)
            for par in range(HEADS_PER_SLAB):
                head = hp * HEADS_PER_SLAB + par
                m = m_scr[head]
                m_new = jnp.maximum(m, smax_buf[par])
                p = jnp.exp(s_buf[par] - m_new).astype(BF16)
                vt_aug = jnp.concatenate([vt[par * HEAD_DIM:(par + 1) * HEAD_DIM, :], ones_rows], axis=0)
                acc_scr[head] = (jnp.exp(m - m_new) * acc_scr[head]
                                 + jnp.dot(vt_aug, p, preferred_element_type=F32))
                m_scr[head] = m_new

        scores(0)
        for hp in range(N_SLABS):
            if hp + 1 < N_SLABS:
                scores(hp + 1)
            absorb(hp)

    run_blocks(key_block)

    for hp in range(N_SLABS):
        halves = []
        for par in range(HEADS_PER_SLAB):
            acc = acc_scr[hp * HEADS_PER_SLAB + par]
            halves.append(acc[:HEAD_DIM] / acc[HEAD_DIM:HEAD_DIM + 1])
        o_ref[hp] = jnp.concatenate(halves, axis=0).T.astype(BF16)


def _fox_attn_scratch(tq, tk):
    return [pltpu.VMEM((N_HEADS, tq, 2 * LANES), BF16),
            pltpu.VMEM((N_HEADS, 1, tq), F32),
            pltpu.VMEM((N_HEADS, HEAD_DIM + FOX_ONES_ROWS, tq), F32),
            pltpu.VMEM((HEADS_PER_SLAB, tk, tq), F32),
            pltpu.VMEM((HEADS_PER_SLAB, tk, tq), F32),
            pltpu.VMEM((HEADS_PER_SLAB, 1, tq), F32),
            pltpu.VMEM((HEADS_PER_SLAB, 1, tq), F32)]


def _fox_attn_kernel(q_ref, bq_ref, k_ref, bk_ref, vt_ref, o_ref, *scratch, tq, tk):
    first_q = pl.program_id(1) * tq
    n_full = first_q // tk

    def run_blocks(key_block):
        def block(kb, masked):
            r0 = pl.multiple_of(kb * tk, tk)
            key_block(kb, masked, lambda hp: k_ref[hp, pl.ds(r0, tk), :], lambda hp: vt_ref[hp, kb])

        def full_block(kb, carry):
            block(kb, False)
            return carry

        lax.fori_loop(0, n_full, full_block, 0)
        block(n_full, True)

    _fox_attn_core(q_ref, bq_ref, bk_ref, o_ref, scratch, tq=tq, tk=tk, first_q=first_q,
                   run_blocks=run_blocks)


def _fox_attn_call(q, bq, k, bk, vt, *, n_seq, seq_len, tq, tk):
    assert tq <= tk and tk % tq == 0 and seq_len % tk == 0
    n_qt = seq_len // tq
    n_kb = seq_len // tk
    return pl.pallas_call(
        functools.partial(_fox_attn_kernel, tq=tq, tk=tk),
        out_shape=jax.ShapeDtypeStruct((N_SLABS, n_seq * seq_len, LANES), BF16),
        grid=(n_seq, n_qt),
        in_specs=[pl.BlockSpec((N_SLABS, tq, LANES), lambda b, i: (0, b * n_qt + i, 0)),
                  pl.BlockSpec((1, tq, LANES), lambda b, i: (b, i, 0)),
                  pl.BlockSpec((N_SLABS, seq_len, LANES), lambda b, i: (0, b, 0)),
                  pl.BlockSpec((1, seq_len, LANES), lambda b, i: (b, 0, 0)),
                  pl.BlockSpec((N_SLABS, n_kb, LANES, tk), lambda b, i: (0, b, 0, 0))],
        out_specs=pl.BlockSpec((N_SLABS, tq, LANES), lambda b, i: (0, b * n_qt + i, 0)),
        scratch_shapes=_fox_attn_scratch(tq, tk),
        compiler_params=_params(("parallel", "arbitrary")),
        name="fox_attn",
    )(q, bq, k, bk, vt)


def _fox_step_attn_kernel(q_ref, bq_ref, kold_ref, knew_ref, bk_ref, vtold_ref, vtnew_ref, o_ref,
                          *scratch, n_new, n_old, tk):
    def run_blocks(key_block):
        for kb in range(n_old // tk):
            rows = slice(kb * tk, (kb + 1) * tk)
            key_block(kb, False, lambda hp, rows=rows: kold_ref[hp, rows, :],
                      lambda hp, rows=rows: vtold_ref[hp, 0, :, rows])
        key_block(n_old // tk, True, lambda hp: knew_ref[hp], lambda hp: vtnew_ref[hp, 0])

    _fox_attn_core(q_ref, bq_ref, bk_ref, o_ref, scratch, tq=n_new, tk=tk, first_q=n_old,
                   run_blocks=run_blocks)


def _fox_step_attn_call(q, bq, k_old, k_new, bk, vt_old, vt_new, *, n_seq, n_new, n_old, tk):
    assert n_new <= tk and n_old % tk == 0 and n_old % n_new == 0
    return pl.pallas_call(
        functools.partial(_fox_step_attn_kernel, n_new=n_new, n_old=n_old, tk=tk),
        out_shape=jax.ShapeDtypeStruct((N_SLABS, n_seq * n_new, LANES), BF16),
        grid=(n_seq,),
        in_specs=[pl.BlockSpec((N_SLABS, n_new, LANES), lambda b: (0, b, 0)),
                  pl.BlockSpec((1, n_new, LANES), lambda b: (b, n_old // n_new, 0)),
                  pl.BlockSpec((N_SLABS, n_old, LANES), lambda b: (0, b, 0)),
                  pl.BlockSpec((N_SLABS, tk, LANES), lambda b: (0, b, 0)),
                  pl.BlockSpec((1, n_old + tk, LANES), lambda b: (b, 0, 0)),
                  pl.BlockSpec((N_SLABS, 1, LANES, n_old), lambda b: (0, b, 0, 0)),
                  pl.BlockSpec((N_SLABS, 1, LANES, tk), lambda b: (0, b, 0, 0))],
        out_specs=pl.BlockSpec((N_SLABS, n_new, LANES), lambda b: (0, b, 0)),
        scratch_shapes=_fox_attn_scratch(n_new, tk),
        compiler_params=_params(("parallel",)),
        name="fox_step_attn",
    )(q, bq, k_old, k_new, bk, vt_old, vt_new)


def _rg_layer(xs, conv_state, h_state, w_in, gate_w, small, layer):
    x_p, x_s = xs
    o_p, nb_p, h_p = _rg_call(
        x_p, jnp.zeros((BATCH, CONV_WIDTH - 1, LRU_WIDTH), F32), jnp.zeros((BATCH, 1, LRU_WIDTH), F32),
        w_in, gate_w, small, layer, n_seq=BATCH, seq_len=SEQ, tile=RG_TILE)
    o_s, nb_s, h_s = _rg_call(
        x_s, conv_state, h_state.reshape(DEC_BATCH, 1, LRU_WIDTH),
        w_in, gate_w, small, layer, n_seq=DEC_BATCH, seq_len=DEC_SEQ, tile=DEC_SEQ)
    return (o_p, o_s), (nb_p, nb_s, h_p, h_s)


def _swa_layer(xs, k_cache, v_cache, w_qkv, sinks, table):
    kvd = SWA_KV_HEADS * HEAD_DIM
    order = jnp.array(_SWA_HEAD_ORDER)
    w_q = (w_qkv[:, :D_MODEL] * ATTN_SCALE).reshape(D_MODEL, N_HEADS, HEAD_DIM)[:, order]
    w = jnp.concatenate([w_q.reshape(D_MODEL, D_MODEL), w_qkv[:, D_MODEL:]], axis=1).astype(BF16)
    bias, sink = _swa_row_tables(table, sinks)

    q_p, k_p, v_p, kb_p, vb_p = _swa_proj_call(xs[0], w)
    pad = ((0, 0), (WINDOW, 0), (0, 0))
    o_p = _swa_attn_call(q_p.reshape(BATCH, SEQ, D_MODEL),
                         jnp.pad(kb_p.reshape(BATCH, SEQ, kvd), pad),
                         jnp.pad(vb_p.reshape(BATCH, SEQ, kvd), pad),
                         bias, sink, n_seq=BATCH, seq_len=SEQ, tile=SWA_TILE, n_invalid=WINDOW)

    q_s, k_s, v_s, kb_s, vb_s = _swa_proj_call(xs[1], w)
    kc = k_cache.reshape(DEC_BATCH, WINDOW, kvd)
    vc = v_cache.reshape(DEC_BATCH, WINDOW, kvd)
    o_s = _swa_attn_call(q_s.reshape(DEC_BATCH, DEC_SEQ, D_MODEL),
                         jnp.concatenate([kc.astype(BF16), kb_s.reshape(DEC_BATCH, DEC_SEQ, kvd)], axis=1),
                         jnp.concatenate([vc.astype(BF16), vb_s.reshape(DEC_BATCH, DEC_SEQ, kvd)], axis=1),
                         bias, sink, n_seq=DEC_BATCH, seq_len=DEC_SEQ, tile=DEC_SEQ, n_invalid=0)

    def tails(new_p, new_s, cache):
        tail_p = new_p.reshape(BATCH, SEQ, kvd)[:, SEQ - WINDOW:]
        tail_s = jnp.concatenate([cache.reshape(DEC_BATCH, WINDOW, kvd)[:, DEC_SEQ:],
                                  new_s.reshape(DEC_BATCH, DEC_SEQ, kvd)], axis=1)
        return (tail_p.reshape(BATCH, WINDOW, SWA_KV_HEADS, HEAD_DIM),
                tail_s.reshape(DEC_BATCH, WINDOW, SWA_KV_HEADS, HEAD_DIM))

    k_tp, k_ts = tails(k_p, k_s, k_cache)
    v_tp, v_ts = tails(v_p, v_s, v_cache)
    return ((o_p.reshape(N_PROMPT, D_MODEL), o_s.reshape(N_SAMPLE, D_MODEL)),
            (k_tp, k_ts, v_tp, v_ts))


def _fox_layer(xs, k_cache, v_cache, logf_cache, w_in, b_f):
    hd = N_HEADS * HEAD_DIM
    def thrice(a):
        rep = jnp.concatenate([a, a, a], axis=-1)
        return jnp.pad(rep, [(0, 0)] * (a.ndim - 1) + [(0, LANES - 3 * N_HEADS)])

    w_q = w_in[:, :hd] * ATTN_SCALE
    w_f = thrice(w_in[:, 3 * hd:])
    w_qkf = jnp.concatenate([w_q, w_in[:, hd:2 * hd], w_f], axis=1).astype(BF16)
    w = jnp.concatenate([w_q, w_in[:, hd:3 * hd], w_f], axis=1).astype(BF16)
    bf = thrice(b_f.astype(F32)).reshape(1, LANES)
    w_vt = w_in[:, 2 * hd:3 * hd].T.astype(BF16)

    q_p, kt_p, vt32_p, kb_p, vt_p, lf_p = _fox_proj_call(xs[0], w_qkf, w_vt, bf, seq_len=SEQ,
                                                         time_minor=True)
    lf_p = lf_p.reshape(BATCH, SEQ, LANES)
    bq_p, bk_p = _fox_bias_call(lf_p)
    o_p = _fox_attn_call(q_p, bq_p, kb_p, bk_p, vt_p, n_seq=BATCH, seq_len=SEQ, tq=FOX_TQ, tk=FOX_TK)

    q_s, k_s, v_s, kb_s, vt_s, lf_s = _fox_proj_call(xs[1], w, w_vt, bf, seq_len=DEC_SEQ,
                                                     time_minor=False)
    lf_s = lf_s.reshape(DEC_BATCH, DEC_SEQ, LANES)
    tail = FOX_TK - DEC_SEQ
    bq_s, bk_s = _fox_bias_call(jnp.pad(jnp.concatenate([thrice(logf_cache.astype(F32)), lf_s], axis=1),
                                        ((0, 0), (0, tail), (0, 0))))
    k_old = k_cache.astype(BF16).reshape(DEC_BATCH, PAST_LEN, N_SLABS, LANES).transpose(2, 0, 1, 3)
    k_old = k_old.reshape(N_SLABS, DEC_BATCH * PAST_LEN, LANES)
    k_new = jnp.pad(kb_s.reshape(N_SLABS, DEC_BATCH, DEC_SEQ, LANES), ((0, 0), (0, 0), (0, tail), (0, 0)))
    k_new = k_new.reshape(N_SLABS, DEC_BATCH * FOX_TK, LANES)
    vt_old = v_cache.astype(BF16).reshape(DEC_BATCH, PAST_LEN, N_SLABS, LANES).transpose(2, 0, 3, 1)
    per_blk = FOX_TK // DEC_SEQ
    vt_new = vt_s.reshape(N_SLABS, N_SAMPLE // FOX_TK, LANES, per_blk, DEC_SEQ)
    vt_new = vt_new.transpose(0, 1, 3, 2, 4).reshape(N_SLABS, DEC_BATCH, LANES, DEC_SEQ)
    vt_new = jnp.pad(vt_new, ((0, 0), (0, 0), (0, 0), (0, tail)))
    o_s = _fox_step_attn_call(q_s, bq_s, k_old, k_new, bk_s, vt_old, vt_new, n_seq=DEC_BATCH,
                              n_new=DEC_SEQ, n_old=PAST_LEN, tk=FOX_TK)

    def heads_last(a):
        return a.reshape(BATCH, N_HEADS, HEAD_DIM, SEQ).transpose(0, 3, 1, 2)

    outs = (heads_last(kt_p), k_s.reshape(DEC_BATCH, DEC_SEQ, N_HEADS, HEAD_DIM),
            heads_last(vt32_p), v_s.reshape(DEC_BATCH, DEC_SEQ, N_HEADS, HEAD_DIM),
            lf_p[..., :N_HEADS], lf_s[..., :N_HEADS])
    return (o_p, o_s), outs


def kernel(x_prompt, x_sample, state_rg_conv, state_rg_h, cache_swa_k, cache_swa_v, cache_fox_k, cache_fox_v, cache_fox_logf, ln_gain, ln_bias, ffn_w_up, ffn_w_down, rg_w_in, rg_conv_w, rg_conv_b, rg_gate_w, rg_gate_b, rg_lambda, rg_w_out, swa_w_qkv, swa_sinks, swa_w_out, rel_bias_table, fox_w_in, fox_b_f, fox_w_out):
    xs = (x_prompt.reshape(N_PROMPT, D_MODEL), x_sample.reshape(N_SAMPLE, D_MODEL))
    w_up = ffn_w_up.astype(BF16)
    w_down = ffn_w_down.astype(BF16)
    rg_in = rg_w_in.astype(BF16)
    rg_gate = rg_gate_w.astype(BF16)
    swa_order = jnp.array(_SWA_HEAD_ORDER)
    n_swa = swa_w_out.shape[0]
    w_out = {0: rg_w_out.astype(BF16),
             1: swa_w_out.reshape(n_swa, N_HEADS, HEAD_DIM, D_MODEL)[:, swa_order]
                         .reshape(n_swa, D_MODEL, D_MODEL).astype(BF16),
             2: fox_w_out.astype(BF16)}
    rg_out, swa_out, fox_out = [], [], []
    for i in range(DEPTH):
        kind, j = i % 3, i // 3
        if kind == 0:
            small = (rg_conv_w[j], rg_conv_b[j].reshape(1, LRU_WIDTH),
                     rg_gate_b[j].reshape(LRU_BLOCKS, 1, 2 * LRU_BLOCK_W), rg_lambda[j].reshape(1, LRU_WIDTH))
            os_, extra = _rg_layer(xs, state_rg_conv[j], state_rg_h[j], rg_in, rg_gate, small, j)
            rg_out.append(extra)
        elif kind == 1:
            os_, extra = _swa_layer(xs, cache_swa_k[j], cache_swa_v[j], swa_w_qkv[j], swa_sinks[j],
                                    rel_bias_table)
            swa_out.append(extra)
        else:
            os_, extra = _fox_layer(xs, cache_fox_k[j], cache_fox_v[j], cache_fox_logf[j], fox_w_in[j],
                                    fox_b_f[j])
            fox_out.append(extra)
        xs = tuple(_post_call(x, o, w_out[kind], j, w_up, w_down, ln_gain, ln_bias, i)
                   for x, o in zip(xs, os_))

    def stack(items, idx):
        return jnp.stack([it[idx] for it in items])

    return (xs[0].reshape(BATCH, SEQ, D_MODEL), xs[1].reshape(DEC_BATCH, DEC_SEQ, D_MODEL),
            stack(rg_out, 0), stack(rg_out, 1), stack(rg_out, 2), stack(rg_out, 3),
            stack(swa_out, 0), stack(swa_out, 1), stack(swa_out, 2), stack(swa_out, 3),
            stack(fox_out, 0), stack(fox_out, 1), stack(fox_out, 2), stack(fox_out, 3),
            stack(fox_out, 4), stack(fox_out, 5))
```

```python
import functools
import math

import jax
import jax.numpy as jnp
from jax import lax
from jax.experimental import pallas as pl
from jax.experimental.pallas import tpu as pltpu

F32 = jnp.float32
BF16 = jnp.bfloat16

D_MODEL = 1024
BATCH = 4
SEQ = 4096
DEPTH = 4
DEC_BATCH = 16
DEC_SEQ = 64
PAST_LEN = 1024
CHUNK = 64
D_FF = 4 * D_MODEL
HEAD_DIM = 64
N_HEADS = 16
SWA_KV_HEADS = 4
SWA_GROUP = 4
WINDOW = 128
LRU_WIDTH = D_MODEL
LRU_BLOCKS = 4
LRU_BLOCK_W = LRU_WIDTH // LRU_BLOCKS
CONV_WIDTH = 4
LRU_C = 8.0
N_BUCKETS = 32
MAX_DISTANCE = 128
ALPHA = (2.0 * DEPTH) ** 0.25
LN_EPS = 1e-5
ATTN_SCALE = HEAD_DIM ** -0.5

N_PROMPT = BATCH * SEQ
N_SAMPLE = DEC_BATCH * DEC_SEQ

LANES = 128
HEADS_PER_SLAB = LANES // HEAD_DIM
N_SLABS = D_MODEL // LANES

TOKEN_TILE = 512
POST_TILE = 512
POST_ROWS = 256
FF_CHUNK = 1024
RG_TILE = 512
RG_PROJ_CHUNKS = 8
SWA_TILE = 512
SWA_KEYS = WINDOW + CHUNK
FOX_TQ = 512
FOX_TK = 512
FOX_ONES_ROWS = 16
BIAS_TILE = 512
VMEM_LIMIT = 56 * 1024 * 1024


def _resident(shape, layer=None):
    zeros = (0,) * len(shape)
    if layer is None:
        return pl.BlockSpec(shape, lambda *_: zeros, pipeline_mode=pl.Buffered(1))
    return pl.BlockSpec((None,) + tuple(shape), lambda *_: (layer,) + zeros,
                        pipeline_mode=pl.Buffered(1))


def _params(semantics):
    return pltpu.CompilerParams(dimension_semantics=semantics, vmem_limit_bytes=VMEM_LIMIT)


def _softplus(x):
    return jnp.maximum(x, 0.0) + jnp.log1p(jnp.exp(-jnp.abs(x)))


def _layer_norm(z, g, b):
    mu = jnp.mean(z, axis=-1, keepdims=True)
    zc = z - mu
    var = jnp.mean(zc * zc, axis=-1, keepdims=True)
    return zc * lax.rsqrt(var + LN_EPS) * g + b


def _post_kernel(x_ref, o_ref, wout_ref, wup_ref, wdn_ref, g_ref, b_ref, out_ref, *, slab_major):
    groups = [slice(r * POST_ROWS, (r + 1) * POST_ROWS) for r in range(POST_TILE // POST_ROWS)]
    y = []
    for rows in groups:
        if slab_major:
            o = jnp.concatenate([o_ref[s, rows, :] for s in range(N_SLABS)], axis=1)
        else:
            o = o_ref[rows, :]
        y.append(jnp.dot(o, wout_ref[...], preferred_element_type=F32))
    x1 = [_layer_norm(ALPHA * x_ref[rows, :] + y_r, g_ref[0:1, :], b_ref[0:1, :])
          for rows, y_r in zip(groups, y)]
    for rows, x1_r in zip(groups, x1):
        x1b = x1_r.astype(BF16)
        acc = jnp.zeros_like(x1_r)
        for c in range(D_FF // FF_CHUNK):
            cols = slice(c * FF_CHUNK, (c + 1) * FF_CHUNK)
            h = jnp.dot(x1b, wup_ref[:, cols], preferred_element_type=F32)
            a = jnp.square(jnp.maximum(h, 0.0)).astype(BF16)
            acc = acc + jnp.dot(a, wdn_ref[cols, :], preferred_element_type=F32)
        out_ref[rows, :] = _layer_norm(ALPHA * x1_r + acc, g_ref[1:2, :], b_ref[1:2, :])


def _post_call(x, o, w_out, mixer_idx, w_up, w_down, gain, bias, layer):
    n = x.shape[0]
    tile = pl.BlockSpec((POST_TILE, D_MODEL), lambda i: (i, 0))
    slab_major = o.ndim == 3
    o_tile = pl.BlockSpec((N_SLABS, POST_TILE, LANES), lambda i: (0, i, 0)) if slab_major else tile
    return pl.pallas_call(
        functools.partial(_post_kernel, slab_major=slab_major),
        out_shape=jax.ShapeDtypeStruct((n, D_MODEL), F32),
        grid=(n // POST_TILE,),
        in_specs=[tile, o_tile, _resident((D_MODEL, D_MODEL), mixer_idx),
                  _resident((D_MODEL, D_FF), layer), _resident((D_FF, D_MODEL), layer),
                  _resident((2, D_MODEL), layer), _resident((2, D_MODEL), layer)],
        out_specs=tile,
        compiler_params=_params(("parallel",)),
        name="post",
    )(x, o, w_out, w_up, w_down, gain, bias)


def _run(*staged, order=None):
    live = {i: iter(s) for i, s in enumerate(staged)}
    plan = list(order or [])
    while live:
        turn = plan.pop(0) if plan else None
        for i in ([turn] if turn in live else list(live)):
            try:
                next(live[i])
            except StopIteration:
                del live[i]


def _rg_project(x_ref, perm_ref, win_ref, gu_ref):
    xb = x_ref[...].astype(BF16)
    xb = jnp.dot(perm_ref[...], xb, preferred_element_type=F32).astype(BF16)
    yield
    width = 2 * LRU_WIDTH // RG_PROJ_CHUNKS
    for c in range(RG_PROJ_CHUNKS):
        cols = slice(c * width, (c + 1) * width)
        gu_ref[:, cols] = jnp.dot(xb, win_ref[:, cols], preferred_element_type=F32)
        yield


def _rg_mix(gu_ref, unperm_ref, cw_ref, cb_ref, gw_ref, gb_ref, lam_ref, o_ref, nbuf_ref, rows,
            scratch, *, tile):
    u_scr, gg_scr, h_scr, ac_scr, hist_scr, hc_scr = scratch
    hist = CONV_WIDTH - 1
    seg = tile // 8
    lead = 8 * hist

    u = gu_ref[:, LRU_WIDTH:]
    u_scr[lead:lead + tile, :] = u
    gg_scr[...] = jax.nn.gelu(gu_ref[:, :LRU_WIDTH])
    yield
    sub = lax.broadcasted_iota(jnp.int32, (8, LRU_WIDTH), 0)
    for d in range(1, hist + 1):
        prev_tail = pltpu.roll(u_scr[lead + 8 * (seg - d):lead + 8 * (seg - d) + 8, :], 1, 0)
        u_scr[lead - 8 * d:lead - 8 * d + 8, :] = jnp.where(sub == 0, hist_scr[8 - d:8 - d + 1, :],
                                                           prev_tail)
    conv = cb_ref[...] + u * cw_ref[hist:hist + 1, :]
    for k in range(hist):
        conv = conv + u_scr[8 * k:8 * k + tile, :] * cw_ref[k:k + 1, :]
    for d in range(1, hist + 1):
        last = u_scr[lead + 8 * (seg - d) + 7:lead + 8 * (seg - d) + 8, :]
        hist_scr[8 - d:8 - d + 1, :] = last
        nbuf_ref[0, hist - d:hist - d + 1, :] = last
    yield

    sub_blk = lax.broadcasted_iota(jnp.int32, (8, LRU_BLOCK_W), 0)
    for n in range(LRU_BLOCKS):
        cols = slice(n * LRU_BLOCK_W, (n + 1) * LRU_BLOCK_W)
        cn = conv[:, cols]
        g = jnp.dot(cn.astype(BF16), gw_ref[n], preferred_element_type=F32) + gb_ref[n]
        r = jax.nn.sigmoid(g[:, :LRU_BLOCK_W])
        ig = jax.nn.sigmoid(g[:, LRU_BLOCK_W:])
        log_a = -LRU_C * r * _softplus(-lam_ref[:, cols])
        a_blk = jnp.exp(log_a)
        var = 1.0 - a_blk * a_blk
        root = jnp.where(var > 0.0, var * lax.rsqrt(var), 0.0)
        b_blk = root * (ig * cn)
        h = jnp.zeros((8, LRU_BLOCK_W), F32)
        prod = jnp.ones((8, LRU_BLOCK_W), F32)
        for j in range(seg):
            a = a_blk[8 * j:8 * j + 8, :]
            h = a * h + b_blk[8 * j:8 * j + 8, :]
            prod = a * prod
            h_scr[8 * j:8 * j + 8, cols] = h
            ac_scr[8 * j:8 * j + 8, cols] = prod
        carry = jnp.where(sub_blk == 0, hc_scr[:, cols], 0.0)
        for i in range(1, 8):
            carry = jnp.where(sub_blk == i, pltpu.roll(prod * carry + h, 1, 0), carry)
        hc_scr[:, cols] = (prod * carry + h)[7:8, :]
        for j in range(seg):
            h_scr[8 * j:8 * j + 8, cols] = (h_scr[8 * j:8 * j + 8, cols]
                                            + ac_scr[8 * j:8 * j + 8, cols] * carry)
        yield

    o = (h_scr[...] * gg_scr[...]).astype(BF16)
    o_ref[rows, :] = jnp.dot(unperm_ref[...], o, preferred_element_type=F32).astype(BF16)


def _rg_kernel(*refs, tile, pipelined):
    n_x = 3 if pipelined else 1
    x_refs = refs[:n_x]
    (cbuf_ref, h0_ref, perm_ref, unperm_ref, win_ref, cw_ref, cb_ref, gw_ref, gb_ref, lam_ref,
     o_ref, nbuf_ref, hlast_ref, gu_a, gu_b) = refs[n_x:n_x + 15]
    scratch = refs[n_x + 15:]
    hist_scr, hc_scr = scratch[-2:]
    hist = CONV_WIDTH - 1

    @pl.when(pl.program_id(1) == 0)
    def _():
        hist_scr[8 - hist:8, :] = cbuf_ref[0]
        hc_scr[...] = h0_ref[0]
        _run(_rg_project(x_refs[0], perm_ref, win_ref, gu_a))

    mix = functools.partial(_rg_mix, unperm_ref=unperm_ref, cw_ref=cw_ref, cb_ref=cb_ref, gw_ref=gw_ref,
                            gb_ref=gb_ref, lam_ref=lam_ref, o_ref=o_ref, nbuf_ref=nbuf_ref,
                            scratch=scratch, tile=tile)
    if pipelined:
        order = [1, 0, 1, 1, 0, 0, 1, 0, 0, 1, 0, 0, 1, 0, 0, 1]
        _run(_rg_project(x_refs[1], perm_ref, win_ref, gu_b), mix(gu_a, rows=slice(0, tile)),
             order=order)
        _run(_rg_project(x_refs[2], perm_ref, win_ref, gu_a), mix(gu_b, rows=slice(tile, 2 * tile)),
             order=order)
    else:
        _run(mix(gu_a, rows=slice(0, tile)))
    hlast_ref[0] = hc_scr[...]


def _segment_interleave(tile):
    seg = tile // 8
    src = [(p % 8) * seg + p // 8 for p in range(tile)]
    return jnp.zeros((tile, tile), BF16).at[jnp.arange(tile), jnp.array(src)].set(1.0)


def _rg_call(x, conv_buf, h0, w_in, gate_w, small, layer, *, n_seq, seq_len, tile):
    conv_w, conv_b, gate_b, lam = small
    n_t = seq_len // tile
    pipelined = n_t >= 2 and n_t % 2 == 0
    tiles_per_step = 2 if pipelined else 1
    n_steps = n_t // tiles_per_step
    perm = _segment_interleave(tile)
    if pipelined:
        x_specs = [pl.BlockSpec((tile, D_MODEL), lambda b, s: (b * n_t, 0)),
                   pl.BlockSpec((tile, D_MODEL), lambda b, s: (b * n_t + 2 * s + 1, 0)),
                   pl.BlockSpec((tile, D_MODEL),
                                lambda b, s: (b * n_t + jnp.minimum(2 * s + 2, n_t - 1), 0))]
    else:
        x_specs = [pl.BlockSpec((tile, D_MODEL), lambda b, s: (b * n_t + s, 0))]
    kern = functools.partial(_rg_kernel, tile=tile, pipelined=pipelined)
    o, nbuf, hlast = pl.pallas_call(
        kern,
        out_shape=(jax.ShapeDtypeStruct((n_seq * seq_len, D_MODEL), BF16),
                   jax.ShapeDtypeStruct((n_seq, CONV_WIDTH - 1, LRU_WIDTH), F32),
                   jax.ShapeDtypeStruct((n_seq, 1, LRU_WIDTH), F32)),
        grid=(n_seq, n_steps),
        in_specs=x_specs + [
            pl.BlockSpec((1, CONV_WIDTH - 1, LRU_WIDTH), lambda b, s: (b, 0, 0)),
            pl.BlockSpec((1, 1, LRU_WIDTH), lambda b, s: (b, 0, 0)),
            _resident((tile, tile)), _resident((tile, tile)),
            _resident((D_MODEL, 2 * LRU_WIDTH), layer),
            _resident((CONV_WIDTH, LRU_WIDTH)),
            _resident((1, LRU_WIDTH)),
            _resident((LRU_BLOCKS, LRU_BLOCK_W, 2 * LRU_BLOCK_W), layer),
            _resident((LRU_BLOCKS, 1, 2 * LRU_BLOCK_W)),
            _resident((1, LRU_WIDTH))],
        out_specs=(pl.BlockSpec((tiles_per_step * tile, D_MODEL), lambda b, s: (b * n_steps + s, 0)),
                   pl.BlockSpec((1, CONV_WIDTH - 1, LRU_WIDTH), lambda b, s: (b, 0, 0)),
                   pl.BlockSpec((1, 1, LRU_WIDTH), lambda b, s: (b, 0, 0))),
        scratch_shapes=[pltpu.VMEM((tile, 2 * LRU_WIDTH), F32),
                        pltpu.VMEM((tile, 2 * LRU_WIDTH), F32),
                        pltpu.VMEM((tile + 8 * (CONV_WIDTH - 1), LRU_WIDTH), F32),
                        pltpu.VMEM((tile, LRU_WIDTH), F32),
                        pltpu.VMEM((tile, LRU_WIDTH), F32),
                        pltpu.VMEM((tile, LRU_WIDTH), F32),
                        pltpu.VMEM((8, LRU_WIDTH), F32),
                        pltpu.VMEM((1, LRU_WIDTH), F32)],
        compiler_params=_params(("parallel", "arbitrary")),
        name="rg_mixer",
    )(*([x] * len(x_specs)), conv_buf, h0, perm, perm.T, w_in, conv_w, conv_b, gate_w, gate_b, lam)
    return o, nbuf, hlast[:, 0, :]


def _swa_proj_kernel(x_ref, w_ref, q_ref, k_ref, v_ref, kb_ref, vb_ref):
    xb = x_ref[...].astype(BF16)
    qkv = jnp.dot(xb, w_ref[...], preferred_element_type=F32)
    kvd = SWA_KV_HEADS * HEAD_DIM
    q_ref[...] = qkv[:, :D_MODEL].astype(BF16)
    k = qkv[:, D_MODEL:D_MODEL + kvd]
    v = qkv[:, D_MODEL + kvd:]
    k_ref[...] = k
    v_ref[...] = v
    kb_ref[...] = k.astype(BF16)
    vb_ref[...] = v.astype(BF16)


def _swa_proj_call(x, w):
    n = x.shape[0]
    kvd = SWA_KV_HEADS * HEAD_DIM
    row = lambda i: (i, 0)
    return pl.pallas_call(
        _swa_proj_kernel,
        out_shape=(jax.ShapeDtypeStruct((n, D_MODEL), BF16),
                   jax.ShapeDtypeStruct((n, kvd), F32),
                   jax.ShapeDtypeStruct((n, kvd), F32),
                   jax.ShapeDtypeStruct((n, kvd), BF16),
                   jax.ShapeDtypeStruct((n, kvd), BF16)),
        grid=(n // TOKEN_TILE,),
        in_specs=[pl.BlockSpec((TOKEN_TILE, D_MODEL), row), _resident((D_MODEL, D_MODEL + 2 * kvd))],
        out_specs=(pl.BlockSpec((TOKEN_TILE, D_MODEL), row),
                   pl.BlockSpec((TOKEN_TILE, kvd), row), pl.BlockSpec((TOKEN_TILE, kvd), row),
                   pl.BlockSpec((TOKEN_TILE, kvd), row), pl.BlockSpec((TOKEN_TILE, kvd), row)),
        compiler_params=_params(("parallel",)),
        name="swa_proj",
    )(x, w)


def _swa_attn_kernel(q_ref, k_ref, v_ref, bias_ref, sink_ref, o_ref, s_even, s_odd,
                     *, tile, n_invalid):
    ti = pl.program_id(1)
    cols = SWA_GROUP * HEADS_PER_SLAB * CHUNK
    half_cols = cols // HEADS_PER_SLAB
    lane_q = lax.broadcasted_iota(jnp.int32, (CHUNK, LANES), 1)
    low_q = lane_q < HEAD_DIM
    key_idx = lax.broadcasted_iota(jnp.int32, (SWA_KEYS, cols), 0)
    n_kv_slabs = SWA_KV_HEADS // HEADS_PER_SLAB
    units = [(c, t) for c in range(tile // CHUNK) for t in range(n_kv_slabs)]
    s_bufs = (s_even, s_odd)

    def window_start(c):
        return pl.multiple_of(ti * tile + c * CHUNK, CHUNK)

    def scores(u):
        c, t = units[u]
        row0 = window_start(c)
        kw = k_ref[0, pl.ds(row0, SWA_KEYS), t * LANES:(t + 1) * LANES]
        slabs = [q_ref[0, c * CHUNK:(c + 1) * CHUNK,
                       (SWA_GROUP * t + g) * LANES:(SWA_GROUP * t + g + 1) * LANES]
                 for g in range(SWA_GROUP)]
        zero = jnp.zeros_like(slabs[0])
        q_stack = jnp.concatenate([jnp.where(low_q, s_, zero) for s_ in slabs]
                                  + [jnp.where(low_q, zero, s_) for s_ in slabs], axis=0)
        s = lax.dot_general(kw, q_stack, (((1,), (1,)), ((), ())), preferred_element_type=F32)
        s = s + bias_ref[t]
        if n_invalid > 0 and c * CHUNK < n_invalid:
            s = jnp.where(key_idx + row0 >= n_invalid, s, -jnp.inf)
        s_bufs[u % 2][...] = s

    def attend(u):
        c, t = units[u]
        s = s_bufs[u % 2][...]
        vw = v_ref[0, pl.ds(window_start(c), SWA_KEYS), t * LANES:(t + 1) * LANES]
        sink = sink_ref[t]
        m = jnp.maximum(jnp.max(s, axis=0, keepdims=True), sink)
        p = jnp.exp(s - m)
        denom = jnp.sum(p, axis=0, keepdims=True) + jnp.exp(sink - m)
        out_t = lax.dot_general(vw, p.astype(BF16), (((0,), (0,)), ((), ())),
                                preferred_element_type=F32) / denom
        pair = jnp.concatenate([out_t[:HEAD_DIM, :half_cols], out_t[HEAD_DIM:, half_cols:]], axis=0)
        pair = pair.T.astype(BF16)
        for g in range(SWA_GROUP):
            s_idx = SWA_GROUP * t + g
            o_ref[0, c * CHUNK:(c + 1) * CHUNK, s_idx * LANES:(s_idx + 1) * LANES] = (
                pair[g * CHUNK:(g + 1) * CHUNK, :])

    scores(0)
    for u in range(len(units)):
        if u + 1 < len(units):
            scores(u + 1)
        attend(u)


def _swa_attn_call(q, k_win, v_win, bias, sink, *, n_seq, seq_len, tile, n_invalid):
    kvd = SWA_KV_HEADS * HEAD_DIM
    n_keys = k_win.shape[1]
    cols = SWA_GROUP * HEADS_PER_SLAB * CHUNK
    n_kv_slabs = SWA_KV_HEADS // HEADS_PER_SLAB
    kern = functools.partial(_swa_attn_kernel, tile=tile, n_invalid=n_invalid)
    return pl.pallas_call(
        kern,
        out_shape=jax.ShapeDtypeStruct((n_seq, seq_len, D_MODEL), BF16),
        grid=(n_seq, seq_len // tile),
        in_specs=[pl.BlockSpec((1, tile, D_MODEL), lambda b, t: (b, t, 0)),
                  pl.BlockSpec((1, n_keys, kvd), lambda b, t: (b, 0, 0)),
                  pl.BlockSpec((1, n_keys, kvd), lambda b, t: (b, 0, 0)),
                  _resident((n_kv_slabs, SWA_KEYS, cols)),
                  _resident((n_kv_slabs, 1, cols))],
        out_specs=pl.BlockSpec((1, tile, D_MODEL), lambda b, t: (b, t, 0)),
        scratch_shapes=[pltpu.VMEM((SWA_KEYS, cols), F32), pltpu.VMEM((SWA_KEYS, cols), F32)],
        compiler_params=_params(("parallel", "arbitrary")),
        name="swa_attn",
    )(q, k_win, v_win, bias, sink)


def _t5_bucket(rel):
    half = N_BUCKETS // 2
    max_exact = half // 2
    n = jnp.abs(rel)
    n_f = jnp.maximum(n, 1).astype(jnp.float32)
    large = max_exact + (jnp.log(n_f / max_exact) / math.log(MAX_DISTANCE / max_exact)
                         * (half - max_exact)).astype(jnp.int32)
    large = jnp.minimum(large, half - 1)
    return jnp.where(rel > 0, half, 0) + jnp.where(n < max_exact, n, large)


_SWA_HEAD_ORDER = [8 * t + 4 * p + g for t in range(2) for g in range(4) for p in range(2)]


def _swa_row_tables(table, sinks):
    rel = jnp.arange(SWA_KEYS)[None, :] - WINDOW - jnp.arange(CHUNK)[:, None]
    bucket = _t5_bucket(rel)
    tbl = table.astype(F32)
    hit = bucket[None, None] == jnp.arange(N_BUCKETS)[:, None, None, None]
    bias = jnp.sum(jnp.where(hit, tbl[:, :, None, None], 0.0), axis=0)
    n_kv_slabs = SWA_KV_HEADS // HEADS_PER_SLAB
    cols = SWA_GROUP * HEADS_PER_SLAB * CHUNK
    bias = bias.reshape(n_kv_slabs, cols, SWA_KEYS).transpose(0, 2, 1)
    sink = jnp.repeat(sinks.astype(F32), CHUNK).reshape(n_kv_slabs, 1, cols)
    return bias, sink


def _fox_proj_kernel(x_ref, w_ref, wvt_ref, bf_ref, q_ref, k_ref, v_ref, kb_ref, vt_ref, lf_ref,
                     *, time_minor):
    xb = x_ref[...].astype(BF16)
    proj = jnp.dot(xb, w_ref[...], preferred_element_type=F32)
    k = proj[:, D_MODEL:2 * D_MODEL]
    vt = lax.dot_general(wvt_ref[...], xb, (((1,), (1,)), ((), ())), preferred_element_type=F32)
    if time_minor:
        k_ref[0] = k.T
        v_ref[0] = vt
        z = proj[:, 2 * D_MODEL:]
    else:
        k_ref[...] = k.reshape(TOKEN_TILE, N_HEADS, HEAD_DIM)
        v_ref[...] = proj[:, 2 * D_MODEL:3 * D_MODEL].reshape(TOKEN_TILE, N_HEADS, HEAD_DIM)
        z = proj[:, 3 * D_MODEL:]
    for s in range(N_SLABS):
        cols = slice(s * LANES, (s + 1) * LANES)
        q_ref[s] = proj[:, cols].astype(BF16)
        kb_ref[s] = k[:, cols].astype(BF16)
        for j in range(TOKEN_TILE // FOX_TK):
            vt_ref[s, j] = vt[cols, j * FOX_TK:(j + 1) * FOX_TK].astype(BF16)
    lf_ref[...] = -_softplus(-(z + bf_ref[...]))


def _fox_proj_call(x, w, w_vt, b_f, *, seq_len, time_minor):
    n = x.shape[0]
    row = lambda i: (i, 0)
    wide = pl.BlockSpec((TOKEN_TILE, D_MODEL), row)
    slabs = pl.BlockSpec((N_SLABS, TOKEN_TILE, LANES), lambda i: (0, i, 0))
    kb_per_tile = TOKEN_TILE // FOX_TK
    if time_minor:
        t_per_seq = seq_len // TOKEN_TILE
        kv_shape = jax.ShapeDtypeStruct((n // seq_len, D_MODEL, seq_len), F32)
        kv_spec = pl.BlockSpec((1, D_MODEL, TOKEN_TILE), lambda i: (i // t_per_seq, 0, i % t_per_seq))
    else:
        kv_shape = jax.ShapeDtypeStruct((n, N_HEADS, HEAD_DIM), F32)
        kv_spec = pl.BlockSpec((TOKEN_TILE, N_HEADS, HEAD_DIM), lambda i: (i, 0, 0))
    return pl.pallas_call(
        functools.partial(_fox_proj_kernel, time_minor=time_minor),
        out_shape=(jax.ShapeDtypeStruct((N_SLABS, n, LANES), BF16), kv_shape, kv_shape,
                   jax.ShapeDtypeStruct((N_SLABS, n, LANES), BF16),
                   jax.ShapeDtypeStruct((N_SLABS, n // FOX_TK, LANES, FOX_TK), BF16),
                   jax.ShapeDtypeStruct((n, LANES), F32)),
        grid=(n // TOKEN_TILE,),
        in_specs=[wide, _resident(w.shape), _resident((D_MODEL, D_MODEL)), _resident((1, LANES))],
        out_specs=(slabs, kv_spec, kv_spec, slabs,
                   pl.BlockSpec((N_SLABS, kb_per_tile, LANES, FOX_TK), lambda i: (0, i, 0, 0)),
                   pl.BlockSpec((TOKEN_TILE, LANES), row)),
        compiler_params=_params(("parallel",)),
        name="fox_proj",
    )(x, w, w_vt, b_f)


def _lane_split3(x, lane):
    hi = x.astype(BF16)
    r1 = x - hi.astype(F32)
    mid = r1.astype(BF16)
    lo = (r1 - mid.astype(F32)).astype(BF16)
    return jnp.where(lane < N_HEADS, hi, jnp.where(lane < 2 * N_HEADS, mid, lo))


def _fox_bias_kernel(lf_ref, tri_ref, place_ref, const_ref, bq_ref, bk_ref, carry_scr):
    t = pl.program_id(1)

    @pl.when(t == 0)
    def _():
        carry_scr[...] = jnp.zeros_like(carry_scr)

    lane = lax.broadcasted_iota(jnp.int32, (BIAS_TILE, LANES), 1)
    sums = carry_scr[...] + jnp.dot(tri_ref[...], _lane_split3(lf_ref[0], lane),
                                    preferred_element_type=F32)
    carry_scr[...] = sums[BIAS_TILE - 1:BIAS_TILE, :]
    c = sums + pltpu.roll(sums, LANES - N_HEADS, 1) + pltpu.roll(sums, LANES - 2 * N_HEADS, 1)
    c = jnp.where(lane < N_HEADS, c, 0.0)
    c = c + pltpu.roll(c, N_HEADS, 1) + pltpu.roll(c, 2 * N_HEADS, 1)
    both = const_ref[...] + jnp.dot(_lane_split3(c, lane), place_ref[...],
                                    preferred_element_type=F32)
    bq_ref[0] = both[:, :LANES].astype(BF16)
    bk_ref[0] = both[:, LANES:].astype(BF16)


def _fox_bias_tables():
    place = [[0.0] * (2 * LANES) for _ in range(LANES)]
    const = [0.0] * (2 * LANES)
    for i in range(3):
        for h in range(N_HEADS):
            const[N_HEADS * i + h] = 1.0
            place[N_HEADS * i + h][N_HEADS * (3 + i) + h] = 1.0
            place[N_HEADS * i + h][LANES + N_HEADS * i + h] = -1.0
            const[LANES + N_HEADS * (3 + i) + h] = 1.0
    return jnp.array(place, BF16), jnp.array([const], F32)


def _fox_bias_call(logf):
    n_seq, n_rows, _ = logf.shape
    place, const = _fox_bias_tables()
    tri = jnp.tril(jnp.ones((BIAS_TILE, BIAS_TILE), BF16))
    blk = pl.BlockSpec((1, BIAS_TILE, LANES), lambda b, t: (b, t, 0))
    return pl.pallas_call(
        _fox_bias_kernel,
        out_shape=(jax.ShapeDtypeStruct((n_seq, n_rows, LANES), BF16),
                   jax.ShapeDtypeStruct((n_seq, n_rows, LANES), BF16)),
        grid=(n_seq, n_rows // BIAS_TILE),
        in_specs=[blk, _resident((BIAS_TILE, BIAS_TILE)), _resident((LANES, 2 * LANES)),
                  _resident((1, 2 * LANES))],
        out_specs=(blk, blk),
        scratch_shapes=[pltpu.VMEM((1, LANES), F32)],
        compiler_params=_params(("parallel", "arbitrary")),
        name="fox_bias",
    )(logf, tri, place, const)


def _fox_attn_core(q_ref, bq_ref, bk_ref, o_ref, scratch, *, tq, tk, first_q, run_blocks):
    qaug_scr, m_scr, acc_scr, s_even, s_odd, smax_even, smax_odd = scratch
    lane = lax.broadcasted_iota(jnp.int32, (tq, LANES), 1)
    low_lane = lane < HEAD_DIM
    bias_lane = lane < 6 * N_HEADS
    k_off = lax.broadcasted_iota(jnp.int32, (tk, tq), 0)
    q_pos = first_q + lax.broadcasted_iota(jnp.int32, (tk, tq), 1)
    bq = bq_ref[0]
    zero = jnp.zeros_like(bq)
    ones_rows = jnp.ones((FOX_ONES_ROWS, tk), BF16)

    for head in range(N_HEADS):
        q_slab = q_ref[head // HEADS_PER_SLAB]
        own_half = low_lane if head % HEADS_PER_SLAB == 0 else jnp.logical_not(low_lane)
        sel = jnp.logical_and((lane & (N_HEADS - 1)) == head, bias_lane)
        qaug_scr[head] = jnp.concatenate([jnp.where(own_half, q_slab, zero),
                                          jnp.where(sel, bq, zero)], axis=1)
    m_scr[...] = jnp.full_like(m_scr, -jnp.inf)
    acc_scr[...] = jnp.zeros_like(acc_scr)
    slots = ((s_even, smax_even), (s_odd, smax_odd))

    def key_block(kb, masked, k_of, vt_of):
        bk_blk = bk_ref[0, pl.ds(pl.multiple_of(kb * tk, tk), tk), :]

        def scores(hp):
            s_buf, smax_buf = slots[hp % 2]
            k_aug = jnp.concatenate([k_of(hp), bk_blk], axis=1)
            for par in range(HEADS_PER_SLAB):
                s = lax.dot_general(k_aug, qaug_scr[hp * HEADS_PER_SLAB + par],
                                    (((1,), (1,)), ((), ())), preferred_element_type=F32)
                if masked:
                    s = jnp.where(kb * tk + k_off <= q_pos, s, -jnp.inf)
                s_buf[par] = s
                smax_buf[par] = jnp.max(s, axis=0, keepdims=True)

        def absorb(hp):
            s_buf, smax_buf = slots[hp % 2]
            vt = vt_of(hp)
            for par in range(HEADS_PER_SLAB):
                head = hp * HEADS_PER_SLAB + par
                m = m_scr[head]
                m_new = jnp.maximum(m, smax_buf[par])
                p = jnp.exp(s_buf[par] - m_new).astype(BF16)
                vt_aug = jnp.concatenate([vt[par * HEAD_DIM:(par + 1) * HEAD_DIM, :], ones_rows], axis=0)
                acc_scr[head] = (jnp.exp(m - m_new) * acc_scr[head]
                                 + jnp.dot(vt_aug, p, preferred_element_type=F32))
                m_scr[head] = m_new

        scores(0)
        for hp in range(N_SLABS):
            if hp + 1 < N_SLABS:
                scores(hp + 1)
            absorb(hp)

    run_blocks(key_block)

    for hp in range(N_SLABS):
        halves = []
        for par in range(HEADS_PER_SLAB):
            acc = acc_scr[hp * HEADS_PER_SLAB + par]
            halves.append(acc[:HEAD_DIM] / acc[HEAD_DIM:HEAD_DIM + 1])
        o_ref[hp] = jnp.concatenate(halves, axis=0).T.astype(BF16)


def _fox_attn_scratch(tq, tk):
    return [pltpu.VMEM((N_HEADS, tq, 2 * LANES), BF16),
            pltpu.VMEM((N_HEADS, 1, tq), F32),
            pltpu.VMEM((N_HEADS, HEAD_DIM + FOX_ONES_ROWS, tq), F32),
            pltpu.VMEM((HEADS_PER_SLAB, tk, tq), F32),
            pltpu.VMEM((HEADS_PER_SLAB, tk, tq), F32),
            pltpu.VMEM((HEADS_PER_SLAB, 1, tq), F32),
            pltpu.VMEM((HEADS_PER_SLAB, 1, tq), F32)]


def _fox_attn_kernel(q_ref, bq_ref, k_ref, bk_ref, vt_ref, o_ref, *scratch, tq, tk):
    first_q = pl.program_id(1) * tq
    n_full = first_q // tk

    def run_blocks(key_block):
        def block(kb, masked):
            r0 = pl.multiple_of(kb * tk, tk)
            key_block(kb, masked, lambda hp: k_ref[hp, pl.ds(r0, tk), :], lambda hp: vt_ref[hp, kb])

        def full_block(kb, carry):
            block(kb, False)
            return carry

        lax.fori_loop(0, n_full, full_block, 0)
        block(n_full, True)

    _fox_attn_core(q_ref, bq_ref, bk_ref, o_ref, scratch, tq=tq, tk=tk, first_q=first_q,
                   run_blocks=run_blocks)


def _fox_attn_call(q, bq, k, bk, vt, *, n_seq, seq_len, tq, tk):
    assert tq <= tk and tk % tq == 0 and seq_len % tk == 0
    n_qt = seq_len // tq
    n_kb = seq_len // tk
    return pl.pallas_call(
        functools.partial(_fox_attn_kernel, tq=tq, tk=tk),
        out_shape=jax.ShapeDtypeStruct((N_SLABS, n_seq * seq_len, LANES), BF16),
        grid=(n_seq, n_qt),
        in_specs=[pl.BlockSpec((N_SLABS, tq, LANES), lambda b, i: (0, b * n_qt + i, 0)),
                  pl.BlockSpec((1, tq, LANES), lambda b, i: (b, i, 0)),
                  pl.BlockSpec((N_SLABS, seq_len, LANES), lambda b, i: (0, b, 0)),
                  pl.BlockSpec((1, seq_len, LANES), lambda b, i: (b, 0, 0)),
                  pl.BlockSpec((N_SLABS, n_kb, LANES, tk), lambda b, i: (0, b, 0, 0))],
        out_specs=pl.BlockSpec((N_SLABS, tq, LANES), lambda b, i: (0, b * n_qt + i, 0)),
        scratch_shapes=_fox_attn_scratch(tq, tk),
        compiler_params=_params(("parallel", "arbitrary")),
        name="fox_attn",
    )(q, bq, k, bk, vt)


def _fox_step_attn_kernel(q_ref, bq_ref, kold_ref, knew_ref, bk_ref, vtold_ref, vtnew_ref, o_ref,
                          *scratch, n_new, n_old, tk):
    def run_blocks(key_block):
        for kb in range(n_old // tk):
            rows = slice(kb * tk, (kb + 1) * tk)
            key_block(kb, False, lambda hp, rows=rows: kold_ref[hp, rows, :],
                      lambda hp, rows=rows: vtold_ref[hp, 0, :, rows])
        key_block(n_old // tk, True, lambda hp: knew_ref[hp], lambda hp: vtnew_ref[hp, 0])

    _fox_attn_core(q_ref, bq_ref, bk_ref, o_ref, scratch, tq=n_new, tk=tk, first_q=n_old,
                   run_blocks=run_blocks)


def _fox_step_attn_call(q, bq, k_old, k_new, bk, vt_old, vt_new, *, n_seq, n_new, n_old, tk):
    assert n_new <= tk and n_old % tk == 0 and n_old % n_new == 0
    return pl.pallas_call(
        functools.partial(_fox_step_attn_kernel, n_new=n_new, n_old=n_old, tk=tk),
        out_shape=jax.ShapeDtypeStruct((N_SLABS, n_seq * n_new, LANES), BF16),
        grid=(n_seq,),
        in_specs=[pl.BlockSpec((N_SLABS, n_new, LANES), lambda b: (0, b, 0)),
                  pl.BlockSpec((1, n_new, LANES), lambda b: (b, n_old // n_new, 0)),
                  pl.BlockSpec((N_SLABS, n_old, LANES), lambda b: (0, b, 0)),
                  pl.BlockSpec((N_SLABS, tk, LANES), lambda b: (0, b, 0)),
                  pl.BlockSpec((1, n_old + tk, LANES), lambda b: (b, 0, 0)),
                  pl.BlockSpec((N_SLABS, 1, LANES, n_old), lambda b: (0, b, 0, 0)),
                  pl.BlockSpec((N_SLABS, 1, LANES, tk), lambda b: (0, b, 0, 0))],
        out_specs=pl.BlockSpec((N_SLABS, n_new, LANES), lambda b: (0, b, 0)),
        scratch_shapes=_fox_attn_scratch(n_new, tk),
        compiler_params=_params(("parallel",)),
        name="fox_step_attn",
    )(q, bq, k_old, k_new, bk, vt_old, vt_new)


def _rg_layer(xs, conv_state, h_state, w_in, gate_w, small, layer):
    x_p, x_s = xs
    o_p, nb_p, h_p = _rg_call(
        x_p, jnp.zeros((BATCH, CONV_WIDTH - 1, LRU_WIDTH), F32), jnp.zeros((BATCH, 1, LRU_WIDTH), F32),
        w_in, gate_w, small, layer, n_seq=BATCH, seq_len=SEQ, tile=RG_TILE)
    o_s, nb_s, h_s = _rg_call(
        x_s, conv_state, h_state.reshape(DEC_BATCH, 1, LRU_WIDTH),
        w_in, gate_w, small, layer, n_seq=DEC_BATCH, seq_len=DEC_SEQ, tile=DEC_SEQ)
    return (o_p, o_s), (nb_p, nb_s, h_p, h_s)


def _swa_layer(xs, k_cache, v_cache, w_qkv, sinks, table):
    kvd = SWA_KV_HEADS * HEAD_DIM
    order = jnp.array(_SWA_HEAD_ORDER)
    w_q = (w_qkv[:, :D_MODEL] * ATTN_SCALE).reshape(D_MODEL, N_HEADS, HEAD_DIM)[:, order]
    w = jnp.concatenate([w_q.reshape(D_MODEL, D_MODEL), w_qkv[:, D_MODEL:]], axis=1).astype(BF16)
    bias, sink = _swa_row_tables(table, sinks)

    q_p, k_p, v_p, kb_p, vb_p = _swa_proj_call(xs[0], w)
    pad = ((0, 0), (WINDOW, 0), (0, 0))
    o_p = _swa_attn_call(q_p.reshape(BATCH, SEQ, D_MODEL),
                         jnp.pad(kb_p.reshape(BATCH, SEQ, kvd), pad),
                         jnp.pad(vb_p.reshape(BATCH, SEQ, kvd), pad),
                         bias, sink, n_seq=BATCH, seq_len=SEQ, tile=SWA_TILE, n_invalid=WINDOW)

    q_s, k_s, v_s, kb_s, vb_s = _swa_proj_call(xs[1], w)
    kc = k_cache.reshape(DEC_BATCH, WINDOW, kvd)
    vc = v_cache.reshape(DEC_BATCH, WINDOW, kvd)
    o_s = _swa_attn_call(q_s.reshape(DEC_BATCH, DEC_SEQ, D_MODEL),
                         jnp.concatenate([kc.astype(BF16), kb_s.reshape(DEC_BATCH, DEC_SEQ, kvd)], axis=1),
                         jnp.concatenate([vc.astype(BF16), vb_s.reshape(DEC_BATCH, DEC_SEQ, kvd)], axis=1),
                         bias, sink, n_seq=DEC_BATCH, seq_len=DEC_SEQ, tile=DEC_SEQ, n_invalid=0)

    def tails(new_p, new_s, cache):
        tail_p = new_p.reshape(BATCH, SEQ, kvd)[:, SEQ - WINDOW:]
        tail_s = jnp.concatenate([cache.reshape(DEC_BATCH, WINDOW, kvd)[:, DEC_SEQ:],
                                  new_s.reshape(DEC_BATCH, DEC_SEQ, kvd)], axis=1)
        return (tail_p.reshape(BATCH, WINDOW, SWA_KV_HEADS, HEAD_DIM),
                tail_s.reshape(DEC_BATCH, WINDOW, SWA_KV_HEADS, HEAD_DIM))

    k_tp, k_ts = tails(k_p, k_s, k_cache)
    v_tp, v_ts = tails(v_p, v_s, v_cache)
    return ((o_p.reshape(N_PROMPT, D_MODEL), o_s.reshape(N_SAMPLE, D_MODEL)),
            (k_tp, k_ts, v_tp, v_ts))


def _fox_layer(xs, k_cache, v_cache, logf_cache, w_in, b_f):
    hd = N_HEADS * HEAD_DIM
    def thrice(a):
        rep = jnp.concatenate([a, a, a], axis=-1)
        return jnp.pad(rep, [(0, 0)] * (a.ndim - 1) + [(0, LANES - 3 * N_HEADS)])

    w_q = w_in[:, :hd] * ATTN_SCALE
    w_f = thrice(w_in[:, 3 * hd:])
    w_qkf = jnp.concatenate([w_q, w_in[:, hd:2 * hd], w_f], axis=1).astype(BF16)
    w = jnp.concatenate([w_q, w_in[:, hd:3 * hd], w_f], axis=1).astype(BF16)
    bf = thrice(b_f.astype(F32)).reshape(1, LANES)
    w_vt = w_in[:, 2 * hd:3 * hd].T.astype(BF16)

    q_p, kt_p, vt32_p, kb_p, vt_p, lf_p = _fox_proj_call(xs[0], w_qkf, w_vt, bf, seq_len=SEQ,
                                                         time_minor=True)
    lf_p = lf_p.reshape(BATCH, SEQ, LANES)
    bq_p, bk_p = _fox_bias_call(lf_p)
    o_p = _fox_attn_call(q_p, bq_p, kb_p, bk_p, vt_p, n_seq=BATCH, seq_len=SEQ, tq=FOX_TQ, tk=FOX_TK)

    q_s, k_s, v_s, kb_s, vt_s, lf_s = _fox_proj_call(xs[1], w, w_vt, bf, seq_len=DEC_SEQ,
                                                     time_minor=False)
    lf_s = lf_s.reshape(DEC_BATCH, DEC_SEQ, LANES)
    tail = FOX_TK - DEC_SEQ
    bq_s, bk_s = _fox_bias_call(jnp.pad(jnp.concatenate([thrice(logf_cache.astype(F32)), lf_s], axis=1),
                                        ((0, 0), (0, tail), (0, 0))))
    k_old = k_cache.astype(BF16).reshape(DEC_BATCH, PAST_LEN, N_SLABS, LANES).transpose(2, 0, 1, 3)
    k_old = k_old.reshape(N_SLABS, DEC_BATCH * PAST_LEN, LANES)
    k_new = jnp.pad(kb_s.reshape(N_SLABS, DEC_BATCH, DEC_SEQ, LANES), ((0, 0), (0, 0), (0, tail), (0, 0)))
    k_new = k_new.reshape(N_SLABS, DEC_BATCH * FOX_TK, LANES)
    vt_old = v_cache.astype(BF16).reshape(DEC_BATCH, PAST_LEN, N_SLABS, LANES).transpose(2, 0, 3, 1)
    per_blk = FOX_TK // DEC_SEQ
    vt_new = vt_s.reshape(N_SLABS, N_SAMPLE // FOX_TK, LANES, per_blk, DEC_SEQ)
    vt_new = vt_new.transpose(0, 1, 3, 2, 4).reshape(N_SLABS, DEC_BATCH, LANES, DEC_SEQ)
    vt_new = jnp.pad(vt_new, ((0, 0), (0, 0), (0, 0), (0, tail)))
    o_s = _fox_step_attn_call(q_s, bq_s, k_old, k_new, bk_s, vt_old, vt_new, n_seq=DEC_BATCH,
                              n_new=DEC_SEQ, n_old=PAST_LEN, tk=FOX_TK)

    def heads_last(a):
        return a.reshape(BATCH, N_HEADS, HEAD_DIM, SEQ).transpose(0, 3, 1, 2)

    outs = (heads_last(kt_p), k_s.reshape(DEC_BATCH, DEC_SEQ, N_HEADS, HEAD_DIM),
            heads_last(vt32_p), v_s.reshape(DEC_BATCH, DEC_SEQ, N_HEADS, HEAD_DIM),
            lf_p[..., :N_HEADS], lf_s[..., :N_HEADS])
    return (o_p, o_s), outs


def kernel(x_prompt, x_sample, state_rg_conv, state_rg_h, cache_swa_k, cache_swa_v, cache_fox_k, cache_fox_v, cache_fox_logf, ln_gain, ln_bias, ffn_w_up, ffn_w_down, rg_w_in, rg_conv_w, rg_conv_b, rg_gate_w, rg_gate_b, rg_lambda, rg_w_out, swa_w_qkv, swa_sinks, swa_w_out, rel_bias_table, fox_w_in, fox_b_f, fox_w_out):
    xs = (x_prompt.reshape(N_PROMPT, D_MODEL), x_sample.reshape(N_SAMPLE, D_MODEL))
    w_up = ffn_w_up.astype(BF16)
    w_down = ffn_w_down.astype(BF16)
    rg_in = rg_w_in.astype(BF16)
    rg_gate = rg_gate_w.astype(BF16)
    swa_order = jnp.array(_SWA_HEAD_ORDER)
    n_swa = swa_w_out.shape[0]
    w_out = {0: rg_w_out.astype(BF16),
             1: swa_w_out.reshape(n_swa, N_HEADS, HEAD_DIM, D_MODEL)[:, swa_order]
                         .reshape(n_swa, D_MODEL, D_MODEL).astype(BF16),
             2: fox_w_out.astype(BF16)}
    rg_out, swa_out, fox_out = [], [], []
    for i in range(DEPTH):
        kind, j = i % 3, i // 3
        if kind == 0:
            small = (rg_conv_w[j], rg_conv_b[j].reshape(1, LRU_WIDTH),
                     rg_gate_b[j].reshape(LRU_BLOCKS, 1, 2 * LRU_BLOCK_W), rg_lambda[j].reshape(1, LRU_WIDTH))
            os_, extra = _rg_layer(xs, state_rg_conv[j], state_rg_h[j], rg_in, rg_gate, small, j)
            rg_out.append(extra)
        elif kind == 1:
            os_, extra = _swa_layer(xs, cache_swa_k[j], cache_swa_v[j], swa_w_qkv[j], swa_sinks[j],
                                    rel_bias_table)
            swa_out.append(extra)
        else:
            os_, extra = _fox_layer(xs, cache_fox_k[j], cache_fox_v[j], cache_fox_logf[j], fox_w_in[j],
                                    fox_b_f[j])
            fox_out.append(extra)
        xs = tuple(_post_call(x, o, w_out[kind], j, w_up, w_down, ln_gain, ln_bias, i)
                   for x, o in zip(xs, os_))

    def stack(items, idx):
        return jnp.stack([it[idx] for it in items])

    return (xs[0].reshape(BATCH, SEQ, D_MODEL), xs[1].reshape(DEC_BATCH, DEC_SEQ, D_MODEL),
            stack(rg_out, 0), stack(rg_out, 1), stack(rg_out, 2), stack(rg_out, 3),
            stack(swa_out, 0), stack(swa_out, 1), stack(swa_out, 2), stack(swa_out, 3),
            stack(fox_out, 0), stack(fox_out, 1), stack(fox_out, 2), stack(fox_out, 3),
            stack(fox_out, 4), stack(fox_out, 5))
```

```python
import functools
import math

import jax
import jax.numpy as jnp
from jax import lax
from jax.experimental import pallas as pl
from jax.experimental.pallas import tpu as pltpu

F32 = jnp.float32
BF16 = jnp.bfloat16

D_MODEL = 1024
BATCH = 4
SEQ = 4096
DEPTH = 4
DEC_BATCH = 16
DEC_SEQ = 64
PAST_LEN = 1024
CHUNK = 64
D_FF = 4 * D_MODEL
HEAD_DIM = 64
N_HEADS = 16
SWA_KV_HEADS = 4
SWA_GROUP = 4
WINDOW = 128
LRU_WIDTH = D_MODEL
LRU_BLOCKS = 4
LRU_BLOCK_W = LRU_WIDTH // LRU_BLOCKS
CONV_WIDTH = 4
LRU_C = 8.0
N_BUCKETS = 32
MAX_DISTANCE = 128
ALPHA = (2.0 * DEPTH) ** 0.25
LN_EPS = 1e-5
ATTN_SCALE = HEAD_DIM ** -0.5

N_PROMPT = BATCH * SEQ
N_SAMPLE = DEC_BATCH * DEC_SEQ

LANES = 128
HEADS_PER_SLAB = LANES // HEAD_DIM
N_SLABS = D_MODEL // LANES

TOKEN_TILE = 512
POST_TILE = 1024
POST_ROWS = 256
FF_CHUNK = 1024
RG_TILE = 512
RG_PROJ_CHUNKS = 8
SWA_TILE = 512
SWA_KEYS = WINDOW + CHUNK
FOX_TQ = 512
FOX_TK = 512
FOX_ONES_ROWS = 16
BIAS_TILE = 512
VMEM_LIMIT = 56 * 1024 * 1024


def _resident(shape, layer=None):
    zeros = (0,) * len(shape)
    if layer is None:
        return pl.BlockSpec(shape, lambda *_: zeros, pipeline_mode=pl.Buffered(1))
    return pl.BlockSpec((None,) + tuple(shape), lambda *_: (layer,) + zeros,
                        pipeline_mode=pl.Buffered(1))


def _params(semantics):
    return pltpu.CompilerParams(dimension_semantics=semantics, vmem_limit_bytes=VMEM_LIMIT)


def _softplus(x):
    return jnp.maximum(x, 0.0) + jnp.log1p(jnp.exp(-jnp.abs(x)))


def _layer_norm(z, g, b):
    mu = jnp.mean(z, axis=-1, keepdims=True)
    zc = z - mu
    var = jnp.mean(zc * zc, axis=-1, keepdims=True)
    return zc * lax.rsqrt(var + LN_EPS) * g + b


def _post_kernel(x_ref, o_ref, wout_ref, wup_ref, wdn_ref, g_ref, b_ref, out_ref, *, slab_major):
    groups = [slice(r * POST_ROWS, (r + 1) * POST_ROWS) for r in range(POST_TILE // POST_ROWS)]
    y = []
    for rows in groups:
        if slab_major:
            o = jnp.concatenate([o_ref[s, rows, :] for s in range(N_SLABS)], axis=1)
        else:
            o = o_ref[rows, :]
        y.append(jnp.dot(o, wout_ref[...], preferred_element_type=F32))
    x1 = [_layer_norm(ALPHA * x_ref[rows, :] + y_r, g_ref[0:1, :], b_ref[0:1, :])
          for rows, y_r in zip(groups, y)]
    for rows, x1_r in zip(groups, x1):
        x1b = x1_r.astype(BF16)
        acc = jnp.zeros_like(x1_r)
        for c in range(D_FF // FF_CHUNK):
            cols = slice(c * FF_CHUNK, (c + 1) * FF_CHUNK)
            h = jnp.dot(x1b, wup_ref[:, cols], preferred_element_type=F32)
            a = jnp.square(jnp.maximum(h, 0.0)).astype(BF16)
            acc = acc + jnp.dot(a, wdn_ref[cols, :], preferred_element_type=F32)
        out_ref[rows, :] = _layer_norm(ALPHA * x1_r + acc, g_ref[1:2, :], b_ref[1:2, :])


def _post_call(x, o, w_out, mixer_idx, w_up, w_down, gain, bias, layer):
    n = x.shape[0]
    tile = pl.BlockSpec((POST_TILE, D_MODEL), lambda i: (i, 0))
    slab_major = o.ndim == 3
    o_tile = pl.BlockSpec((N_SLABS, POST_TILE, LANES), lambda i: (0, i, 0)) if slab_major else tile
    return pl.pallas_call(
        functools.partial(_post_kernel, slab_major=slab_major),
        out_shape=jax.ShapeDtypeStruct((n, D_MODEL), F32),
        grid=(n // POST_TILE,),
        in_specs=[tile, o_tile, _resident((D_MODEL, D_MODEL), mixer_idx),
                  _resident((D_MODEL, D_FF), layer), _resident((D_FF, D_MODEL), layer),
                  _resident((2, D_MODEL), layer), _resident((2, D_MODEL), layer)],
        out_specs=tile,
        compiler_params=_params(("parallel",)),
        name="post",
    )(x, o, w_out, w_up, w_down, gain, bias)


def _run(*staged, order=None):
    live = {i: iter(s) for i, s in enumerate(staged)}
    plan = list(order or [])
    while live:
        turn = plan.pop(0) if plan else None
        for i in ([turn] if turn in live else list(live)):
            try:
                next(live[i])
            except StopIteration:
                del live[i]


def _rg_project(x_ref, perm_ref, win_ref, gu_ref):
    xb = x_ref[...].astype(BF16)
    xb = jnp.dot(perm_ref[...], xb, preferred_element_type=F32).astype(BF16)
    yield
    width = 2 * LRU_WIDTH // RG_PROJ_CHUNKS
    for c in range(RG_PROJ_CHUNKS):
        cols = slice(c * width, (c + 1) * width)
        gu_ref[:, cols] = jnp.dot(xb, win_ref[:, cols], preferred_element_type=F32)
        yield


def _rg_mix(gu_ref, unperm_ref, cw_ref, cb_ref, gw_ref, gb_ref, lam_ref, o_ref, nbuf_ref, rows,
            scratch, *, tile):
    u_scr, gg_scr, h_scr, ac_scr, hist_scr, hc_scr = scratch
    hist = CONV_WIDTH - 1
    seg = tile // 8
    lead = 8 * hist

    u = gu_ref[:, LRU_WIDTH:]
    u_scr[lead:lead + tile, :] = u
    gg_scr[...] = jax.nn.gelu(gu_ref[:, :LRU_WIDTH])
    yield
    sub = lax.broadcasted_iota(jnp.int32, (8, LRU_WIDTH), 0)
    for d in range(1, hist + 1):
        prev_tail = pltpu.roll(u_scr[lead + 8 * (seg - d):lead + 8 * (seg - d) + 8, :], 1, 0)
        u_scr[lead - 8 * d:lead - 8 * d + 8, :] = jnp.where(sub == 0, hist_scr[8 - d:8 - d + 1, :],
                                                           prev_tail)
    conv = cb_ref[...] + u * cw_ref[hist:hist + 1, :]
    for k in range(hist):
        conv = conv + u_scr[8 * k:8 * k + tile, :] * cw_ref[k:k + 1, :]
    for d in range(1, hist + 1):
        last = u_scr[lead + 8 * (seg - d) + 7:lead + 8 * (seg - d) + 8, :]
        hist_scr[8 - d:8 - d + 1, :] = last
        nbuf_ref[0, hist - d:hist - d + 1, :] = last
    yield

    sub_blk = lax.broadcasted_iota(jnp.int32, (8, LRU_BLOCK_W), 0)
    for n in range(LRU_BLOCKS):
        cols = slice(n * LRU_BLOCK_W, (n + 1) * LRU_BLOCK_W)
        cn = conv[:, cols]
        g = jnp.dot(cn.astype(BF16), gw_ref[n], preferred_element_type=F32) + gb_ref[n]
        r = jax.nn.sigmoid(g[:, :LRU_BLOCK_W])
        ig = jax.nn.sigmoid(g[:, LRU_BLOCK_W:])
        log_a = -LRU_C * r * _softplus(-lam_ref[:, cols])
        a_blk = jnp.exp(log_a)
        var = 1.0 - a_blk * a_blk
        root = jnp.where(var > 0.0, var * lax.rsqrt(var), 0.0)
        b_blk = root * (ig * cn)
        h = jnp.zeros((8, LRU_BLOCK_W), F32)
        prod = jnp.ones((8, LRU_BLOCK_W), F32)
        for j in range(seg):
            a = a_blk[8 * j:8 * j + 8, :]
            h = a * h + b_blk[8 * j:8 * j + 8, :]
            prod = a * prod
            h_scr[8 * j:8 * j + 8, cols] = h
            ac_scr[8 * j:8 * j + 8, cols] = prod
        carry = jnp.where(sub_blk == 0, hc_scr[:, cols], 0.0)
        for i in range(1, 8):
            carry = jnp.where(sub_blk == i, pltpu.roll(prod * carry + h, 1, 0), carry)
        hc_scr[:, cols] = (prod * carry + h)[7:8, :]
        for j in range(seg):
            h_scr[8 * j:8 * j + 8, cols] = (h_scr[8 * j:8 * j + 8, cols]
                                            + ac_scr[8 * j:8 * j + 8, cols] * carry)
        yield

    o = (h_scr[...] * gg_scr[...]).astype(BF16)
    o_ref[rows, :] = jnp.dot(unperm_ref[...], o, preferred_element_type=F32).astype(BF16)


def _rg_kernel(*refs, tile, pipelined):
    n_x = 3 if pipelined else 1
    x_refs = refs[:n_x]
    (cbuf_ref, h0_ref, perm_ref, unperm_ref, win_ref, cw_ref, cb_ref, gw_ref, gb_ref, lam_ref,
     o_ref, nbuf_ref, hlast_ref, gu_a, gu_b) = refs[n_x:n_x + 15]
    scratch = refs[n_x + 15:]
    hist_scr, hc_scr = scratch[-2:]
    hist = CONV_WIDTH - 1

    @pl.when(pl.program_id(1) == 0)
    def _():
        hist_scr[8 - hist:8, :] = cbuf_ref[0]
        hc_scr[...] = h0_ref[0]
        _run(_rg_project(x_refs[0], perm_ref, win_ref, gu_a))

    mix = functools.partial(_rg_mix, unperm_ref=unperm_ref, cw_ref=cw_ref, cb_ref=cb_ref, gw_ref=gw_ref,
                            gb_ref=gb_ref, lam_ref=lam_ref, o_ref=o_ref, nbuf_ref=nbuf_ref,
                            scratch=scratch, tile=tile)
    if pipelined:
        order = [1, 0, 1, 1, 0, 0, 1, 0, 0, 1, 0, 0, 1, 0, 0, 1]
        _run(_rg_project(x_refs[1], perm_ref, win_ref, gu_b), mix(gu_a, rows=slice(0, tile)),
             order=order)
        _run(_rg_project(x_refs[2], perm_ref, win_ref, gu_a), mix(gu_b, rows=slice(tile, 2 * tile)),
             order=order)
    else:
        _run(mix(gu_a, rows=slice(0, tile)))
    hlast_ref[0] = hc_scr[...]


def _segment_interleave(tile):
    seg = tile // 8
    src = [(p % 8) * seg + p // 8 for p in range(tile)]
    return jnp.zeros((tile, tile), BF16).at[jnp.arange(tile), jnp.array(src)].set(1.0)


def _rg_call(x, conv_buf, h0, w_in, gate_w, small, layer, *, n_seq, seq_len, tile):
    conv_w, conv_b, gate_b, lam = small
    n_t = seq_len // tile
    pipelined = n_t >= 2 and n_t % 2 == 0
    tiles_per_step = 2 if pipelined else 1
    n_steps = n_t // tiles_per_step
    perm = _segment_interleave(tile)
    if pipelined:
        x_specs = [pl.BlockSpec((tile, D_MODEL), lambda b, s: (b * n_t, 0)),
                   pl.BlockSpec((tile, D_MODEL), lambda b, s: (b * n_t + 2 * s + 1, 0)),
                   pl.BlockSpec((tile, D_MODEL),
                                lambda b, s: (b * n_t + jnp.minimum(2 * s + 2, n_t - 1), 0))]
    else:
        x_specs = [pl.BlockSpec((tile, D_MODEL), lambda b, s: (b * n_t + s, 0))]
    kern = functools.partial(_rg_kernel, tile=tile, pipelined=pipelined)
    o, nbuf, hlast = pl.pallas_call(
        kern,
        out_shape=(jax.ShapeDtypeStruct((n_seq * seq_len, D_MODEL), BF16),
                   jax.ShapeDtypeStruct((n_seq, CONV_WIDTH - 1, LRU_WIDTH), F32),
                   jax.ShapeDtypeStruct((n_seq, 1, LRU_WIDTH), F32)),
        grid=(n_seq, n_steps),
        in_specs=x_specs + [
            pl.BlockSpec((1, CONV_WIDTH - 1, LRU_WIDTH), lambda b, s: (b, 0, 0)),
            pl.BlockSpec((1, 1, LRU_WIDTH), lambda b, s: (b, 0, 0)),
            _resident((tile, tile)), _resident((tile, tile)),
            _resident((D_MODEL, 2 * LRU_WIDTH), layer),
            _resident((CONV_WIDTH, LRU_WIDTH)),
            _resident((1, LRU_WIDTH)),
            _resident((LRU_BLOCKS, LRU_BLOCK_W, 2 * LRU_BLOCK_W), layer),
            _resident((LRU_BLOCKS, 1, 2 * LRU_BLOCK_W)),
            _resident((1, LRU_WIDTH))],
        out_specs=(pl.BlockSpec((tiles_per_step * tile, D_MODEL), lambda b, s: (b * n_steps + s, 0)),
                   pl.BlockSpec((1, CONV_WIDTH - 1, LRU_WIDTH), lambda b, s: (b, 0, 0)),
                   pl.BlockSpec((1, 1, LRU_WIDTH), lambda b, s: (b, 0, 0))),
        scratch_shapes=[pltpu.VMEM((tile, 2 * LRU_WIDTH), F32),
                        pltpu.VMEM((tile, 2 * LRU_WIDTH), F32),
                        pltpu.VMEM((tile + 8 * (CONV_WIDTH - 1), LRU_WIDTH), F32),
                        pltpu.VMEM((tile, LRU_WIDTH), F32),
                        pltpu.VMEM((tile, LRU_WIDTH), F32),
                        pltpu.VMEM((tile, LRU_WIDTH), F32),
                        pltpu.VMEM((8, LRU_WIDTH), F32),
                        pltpu.VMEM((1, LRU_WIDTH), F32)],
        compiler_params=_params(("parallel", "arbitrary")),
        name="rg_mixer",
    )(*([x] * len(x_specs)), conv_buf, h0, perm, perm.T, w_in, conv_w, conv_b, gate_w, gate_b, lam)
    return o, nbuf, hlast[:, 0, :]


def _swa_proj_kernel(x_ref, w_ref, q_ref, k_ref, v_ref, kb_ref, vb_ref):
    xb = x_ref[...].astype(BF16)
    qkv = jnp.dot(xb, w_ref[...], preferred_element_type=F32)
    kvd = SWA_KV_HEADS * HEAD_DIM
    q_ref[...] = qkv[:, :D_MODEL].astype(BF16)
    k = qkv[:, D_MODEL:D_MODEL + kvd]
    v = qkv[:, D_MODEL + kvd:]
    k_ref[...] = k
    v_ref[...] = v
    kb_ref[...] = k.astype(BF16)
    vb_ref[...] = v.astype(BF16)


def _swa_proj_call(x, w):
    n = x.shape[0]
    kvd = SWA_KV_HEADS * HEAD_DIM
    row = lambda i: (i, 0)
    return pl.pallas_call(
        _swa_proj_kernel,
        out_shape=(jax.ShapeDtypeStruct((n, D_MODEL), BF16),
                   jax.ShapeDtypeStruct((n, kvd), F32),
                   jax.ShapeDtypeStruct((n, kvd), F32),
                   jax.ShapeDtypeStruct((n, kvd), BF16),
                   jax.ShapeDtypeStruct((n, kvd), BF16)),
        grid=(n // TOKEN_TILE,),
        in_specs=[pl.BlockSpec((TOKEN_TILE, D_MODEL), row), _resident((D_MODEL, D_MODEL + 2 * kvd))],
        out_specs=(pl.BlockSpec((TOKEN_TILE, D_MODEL), row),
                   pl.BlockSpec((TOKEN_TILE, kvd), row), pl.BlockSpec((TOKEN_TILE, kvd), row),
                   pl.BlockSpec((TOKEN_TILE, kvd), row), pl.BlockSpec((TOKEN_TILE, kvd), row)),
        compiler_params=_params(("parallel",)),
        name="swa_proj",
    )(x, w)


def _swa_attn_kernel(q_ref, k_ref, v_ref, bias_ref, sink_ref, o_ref, s_even, s_odd,
                     *, tile, n_invalid):
    ti = pl.program_id(1)
    cols = SWA_GROUP * HEADS_PER_SLAB * CHUNK
    half_cols = cols // HEADS_PER_SLAB
    lane_q = lax.broadcasted_iota(jnp.int32, (CHUNK, LANES), 1)
    low_q = lane_q < HEAD_DIM
    key_idx = lax.broadcasted_iota(jnp.int32, (SWA_KEYS, cols), 0)
    n_kv_slabs = SWA_KV_HEADS // HEADS_PER_SLAB
    units = [(c, t) for c in range(tile // CHUNK) for t in range(n_kv_slabs)]
    s_bufs = (s_even, s_odd)

    def window_start(c):
        return pl.multiple_of(ti * tile + c * CHUNK, CHUNK)

    def scores(u):
        c, t = units[u]
        row0 = window_start(c)
        kw = k_ref[0, pl.ds(row0, SWA_KEYS), t * LANES:(t + 1) * LANES]
        slabs = [q_ref[0, c * CHUNK:(c + 1) * CHUNK,
                       (SWA_GROUP * t + g) * LANES:(SWA_GROUP * t + g + 1) * LANES]
                 for g in range(SWA_GROUP)]
        zero = jnp.zeros_like(slabs[0])
        q_stack = jnp.concatenate([jnp.where(low_q, s_, zero) for s_ in slabs]
                                  + [jnp.where(low_q, zero, s_) for s_ in slabs], axis=0)
        s = lax.dot_general(kw, q_stack, (((1,), (1,)), ((), ())), preferred_element_type=F32)
        s = s + bias_ref[t]
        if n_invalid > 0 and c * CHUNK < n_invalid:
            s = jnp.where(key_idx + row0 >= n_invalid, s, -jnp.inf)
        s_bufs[u % 2][...] = s

    def attend(u):
        c, t = units[u]
        s = s_bufs[u % 2][...]
        vw = v_ref[0, pl.ds(window_start(c), SWA_KEYS), t * LANES:(t + 1) * LANES]
        sink = sink_ref[t]
        m = jnp.maximum(jnp.max(s, axis=0, keepdims=True), sink)
        p = jnp.exp(s - m)
        denom = jnp.sum(p, axis=0, keepdims=True) + jnp.exp(sink - m)
        out_t = lax.dot_general(vw, p.astype(BF16), (((0,), (0,)), ((), ())),
                                preferred_element_type=F32) / denom
        pair = jnp.concatenate([out_t[:HEAD_DIM, :half_cols], out_t[HEAD_DIM:, half_cols:]], axis=0)
        pair = pair.T.astype(BF16)
        for g in range(SWA_GROUP):
            s_idx = SWA_GROUP * t + g
            o_ref[0, c * CHUNK:(c + 1) * CHUNK, s_idx * LANES:(s_idx + 1) * LANES] = (
                pair[g * CHUNK:(g + 1) * CHUNK, :])

    scores(0)
    for u in range(len(units)):
        if u + 1 < len(units):
            scores(u + 1)
        attend(u)


def _swa_attn_call(q, k_win, v_win, bias, sink, *, n_seq, seq_len, tile, n_invalid):
    kvd = SWA_KV_HEADS * HEAD_DIM
    n_keys = k_win.shape[1]
    cols = SWA_GROUP * HEADS_PER_SLAB * CHUNK
    n_kv_slabs = SWA_KV_HEADS // HEADS_PER_SLAB
    kern = functools.partial(_swa_attn_kernel, tile=tile, n_invalid=n_invalid)
    return pl.pallas_call(
        kern,
        out_shape=jax.ShapeDtypeStruct((n_seq, seq_len, D_MODEL), BF16),
        grid=(n_seq, seq_len // tile),
        in_specs=[pl.BlockSpec((1, tile, D_MODEL), lambda b, t: (b, t, 0)),
                  pl.BlockSpec((1, n_keys, kvd), lambda b, t: (b, 0, 0)),
                  pl.BlockSpec((1, n_keys, kvd), lambda b, t: (b, 0, 0)),
                  _resident((n_kv_slabs, SWA_KEYS, cols)),
                  _resident((n_kv_slabs, 1, cols))],
        out_specs=pl.BlockSpec((1, tile, D_MODEL), lambda b, t: (b, t, 0)),
        scratch_shapes=[pltpu.VMEM((SWA_KEYS, cols), F32), pltpu.VMEM((SWA_KEYS, cols), F32)],
        compiler_params=_params(("parallel", "arbitrary")),
        name="swa_attn",
    )(q, k_win, v_win, bias, sink)


def _t5_bucket(rel):
    half = N_BUCKETS // 2
    max_exact = half // 2
    n = jnp.abs(rel)
    n_f = jnp.maximum(n, 1).astype(jnp.float32)
    large = max_exact + (jnp.log(n_f / max_exact) / math.log(MAX_DISTANCE / max_exact)
                         * (half - max_exact)).astype(jnp.int32)
    large = jnp.minimum(large, half - 1)
    return jnp.where(rel > 0, half, 0) + jnp.where(n < max_exact, n, large)


_SWA_HEAD_ORDER = [8 * t + 4 * p + g for t in range(2) for g in range(4) for p in range(2)]


def _swa_row_tables(table, sinks):
    rel = jnp.arange(SWA_KEYS)[None, :] - WINDOW - jnp.arange(CHUNK)[:, None]
    bucket = _t5_bucket(rel)
    tbl = table.astype(F32)
    hit = bucket[None, None] == jnp.arange(N_BUCKETS)[:, None, None, None]
    bias = jnp.sum(jnp.where(hit, tbl[:, :, None, None], 0.0), axis=0)
    n_kv_slabs = SWA_KV_HEADS // HEADS_PER_SLAB
    cols = SWA_GROUP * HEADS_PER_SLAB * CHUNK
    bias = bias.reshape(n_kv_slabs, cols, SWA_KEYS).transpose(0, 2, 1)
    sink = jnp.repeat(sinks.astype(F32), CHUNK).reshape(n_kv_slabs, 1, cols)
    return bias, sink


def _fox_proj_kernel(x_ref, w_ref, wvt_ref, bf_ref, q_ref, k_ref, v_ref, kb_ref, vt_ref, lf_ref,
                     *, time_minor):
    xb = x_ref[...].astype(BF16)
    proj = jnp.dot(xb, w_ref[...], preferred_element_type=F32)
    k = proj[:, D_MODEL:2 * D_MODEL]
    vt = lax.dot_general(wvt_ref[...], xb, (((1,), (1,)), ((), ())), preferred_element_type=F32)
    if time_minor:
        k_ref[0] = k.T
        v_ref[0] = vt
        z = proj[:, 2 * D_MODEL:]
    else:
        k_ref[...] = k.reshape(TOKEN_TILE, N_HEADS, HEAD_DIM)
        v_ref[...] = proj[:, 2 * D_MODEL:3 * D_MODEL].reshape(TOKEN_TILE, N_HEADS, HEAD_DIM)
        z = proj[:, 3 * D_MODEL:]
    for s in range(N_SLABS):
        cols = slice(s * LANES, (s + 1) * LANES)
        q_ref[s] = proj[:, cols].astype(BF16)
        kb_ref[s] = k[:, cols].astype(BF16)
        for j in range(TOKEN_TILE // FOX_TK):
            vt_ref[s, j] = vt[cols, j * FOX_TK:(j + 1) * FOX_TK].astype(BF16)
    lf_ref[...] = -_softplus(-(z + bf_ref[...]))


def _fox_proj_call(x, w, w_vt, b_f, *, seq_len, time_minor):
    n = x.shape[0]
    row = lambda i: (i, 0)
    wide = pl.BlockSpec((TOKEN_TILE, D_MODEL), row)
    slabs = pl.BlockSpec((N_SLABS, TOKEN_TILE, LANES), lambda i: (0, i, 0))
    kb_per_tile = TOKEN_TILE // FOX_TK
    if time_minor:
        t_per_seq = seq_len // TOKEN_TILE
        kv_shape = jax.ShapeDtypeStruct((n // seq_len, D_MODEL, seq_len), F32)
        kv_spec = pl.BlockSpec((1, D_MODEL, TOKEN_TILE), lambda i: (i // t_per_seq, 0, i % t_per_seq))
    else:
        kv_shape = jax.ShapeDtypeStruct((n, N_HEADS, HEAD_DIM), F32)
        kv_spec = pl.BlockSpec((TOKEN_TILE, N_HEADS, HEAD_DIM), lambda i: (i, 0, 0))
    return pl.pallas_call(
        functools.partial(_fox_proj_kernel, time_minor=time_minor),
        out_shape=(jax.ShapeDtypeStruct((N_SLABS, n, LANES), BF16), kv_shape, kv_shape,
                   jax.ShapeDtypeStruct((N_SLABS, n, LANES), BF16),
                   jax.ShapeDtypeStruct((N_SLABS, n // FOX_TK, LANES, FOX_TK), BF16),
                   jax.ShapeDtypeStruct((n, LANES), F32)),
        grid=(n // TOKEN_TILE,),
        in_specs=[wide, _resident(w.shape), _resident((D_MODEL, D_MODEL)), _resident((1, LANES))],
        out_specs=(slabs, kv_spec, kv_spec, slabs,
                   pl.BlockSpec((N_SLABS, kb_per_tile, LANES, FOX_TK), lambda i: (0, i, 0, 0)),
                   pl.BlockSpec((TOKEN_TILE, LANES), row)),
        compiler_params=_params(("parallel",)),
        name="fox_proj",
    )(x, w, w_vt, b_f)


def _lane_split3(x, lane):
    hi = x.astype(BF16)
    r1 = x - hi.astype(F32)
    mid = r1.astype(BF16)
    lo = (r1 - mid.astype(F32)).astype(BF16)
    return jnp.where(lane < N_HEADS, hi, jnp.where(lane < 2 * N_HEADS, mid, lo))


def _fox_bias_kernel(lf_ref, tri_ref, place_ref, const_ref, bq_ref, bk_ref, carry_scr):
    t = pl.program_id(1)

    @pl.when(t == 0)
    def _():
        carry_scr[...] = jnp.zeros_like(carry_scr)

    lane = lax.broadcasted_iota(jnp.int32, (BIAS_TILE, LANES), 1)
    sums = carry_scr[...] + jnp.dot(tri_ref[...], _lane_split3(lf_ref[0], lane),
                                    preferred_element_type=F32)
    carry_scr[...] = sums[BIAS_TILE - 1:BIAS_TILE, :]
    c = sums + pltpu.roll(sums, LANES - N_HEADS, 1) + pltpu.roll(sums, LANES - 2 * N_HEADS, 1)
    c = jnp.where(lane < N_HEADS, c, 0.0)
    c = c + pltpu.roll(c, N_HEADS, 1) + pltpu.roll(c, 2 * N_HEADS, 1)
    both = const_ref[...] + jnp.dot(_lane_split3(c, lane), place_ref[...],
                                    preferred_element_type=F32)
    bq_ref[0] = both[:, :LANES].astype(BF16)
    bk_ref[0] = both[:, LANES:].astype(BF16)


def _fox_bias_tables():
    place = [[0.0] * (2 * LANES) for _ in range(LANES)]
    const = [0.0] * (2 * LANES)
    for i in range(3):
        for h in range(N_HEADS):
            const[N_HEADS * i + h] = 1.0
            place[N_HEADS * i + h][N_HEADS * (3 + i) + h] = 1.0
            place[N_HEADS * i + h][LANES + N_HEADS * i + h] = -1.0
            const[LANES + N_HEADS * (3 + i) + h] = 1.0
    return jnp.array(place, BF16), jnp.array([const], F32)


def _fox_bias_call(logf):
    n_seq, n_rows, _ = logf.shape
    place, const = _fox_bias_tables()
    tri = jnp.tril(jnp.ones((BIAS_TILE, BIAS_TILE), BF16))
    blk = pl.BlockSpec((1, BIAS_TILE, LANES), lambda b, t: (b, t, 0))
    return pl.pallas_call(
        _fox_bias_kernel,
        out_shape=(jax.ShapeDtypeStruct((n_seq, n_rows, LANES), BF16),
                   jax.ShapeDtypeStruct((n_seq, n_rows, LANES), BF16)),
        grid=(n_seq, n_rows // BIAS_TILE),
        in_specs=[blk, _resident((BIAS_TILE, BIAS_TILE)), _resident((LANES, 2 * LANES)),
                  _resident((1, 2 * LANES))],
        out_specs=(blk, blk),
        scratch_shapes=[pltpu.VMEM((1, LANES), F32)],
        compiler_params=_params(("parallel", "arbitrary")),
        name="fox_bias",
    )(logf, tri, place, const)


def _fox_attn_core(q_ref, bq_ref, bk_ref, o_ref, scratch, *, tq, tk, first_q, run_blocks):
    qaug_scr, m_scr, acc_scr, s_even, s_odd, smax_even, smax_odd = scratch
    lane = lax.broadcasted_iota(jnp.int32, (tq, LANES), 1)
    low_lane = lane < HEAD_DIM
    bias_lane = lane < 6 * N_HEADS
    k_off = lax.broadcasted_iota(jnp.int32, (tk, tq), 0)
    q_pos = first_q + lax.broadcasted_iota(jnp.int32, (tk, tq), 1)
    bq = bq_ref[0]
    zero = jnp.zeros_like(bq)
    ones_rows = jnp.ones((FOX_ONES_ROWS, tk), BF16)

    for head in range(N_HEADS):
        q_slab = q_ref[head // HEADS_PER_SLAB]
        own_half = low_lane if head % HEADS_PER_SLAB == 0 else jnp.logical_not(low_lane)
        sel = jnp.logical_and((lane & (N_HEADS - 1)) == head, bias_lane)
        qaug_scr[head] = jnp.concatenate([jnp.where(own_half, q_slab, zero),
                                          jnp.where(sel, bq, zero)], axis=1)
    m_scr[...] = jnp.full_like(m_scr, -jnp.inf)
    acc_scr[...] = jnp.zeros_like(acc_scr)
    slots = ((s_even, smax_even), (s_odd, smax_odd))

    def key_block(kb, masked, k_of, vt_of):
        bk_blk = bk_ref[0, pl.ds(pl.multiple_of(kb * tk, tk), tk), :]

        def scores(hp):
            s_buf, smax_buf = slots[hp % 2]
            k_aug = jnp.concatenate([k_of(hp), bk_blk], axis=1)
            for par in range(HEADS_PER_SLAB):
                s = lax.dot_general(k_aug, qaug_scr[hp * HEADS_PER_SLAB + par],
                                    (((1,), (1,)), ((), ())), preferred_element_type=F32)
                if masked:
                    s = jnp.where(kb * tk + k_off <= q_pos, s, -jnp.inf)
                s_buf[par] = s
                smax_buf[par] = jnp.max(s, axis=0, keepdims=True)

        def absorb(hp):
            s_buf, smax_buf = slots[hp % 2]
            vt = vt_of(hp)
            for par in range(HEADS_PER_SLAB):
                head = hp * HEADS_PER_SLAB + par
                m = m_scr[head]
                m_new = jnp.maximum(m, smax_buf[par])
                p = jnp.exp(s_buf[par] - m_new).astype(BF16)
                vt_aug = jnp.concatenate([vt[par * HEAD_DIM:(par + 1) * HEAD_DIM, :], ones_rows], axis=0)
                acc_scr[head] = (jnp.exp(m - m_new) * acc_scr[head]
                                 + jnp.dot(vt_aug, p, preferred_element_type=F32))
                m_scr[head] = m_new

        scores(0)
        for hp in range(N_SLABS):
            if hp + 1 < N_SLABS:
                scores(hp + 1)
            absorb(hp)

    run_blocks(key_block)

    for hp in range(N_SLABS):
        halves = []
        for par in range(HEADS_PER_SLAB):
            acc = acc_scr[hp * HEADS_PER_SLAB + par]
            halves.append(acc[:HEAD_DIM] / acc[HEAD_DIM:HEAD_DIM + 1])
        o_ref[hp] = jnp.concatenate(halves, axis=0).T.astype(BF16)


def _fox_attn_scratch(tq, tk):
    return [pltpu.VMEM((N_HEADS, tq, 2 * LANES), BF16),
            pltpu.VMEM((N_HEADS, 1, tq), F32),
            pltpu.VMEM((N_HEADS, HEAD_DIM + FOX_ONES_ROWS, tq), F32),
            pltpu.VMEM((HEADS_PER_SLAB, tk, tq), F32),
            pltpu.VMEM((HEADS_PER_SLAB, tk, tq), F32),
            pltpu.VMEM((HEADS_PER_SLAB, 1, tq), F32),
            pltpu.VMEM((HEADS_PER_SLAB, 1, tq), F32)]


def _fox_attn_kernel(q_ref, bq_ref, k_ref, bk_ref, vt_ref, o_ref, *scratch, tq, tk):
    first_q = pl.program_id(1) * tq
    n_full = first_q // tk

    def run_blocks(key_block):
        def block(kb, masked):
            r0 = pl.multiple_of(kb * tk, tk)
            key_block(kb, masked, lambda hp: k_ref[hp, pl.ds(r0, tk), :], lambda hp: vt_ref[hp, kb])

        def full_block(kb, carry):
            block(kb, False)
            return carry

        lax.fori_loop(0, n_full, full_block, 0)
        block(n_full, True)

    _fox_attn_core(q_ref, bq_ref, bk_ref, o_ref, scratch, tq=tq, tk=tk, first_q=first_q,
                   run_blocks=run_blocks)


def _fox_attn_call(q, bq, k, bk, vt, *, n_seq, seq_len, tq, tk):
    assert tq <= tk and tk % tq == 0 and seq_len % tk == 0
    n_qt = seq_len // tq
    n_kb = seq_len // tk
    return pl.pallas_call(
        functools.partial(_fox_attn_kernel, tq=tq, tk=tk),
        out_shape=jax.ShapeDtypeStruct((N_SLABS, n_seq * seq_len, LANES), BF16),
        grid=(n_seq, n_qt),
        in_specs=[pl.BlockSpec((N_SLABS, tq, LANES), lambda b, i: (0, b * n_qt + i, 0)),
                  pl.BlockSpec((1, tq, LANES), lambda b, i: (b, i, 0)),
                  pl.BlockSpec((N_SLABS, seq_len, LANES), lambda b, i: (0, b, 0)),
                  pl.BlockSpec((1, seq_len, LANES), lambda b, i: (b, 0, 0)),
                  pl.BlockSpec((N_SLABS, n_kb, LANES, tk), lambda b, i: (0, b, 0, 0))],
        out_specs=pl.BlockSpec((N_SLABS, tq, LANES), lambda b, i: (0, b * n_qt + i, 0)),
        scratch_shapes=_fox_attn_scratch(tq, tk),
        compiler_params=_params(("parallel", "arbitrary")),
        name="fox_attn",
    )(q, bq, k, bk, vt)


def _fox_step_attn_kernel(q_ref, bq_ref, kold_ref, knew_ref, bk_ref, vtold_ref, vtnew_ref, o_ref,
                          *scratch, n_new, n_old, tk):
    def run_blocks(key_block):
        for kb in range(n_old // tk):
            rows = slice(kb * tk, (kb + 1) * tk)
            key_block(kb, False, lambda hp, rows=rows: kold_ref[0, hp, :, rows].T.astype(BF16),
                      lambda hp, rows=rows: vtold_ref[0, hp, :, rows].astype(BF16))
        key_block(n_old // tk, True, lambda hp: knew_ref[hp], lambda hp: vtnew_ref[hp, 0])

    _fox_attn_core(q_ref, bq_ref, bk_ref, o_ref, scratch, tq=n_new, tk=tk, first_q=n_old,
                   run_blocks=run_blocks)


def _fox_step_attn_call(q, bq, kt_old, k_new, bk, vt_old, vt_new, *, n_seq, n_new, n_old, tk):
    assert n_new <= tk and n_old % tk == 0 and n_old % n_new == 0
    return pl.pallas_call(
        functools.partial(_fox_step_attn_kernel, n_new=n_new, n_old=n_old, tk=tk),
        out_shape=jax.ShapeDtypeStruct((N_SLABS, n_seq * n_new, LANES), BF16),
        grid=(n_seq,),
        in_specs=[pl.BlockSpec((N_SLABS, n_new, LANES), lambda b: (0, b, 0)),
                  pl.BlockSpec((1, n_new, LANES), lambda b: (b, n_old // n_new, 0)),
                  pl.BlockSpec((1, N_SLABS, LANES, n_old), lambda b: (b, 0, 0, 0)),
                  pl.BlockSpec((N_SLABS, tk, LANES), lambda b: (0, b, 0)),
                  pl.BlockSpec((1, n_old + tk, LANES), lambda b: (b, 0, 0)),
                  pl.BlockSpec((1, N_SLABS, LANES, n_old), lambda b: (b, 0, 0, 0)),
                  pl.BlockSpec((N_SLABS, 1, LANES, tk), lambda b: (0, b, 0, 0))],
        out_specs=pl.BlockSpec((N_SLABS, n_new, LANES), lambda b: (0, b, 0)),
        scratch_shapes=_fox_attn_scratch(n_new, tk),
        compiler_params=_params(("parallel",)),
        name="fox_step_attn",
    )(q, bq, kt_old, k_new, bk, vt_old, vt_new)


def _rg_layer(xs, conv_state, h_state, w_in, gate_w, small, layer):
    x_p, x_s = xs
    o_p, nb_p, h_p = _rg_call(
        x_p, jnp.zeros((BATCH, CONV_WIDTH - 1, LRU_WIDTH), F32), jnp.zeros((BATCH, 1, LRU_WIDTH), F32),
        w_in, gate_w, small, layer, n_seq=BATCH, seq_len=SEQ, tile=RG_TILE)
    o_s, nb_s, h_s = _rg_call(
        x_s, conv_state, h_state.reshape(DEC_BATCH, 1, LRU_WIDTH),
        w_in, gate_w, small, layer, n_seq=DEC_BATCH, seq_len=DEC_SEQ, tile=DEC_SEQ)
    return (o_p, o_s), (nb_p, nb_s, h_p, h_s)


def _swa_layer(xs, k_cache, v_cache, w_qkv, sinks, table):
    kvd = SWA_KV_HEADS * HEAD_DIM
    order = jnp.array(_SWA_HEAD_ORDER)
    w_q = (w_qkv[:, :D_MODEL] * ATTN_SCALE).reshape(D_MODEL, N_HEADS, HEAD_DIM)[:, order]
    w = jnp.concatenate([w_q.reshape(D_MODEL, D_MODEL), w_qkv[:, D_MODEL:]], axis=1).astype(BF16)
    bias, sink = _swa_row_tables(table, sinks)

    q_p, k_p, v_p, kb_p, vb_p = _swa_proj_call(xs[0], w)
    pad = ((0, 0), (WINDOW, 0), (0, 0))
    o_p = _swa_attn_call(q_p.reshape(BATCH, SEQ, D_MODEL),
                         jnp.pad(kb_p.reshape(BATCH, SEQ, kvd), pad),
                         jnp.pad(vb_p.reshape(BATCH, SEQ, kvd), pad),
                         bias, sink, n_seq=BATCH, seq_len=SEQ, tile=SWA_TILE, n_invalid=WINDOW)

    q_s, k_s, v_s, kb_s, vb_s = _swa_proj_call(xs[1], w)
    kc = k_cache.reshape(DEC_BATCH, WINDOW, kvd)
    vc = v_cache.reshape(DEC_BATCH, WINDOW, kvd)
    o_s = _swa_attn_call(q_s.reshape(DEC_BATCH, DEC_SEQ, D_MODEL),
                         jnp.concatenate([kc.astype(BF16), kb_s.reshape(DEC_BATCH, DEC_SEQ, kvd)], axis=1),
                         jnp.concatenate([vc.astype(BF16), vb_s.reshape(DEC_BATCH, DEC_SEQ, kvd)], axis=1),
                         bias, sink, n_seq=DEC_BATCH, seq_len=DEC_SEQ, tile=DEC_SEQ, n_invalid=0)

    def tails(new_p, new_s, cache):
        tail_p = new_p.reshape(BATCH, SEQ, kvd)[:, SEQ - WINDOW:]
        tail_s = jnp.concatenate([cache.reshape(DEC_BATCH, WINDOW, kvd)[:, DEC_SEQ:],
                                  new_s.reshape(DEC_BATCH, DEC_SEQ, kvd)], axis=1)
        return (tail_p.reshape(BATCH, WINDOW, SWA_KV_HEADS, HEAD_DIM),
                tail_s.reshape(DEC_BATCH, WINDOW, SWA_KV_HEADS, HEAD_DIM))

    k_tp, k_ts = tails(k_p, k_s, k_cache)
    v_tp, v_ts = tails(v_p, v_s, v_cache)
    return ((o_p.reshape(N_PROMPT, D_MODEL), o_s.reshape(N_SAMPLE, D_MODEL)),
            (k_tp, k_ts, v_tp, v_ts))


def _fox_layer(xs, k_cache, v_cache, logf_cache, w_in, b_f):
    hd = N_HEADS * HEAD_DIM
    def thrice(a):
        rep = jnp.concatenate([a, a, a], axis=-1)
        return jnp.pad(rep, [(0, 0)] * (a.ndim - 1) + [(0, LANES - 3 * N_HEADS)])

    w_q = w_in[:, :hd] * ATTN_SCALE
    w_f = thrice(w_in[:, 3 * hd:])
    w_qkf = jnp.concatenate([w_q, w_in[:, hd:2 * hd], w_f], axis=1).astype(BF16)
    w = jnp.concatenate([w_q, w_in[:, hd:3 * hd], w_f], axis=1).astype(BF16)
    bf = thrice(b_f.astype(F32)).reshape(1, LANES)
    w_vt = w_in[:, 2 * hd:3 * hd].T.astype(BF16)

    q_p, kt_p, vt32_p, kb_p, vt_p, lf_p = _fox_proj_call(xs[0], w_qkf, w_vt, bf, seq_len=SEQ,
                                                         time_minor=True)
    lf_p = lf_p.reshape(BATCH, SEQ, LANES)
    bq_p, bk_p = _fox_bias_call(lf_p)
    o_p = _fox_attn_call(q_p, bq_p, kb_p, bk_p, vt_p, n_seq=BATCH, seq_len=SEQ, tq=FOX_TQ, tk=FOX_TK)

    q_s, k_s, v_s, kb_s, vt_s, lf_s = _fox_proj_call(xs[1], w, w_vt, bf, seq_len=DEC_SEQ,
                                                     time_minor=False)
    lf_s = lf_s.reshape(DEC_BATCH, DEC_SEQ, LANES)
    tail = FOX_TK - DEC_SEQ
    bq_s, bk_s = _fox_bias_call(jnp.pad(jnp.concatenate([thrice(logf_cache.astype(F32)), lf_s], axis=1),
                                        ((0, 0), (0, tail), (0, 0))))
    k_new = jnp.pad(kb_s.reshape(N_SLABS, DEC_BATCH, DEC_SEQ, LANES), ((0, 0), (0, 0), (0, tail), (0, 0)))
    k_new = k_new.reshape(N_SLABS, DEC_BATCH * FOX_TK, LANES)

    def time_minor(cache):
        return cache.astype(F32).reshape(DEC_BATCH, PAST_LEN, N_SLABS, LANES).transpose(0, 2, 3, 1)

    kt_old = time_minor(k_cache)
    vt_old = time_minor(v_cache)
    per_blk = FOX_TK // DEC_SEQ
    vt_new = vt_s.reshape(N_SLABS, N_SAMPLE // FOX_TK, LANES, per_blk, DEC_SEQ)
    vt_new = vt_new.transpose(0, 1, 3, 2, 4).reshape(N_SLABS, DEC_BATCH, LANES, DEC_SEQ)
    vt_new = jnp.pad(vt_new, ((0, 0), (0, 0), (0, 0), (0, tail)))
    o_s = _fox_step_attn_call(q_s, bq_s, kt_old, k_new, bk_s, vt_old, vt_new, n_seq=DEC_BATCH,
                              n_new=DEC_SEQ, n_old=PAST_LEN, tk=FOX_TK)

    def heads_last(a):
        return a.reshape(BATCH, N_HEADS, HEAD_DIM, SEQ).transpose(0, 3, 1, 2)

    outs = (heads_last(kt_p), k_s.reshape(DEC_BATCH, DEC_SEQ, N_HEADS, HEAD_DIM),
            heads_last(vt32_p), v_s.reshape(DEC_BATCH, DEC_SEQ, N_HEADS, HEAD_DIM),
            lf_p[..., :N_HEADS], lf_s[..., :N_HEADS])
    return (o_p, o_s), outs


def kernel(x_prompt, x_sample, state_rg_conv, state_rg_h, cache_swa_k, cache_swa_v, cache_fox_k, cache_fox_v, cache_fox_logf, ln_gain, ln_bias, ffn_w_up, ffn_w_down, rg_w_in, rg_conv_w, rg_conv_b, rg_gate_w, rg_gate_b, rg_lambda, rg_w_out, swa_w_qkv, swa_sinks, swa_w_out, rel_bias_table, fox_w_in, fox_b_f, fox_w_out):
    xs = (x_prompt.reshape(N_PROMPT, D_MODEL), x_sample.reshape(N_SAMPLE, D_MODEL))
    w_up = ffn_w_up.astype(BF16)
    w_down = ffn_w_down.astype(BF16)
    rg_in = rg_w_in.astype(BF16)
    rg_gate = rg_gate_w.astype(BF16)
    swa_order = jnp.array(_SWA_HEAD_ORDER)
    n_swa = swa_w_out.shape[0]
    w_out = {0: rg_w_out.astype(BF16),
             1: swa_w_out.reshape(n_swa, N_HEADS, HEAD_DIM, D_MODEL)[:, swa_order]
                         .reshape(n_swa, D_MODEL, D_MODEL).astype(BF16),
             2: fox_w_out.astype(BF16)}
    rg_out, swa_out, fox_out = [], [], []
    for i in range(DEPTH):
        kind, j = i % 3, i // 3
        if kind == 0:
            small = (rg_conv_w[j], rg_conv_b[j].reshape(1, LRU_WIDTH),
                     rg_gate_b[j].reshape(LRU_BLOCKS, 1, 2 * LRU_BLOCK_W), rg_lambda[j].reshape(1, LRU_WIDTH))
            os_, extra = _rg_layer(xs, state_rg_conv[j], state_rg_h[j], rg_in, rg_gate, small, j)
            rg_out.append(extra)
        elif kind == 1:
            os_, extra = _swa_layer(xs, cache_swa_k[j], cache_swa_v[j], swa_w_qkv[j], swa_sinks[j],
                                    rel_bias_table)
            swa_out.append(extra)
        else:
            os_, extra = _fox_layer(xs, cache_fox_k[j], cache_fox_v[j], cache_fox_logf[j], fox_w_in[j],
                                    fox_b_f[j])
            fox_out.append(extra)
        xs = tuple(_post_call(x, o, w_out[kind], j, w_up, w_down, ln_gain, ln_bias, i)
                   for x, o in zip(xs, os_))

    def stack(items, idx):
        return jnp.stack([it[idx] for it in items])

    return (xs[0].reshape(BATCH, SEQ, D_MODEL), xs[1].reshape(DEC_BATCH, DEC_SEQ, D_MODEL),
            stack(rg_out, 0), stack(rg_out, 1), stack(rg_out, 2), stack(rg_out, 3),
            stack(swa_out, 0), stack(swa_out, 1), stack(swa_out, 2), stack(swa_out, 3),
            stack(fox_out, 0), stack(fox_out, 1), stack(fox_out, 2), stack(fox_out, 3),
            stack(fox_out, 4), stack(fox_out, 5))
```

```python
import functools
import math

import jax
import jax.numpy as jnp
from jax import lax
from jax.experimental import pallas as pl
from jax.experimental.pallas import tpu as pltpu

F32 = jnp.float32
BF16 = jnp.bfloat16

D_MODEL = 1024
BATCH = 4
SEQ = 4096
DEPTH = 4
DEC_BATCH = 16
DEC_SEQ = 64
PAST_LEN = 1024
CHUNK = 64
D_FF = 4 * D_MODEL
HEAD_DIM = 64
N_HEADS = 16
SWA_KV_HEADS = 4
SWA_GROUP = 4
WINDOW = 128
LRU_WIDTH = D_MODEL
LRU_BLOCKS = 4
LRU_BLOCK_W = LRU_WIDTH // LRU_BLOCKS
CONV_WIDTH = 4
LRU_C = 8.0
N_BUCKETS = 32
MAX_DISTANCE = 128
ALPHA = (2.0 * DEPTH) ** 0.25
LN_EPS = 1e-5
ATTN_SCALE = HEAD_DIM ** -0.5

N_PROMPT = BATCH * SEQ
N_SAMPLE = DEC_BATCH * DEC_SEQ

LANES = 128
HEADS_PER_SLAB = LANES // HEAD_DIM
N_SLABS = D_MODEL // LANES

TOKEN_TILE = 512
POST_TILE = 1024
POST_ROWS = 256
FF_CHUNK = 1024
RG_TILE = 512
RG_PROJ_CHUNKS = 8
SWA_TILE = 512
SWA_KEYS = WINDOW + CHUNK
FOX_TQ = 512
FOX_TK = 512
FOX_ONES_ROWS = 16
BIAS_TILE = 512
VMEM_LIMIT = 56 * 1024 * 1024


def _resident(shape, layer=None):
    zeros = (0,) * len(shape)
    if layer is None:
        return pl.BlockSpec(shape, lambda *_: zeros, pipeline_mode=pl.Buffered(1))
    return pl.BlockSpec((None,) + tuple(shape), lambda *_: (layer,) + zeros,
                        pipeline_mode=pl.Buffered(1))


def _params(semantics):
    return pltpu.CompilerParams(dimension_semantics=semantics, vmem_limit_bytes=VMEM_LIMIT)


def _softplus(x):
    return jnp.maximum(x, 0.0) + jnp.log1p(jnp.exp(-jnp.abs(x)))


def _layer_norm(z, g, b):
    mu = jnp.mean(z, axis=-1, keepdims=True)
    zc = z - mu
    var = jnp.mean(zc * zc, axis=-1, keepdims=True)
    return zc * lax.rsqrt(var + LN_EPS) * g + b


def _post_kernel(x_ref, o_ref, wout_ref, wup_ref, wdn_ref, g_ref, b_ref, out_ref, *, slab_major):
    groups = [slice(r * POST_ROWS, (r + 1) * POST_ROWS) for r in range(POST_TILE // POST_ROWS)]
    y = []
    for rows in groups:
        if slab_major:
            o = jnp.concatenate([o_ref[s, rows, :] for s in range(N_SLABS)], axis=1)
        else:
            o = o_ref[rows, :]
        y.append(jnp.dot(o, wout_ref[...], preferred_element_type=F32))
    x1 = [_layer_norm(ALPHA * x_ref[rows, :] + y_r, g_ref[0:1, :], b_ref[0:1, :])
          for rows, y_r in zip(groups, y)]
    for rows, x1_r in zip(groups, x1):
        x1b = x1_r.astype(BF16)
        acc = jnp.zeros_like(x1_r)
        for c in range(D_FF // FF_CHUNK):
            cols = slice(c * FF_CHUNK, (c + 1) * FF_CHUNK)
            h = jnp.dot(x1b, wup_ref[:, cols], preferred_element_type=F32)
            a = jnp.square(jnp.maximum(h, 0.0)).astype(BF16)
            acc = acc + jnp.dot(a, wdn_ref[cols, :], preferred_element_type=F32)
        out_ref[rows, :] = _layer_norm(ALPHA * x1_r + acc, g_ref[1:2, :], b_ref[1:2, :])


def _post_call(x, o, w_out, mixer_idx, w_up, w_down, gain, bias, layer):
    n = x.shape[0]
    tile = pl.BlockSpec((POST_TILE, D_MODEL), lambda i: (i, 0))
    slab_major = o.ndim == 3
    o_tile = pl.BlockSpec((N_SLABS, POST_TILE, LANES), lambda i: (0, i, 0)) if slab_major else tile
    return pl.pallas_call(
        functools.partial(_post_kernel, slab_major=slab_major),
        out_shape=jax.ShapeDtypeStruct((n, D_MODEL), F32),
        grid=(n // POST_TILE,),
        in_specs=[tile, o_tile, _resident((D_MODEL, D_MODEL), mixer_idx),
                  _resident((D_MODEL, D_FF), layer), _resident((D_FF, D_MODEL), layer),
                  _resident((2, D_MODEL), layer), _resident((2, D_MODEL), layer)],
        out_specs=tile,
        compiler_params=_params(("parallel",)),
        name="post",
    )(x, o, w_out, w_up, w_down, gain, bias)


def _run(*staged, order=None):
    live = {i: iter(s) for i, s in enumerate(staged)}
    plan = list(order or [])
    while live:
        turn = plan.pop(0) if plan else None
        for i in ([turn] if turn in live else list(live)):
            try:
                next(live[i])
            except StopIteration:
                del live[i]


def _rg_project(x_ref, perm_ref, win_ref, gu_ref):
    xb = x_ref[...].astype(BF16)
    xb = jnp.dot(perm_ref[...], xb, preferred_element_type=F32).astype(BF16)
    yield
    width = 2 * LRU_WIDTH // RG_PROJ_CHUNKS
    for c in range(RG_PROJ_CHUNKS):
        cols = slice(c * width, (c + 1) * width)
        gu_ref[:, cols] = jnp.dot(xb, win_ref[:, cols], preferred_element_type=F32)
        yield


def _rg_mix(gu_ref, unperm_ref, cw_ref, cb_ref, gw_ref, gb_ref, lam_ref, o_ref, nbuf_ref, rows,
            scratch, *, tile):
    u_scr, gg_scr, h_scr, ac_scr, hist_scr, hc_scr = scratch
    hist = CONV_WIDTH - 1
    seg = tile // 8
    lead = 8 * hist

    u = gu_ref[:, LRU_WIDTH:]
    u_scr[lead:lead + tile, :] = u
    gg_scr[...] = jax.nn.gelu(gu_ref[:, :LRU_WIDTH])
    yield
    sub = lax.broadcasted_iota(jnp.int32, (8, LRU_WIDTH), 0)
    for d in range(1, hist + 1):
        prev_tail = pltpu.roll(u_scr[lead + 8 * (seg - d):lead + 8 * (seg - d) + 8, :], 1, 0)
        u_scr[lead - 8 * d:lead - 8 * d + 8, :] = jnp.where(sub == 0, hist_scr[8 - d:8 - d + 1, :],
                                                           prev_tail)
    conv = cb_ref[...] + u * cw_ref[hist:hist + 1, :]
    for k in range(hist):
        conv = conv + u_scr[8 * k:8 * k + tile, :] * cw_ref[k:k + 1, :]
    for d in range(1, hist + 1):
        last = u_scr[lead + 8 * (seg - d) + 7:lead + 8 * (seg - d) + 8, :]
        hist_scr[8 - d:8 - d + 1, :] = last
        nbuf_ref[0, hist - d:hist - d + 1, :] = last
    yield

    sub_blk = lax.broadcasted_iota(jnp.int32, (8, LRU_BLOCK_W), 0)
    for n in range(LRU_BLOCKS):
        cols = slice(n * LRU_BLOCK_W, (n + 1) * LRU_BLOCK_W)
        cn = conv[:, cols]
        g = jnp.dot(cn.astype(BF16), gw_ref[n], preferred_element_type=F32) + gb_ref[n]
        r = jax.nn.sigmoid(g[:, :LRU_BLOCK_W])
        ig = jax.nn.sigmoid(g[:, LRU_BLOCK_W:])
        log_a = -LRU_C * r * _softplus(-lam_ref[:, cols])
        a_blk = jnp.exp(log_a)
        var = 1.0 - a_blk * a_blk
        root = jnp.where(var > 0.0, var * lax.rsqrt(var), 0.0)
        b_blk = root * (ig * cn)
        h = jnp.zeros((8, LRU_BLOCK_W), F32)
        prod = jnp.ones((8, LRU_BLOCK_W), F32)
        for j in range(seg):
            a = a_blk[8 * j:8 * j + 8, :]
            h = a * h + b_blk[8 * j:8 * j + 8, :]
            prod = a * prod
            h_scr[8 * j:8 * j + 8, cols] = h
            ac_scr[8 * j:8 * j + 8, cols] = prod
        carry = jnp.where(sub_blk == 0, hc_scr[:, cols], 0.0)
        for i in range(1, 8):
            carry = jnp.where(sub_blk == i, pltpu.roll(prod * carry + h, 1, 0), carry)
        hc_scr[:, cols] = (prod * carry + h)[7:8, :]
        for j in range(seg):
            h_scr[8 * j:8 * j + 8, cols] = (h_scr[8 * j:8 * j + 8, cols]
                                            + ac_scr[8 * j:8 * j + 8, cols] * carry)
        yield

    o = (h_scr[...] * gg_scr[...]).astype(BF16)
    o_ref[rows, :] = jnp.dot(unperm_ref[...], o, preferred_element_type=F32).astype(BF16)


def _rg_kernel(*refs, tile, pipelined):
    n_x = 3 if pipelined else 1
    x_refs = refs[:n_x]
    (cbuf_ref, h0_ref, perm_ref, unperm_ref, win_ref, cw_ref, cb_ref, gw_ref, gb_ref, lam_ref,
     o_ref, nbuf_ref, hlast_ref, gu_a, gu_b) = refs[n_x:n_x + 15]
    scratch = refs[n_x + 15:]
    hist_scr, hc_scr = scratch[-2:]
    hist = CONV_WIDTH - 1

    @pl.when(pl.program_id(1) == 0)
    def _():
        hist_scr[8 - hist:8, :] = cbuf_ref[0]
        hc_scr[...] = h0_ref[0]
        _run(_rg_project(x_refs[0], perm_ref, win_ref, gu_a))

    mix = functools.partial(_rg_mix, unperm_ref=unperm_ref, cw_ref=cw_ref, cb_ref=cb_ref, gw_ref=gw_ref,
                            gb_ref=gb_ref, lam_ref=lam_ref, o_ref=o_ref, nbuf_ref=nbuf_ref,
                            scratch=scratch, tile=tile)
    if pipelined:
        order = [1, 0, 1, 1, 0, 0, 1, 0, 0, 1, 0, 0, 1, 0, 0, 1]
        _run(_rg_project(x_refs[1], perm_ref, win_ref, gu_b), mix(gu_a, rows=slice(0, tile)),
             order=order)
        _run(_rg_project(x_refs[2], perm_ref, win_ref, gu_a), mix(gu_b, rows=slice(tile, 2 * tile)),
             order=order)
    else:
        _run(mix(gu_a, rows=slice(0, tile)))
    hlast_ref[0] = hc_scr[...]


def _segment_interleave(tile):
    seg = tile // 8
    src = [(p % 8) * seg + p // 8 for p in range(tile)]
    return jnp.zeros((tile, tile), BF16).at[jnp.arange(tile), jnp.array(src)].set(1.0)


def _rg_call(x, conv_buf, h0, w_in, gate_w, small, layer, *, n_seq, seq_len, tile):
    conv_w, conv_b, gate_b, lam = small
    n_t = seq_len // tile
    pipelined = n_t >= 2 and n_t % 2 == 0
    tiles_per_step = 2 if pipelined else 1
    n_steps = n_t // tiles_per_step
    perm = _segment_interleave(tile)
    if pipelined:
        x_specs = [pl.BlockSpec((tile, D_MODEL), lambda b, s: (b * n_t, 0)),
                   pl.BlockSpec((tile, D_MODEL), lambda b, s: (b * n_t + 2 * s + 1, 0)),
                   pl.BlockSpec((tile, D_MODEL),
                                lambda b, s: (b * n_t + jnp.minimum(2 * s + 2, n_t - 1), 0))]
    else:
        x_specs = [pl.BlockSpec((tile, D_MODEL), lambda b, s: (b * n_t + s, 0))]
    kern = functools.partial(_rg_kernel, tile=tile, pipelined=pipelined)
    o, nbuf, hlast = pl.pallas_call(
        kern,
        out_shape=(jax.ShapeDtypeStruct((n_seq * seq_len, D_MODEL), BF16),
                   jax.ShapeDtypeStruct((n_seq, CONV_WIDTH - 1, LRU_WIDTH), F32),
                   jax.ShapeDtypeStruct((n_seq, 1, LRU_WIDTH), F32)),
        grid=(n_seq, n_steps),
        in_specs=x_specs + [
            pl.BlockSpec((1, CONV_WIDTH - 1, LRU_WIDTH), lambda b, s: (b, 0, 0)),
            pl.BlockSpec((1, 1, LRU_WIDTH), lambda b, s: (b, 0, 0)),
            _resident((tile, tile)), _resident((tile, tile)),
            _resident((D_MODEL, 2 * LRU_WIDTH), layer),
            _resident((CONV_WIDTH, LRU_WIDTH)),
            _resident((1, LRU_WIDTH)),
            _resident((LRU_BLOCKS, LRU_BLOCK_W, 2 * LRU_BLOCK_W), layer),
            _resident((LRU_BLOCKS, 1, 2 * LRU_BLOCK_W)),
            _resident((1, LRU_WIDTH))],
        out_specs=(pl.BlockSpec((tiles_per_step * tile, D_MODEL), lambda b, s: (b * n_steps + s, 0)),
                   pl.BlockSpec((1, CONV_WIDTH - 1, LRU_WIDTH), lambda b, s: (b, 0, 0)),
                   pl.BlockSpec((1, 1, LRU_WIDTH), lambda b, s: (b, 0, 0))),
        scratch_shapes=[pltpu.VMEM((tile, 2 * LRU_WIDTH), F32),
                        pltpu.VMEM((tile, 2 * LRU_WIDTH), F32),
                        pltpu.VMEM((tile + 8 * (CONV_WIDTH - 1), LRU_WIDTH), F32),
                        pltpu.VMEM((tile, LRU_WIDTH), F32),
                        pltpu.VMEM((tile, LRU_WIDTH), F32),
                        pltpu.VMEM((tile, LRU_WIDTH), F32),
                        pltpu.VMEM((8, LRU_WIDTH), F32),
                        pltpu.VMEM((1, LRU_WIDTH), F32)],
        compiler_params=_params(("parallel", "arbitrary")),
        name="rg_mixer",
    )(*([x] * len(x_specs)), conv_buf, h0, perm, perm.T, w_in, conv_w, conv_b, gate_w, gate_b, lam)
    return o, nbuf, hlast[:, 0, :]


def _swa_proj_kernel(x_ref, w_ref, q_ref, k_ref, v_ref, kb_ref, vb_ref):
    xb = x_ref[...].astype(BF16)
    qkv = jnp.dot(xb, w_ref[...], preferred_element_type=F32)
    kvd = SWA_KV_HEADS * HEAD_DIM
    q_ref[...] = qkv[:, :D_MODEL].astype(BF16)
    k = qkv[:, D_MODEL:D_MODEL + kvd]
    v = qkv[:, D_MODEL + kvd:]
    k_ref[...] = k
    v_ref[...] = v
    kb_ref[...] = k.astype(BF16)
    vb_ref[...] = v.astype(BF16)


def _swa_proj_call(x, w):
    n = x.shape[0]
    kvd = SWA_KV_HEADS * HEAD_DIM
    row = lambda i: (i, 0)
    return pl.pallas_call(
        _swa_proj_kernel,
        out_shape=(jax.ShapeDtypeStruct((n, D_MODEL), BF16),
                   jax.ShapeDtypeStruct((n, kvd), F32),
                   jax.ShapeDtypeStruct((n, kvd), F32),
                   jax.ShapeDtypeStruct((n, kvd), BF16),
                   jax.ShapeDtypeStruct((n, kvd), BF16)),
        grid=(n // TOKEN_TILE,),
        in_specs=[pl.BlockSpec((TOKEN_TILE, D_MODEL), row), _resident((D_MODEL, D_MODEL + 2 * kvd))],
        out_specs=(pl.BlockSpec((TOKEN_TILE, D_MODEL), row),
                   pl.BlockSpec((TOKEN_TILE, kvd), row), pl.BlockSpec((TOKEN_TILE, kvd), row),
                   pl.BlockSpec((TOKEN_TILE, kvd), row), pl.BlockSpec((TOKEN_TILE, kvd), row)),
        compiler_params=_params(("parallel",)),
        name="swa_proj",
    )(x, w)


def _swa_attn_kernel(q_ref, k_ref, v_ref, bias_ref, sink_ref, o_ref, s_even, s_odd,
                     *, tile, n_invalid):
    ti = pl.program_id(1)
    cols = SWA_GROUP * HEADS_PER_SLAB * CHUNK
    half_cols = cols // HEADS_PER_SLAB
    lane_q = lax.broadcasted_iota(jnp.int32, (CHUNK, LANES), 1)
    low_q = lane_q < HEAD_DIM
    key_idx = lax.broadcasted_iota(jnp.int32, (SWA_KEYS, cols), 0)
    n_kv_slabs = SWA_KV_HEADS // HEADS_PER_SLAB
    units = [(c, t) for c in range(tile // CHUNK) for t in range(n_kv_slabs)]
    s_bufs = (s_even, s_odd)

    def window_start(c):
        return pl.multiple_of(ti * tile + c * CHUNK, CHUNK)

    def scores(u):
        c, t = units[u]
        row0 = window_start(c)
        kw = k_ref[0, pl.ds(row0, SWA_KEYS), t * LANES:(t + 1) * LANES]
        slabs = [q_ref[0, c * CHUNK:(c + 1) * CHUNK,
                       (SWA_GROUP * t + g) * LANES:(SWA_GROUP * t + g + 1) * LANES]
                 for g in range(SWA_GROUP)]
        zero = jnp.zeros_like(slabs[0])
        q_stack = jnp.concatenate([jnp.where(low_q, s_, zero) for s_ in slabs]
                                  + [jnp.where(low_q, zero, s_) for s_ in slabs], axis=0)
        s = lax.dot_general(kw, q_stack, (((1,), (1,)), ((), ())), preferred_element_type=F32)
        s = s + bias_ref[t]
        if n_invalid > 0 and c * CHUNK < n_invalid:
            s = jnp.where(key_idx + row0 >= n_invalid, s, -jnp.inf)
        s_bufs[u % 2][...] = s

    def attend(u):
        c, t = units[u]
        s = s_bufs[u % 2][...]
        vw = v_ref[0, pl.ds(window_start(c), SWA_KEYS), t * LANES:(t + 1) * LANES]
        sink = sink_ref[t]
        m = jnp.maximum(jnp.max(s, axis=0, keepdims=True), sink)
        p = jnp.exp(s - m)
        denom = jnp.sum(p, axis=0, keepdims=True) + jnp.exp(sink - m)
        out_t = lax.dot_general(vw, p.astype(BF16), (((0,), (0,)), ((), ())),
                                preferred_element_type=F32) / denom
        pair = jnp.concatenate([out_t[:HEAD_DIM, :half_cols], out_t[HEAD_DIM:, half_cols:]], axis=0)
        pair = pair.T.astype(BF16)
        for g in range(SWA_GROUP):
            s_idx = SWA_GROUP * t + g
            o_ref[0, c * CHUNK:(c + 1) * CHUNK, s_idx * LANES:(s_idx + 1) * LANES] = (
                pair[g * CHUNK:(g + 1) * CHUNK, :])

    scores(0)
    for u in range(len(units)):
        if u + 1 < len(units):
            scores(u + 1)
        attend(u)


def _swa_attn_call(q, k_win, v_win, bias, sink, *, n_seq, seq_len, tile, n_invalid):
    kvd = SWA_KV_HEADS * HEAD_DIM
    n_keys = k_win.shape[1]
    cols = SWA_GROUP * HEADS_PER_SLAB * CHUNK
    n_kv_slabs = SWA_KV_HEADS // HEADS_PER_SLAB
    kern = functools.partial(_swa_attn_kernel, tile=tile, n_invalid=n_invalid)
    return pl.pallas_call(
        kern,
        out_shape=jax.ShapeDtypeStruct((n_seq, seq_len, D_MODEL), BF16),
        grid=(n_seq, seq_len // tile),
        in_specs=[pl.BlockSpec((1, tile, D_MODEL), lambda b, t: (b, t, 0)),
                  pl.BlockSpec((1, n_keys, kvd), lambda b, t: (b, 0, 0)),
                  pl.BlockSpec((1, n_keys, kvd), lambda b, t: (b, 0, 0)),
                  _resident((n_kv_slabs, SWA_KEYS, cols)),
                  _resident((n_kv_slabs, 1, cols))],
        out_specs=pl.BlockSpec((1, tile, D_MODEL), lambda b, t: (b, t, 0)),
        scratch_shapes=[pltpu.VMEM((SWA_KEYS, cols), F32), pltpu.VMEM((SWA_KEYS, cols), F32)],
        compiler_params=_params(("parallel", "arbitrary")),
        name="swa_attn",
    )(q, k_win, v_win, bias, sink)


def _t5_bucket(rel):
    half = N_BUCKETS // 2
    max_exact = half // 2
    n = jnp.abs(rel)
    n_f = jnp.maximum(n, 1).astype(jnp.float32)
    large = max_exact + (jnp.log(n_f / max_exact) / math.log(MAX_DISTANCE / max_exact)
                         * (half - max_exact)).astype(jnp.int32)
    large = jnp.minimum(large, half - 1)
    return jnp.where(rel > 0, half, 0) + jnp.where(n < max_exact, n, large)


_SWA_HEAD_ORDER = [8 * t + 4 * p + g for t in range(2) for g in range(4) for p in range(2)]


def _swa_row_tables(table, sinks):
    rel = jnp.arange(SWA_KEYS)[None, :] - WINDOW - jnp.arange(CHUNK)[:, None]
    bucket = _t5_bucket(rel)
    tbl = table.astype(F32)
    hit = bucket[None, None] == jnp.arange(N_BUCKETS)[:, None, None, None]
    bias = jnp.sum(jnp.where(hit, tbl[:, :, None, None], 0.0), axis=0)
    n_kv_slabs = SWA_KV_HEADS // HEADS_PER_SLAB
    cols = SWA_GROUP * HEADS_PER_SLAB * CHUNK
    bias = bias.reshape(n_kv_slabs, cols, SWA_KEYS).transpose(0, 2, 1)
    sink = jnp.repeat(sinks.astype(F32), CHUNK).reshape(n_kv_slabs, 1, cols)
    return bias, sink


def _fox_proj_kernel(x_ref, w_ref, wt_ref, bf_ref, q_ref, k_ref, v_ref, kb_ref, vt_ref, lf_ref,
                     *, time_minor):
    xb = x_ref[...].astype(BF16)
    proj = jnp.dot(xb, w_ref[...], preferred_element_type=F32)
    proj_t = lax.dot_general(wt_ref[...], xb, (((1,), (1,)), ((), ())), preferred_element_type=F32)
    if time_minor:
        k = proj[:, :D_MODEL]
        z = proj[:, D_MODEL:]
        vt = proj_t[D_MODEL:, :]
        k_ref[0] = k.T
        v_ref[0] = vt
    else:
        k = proj[:, D_MODEL:2 * D_MODEL]
        z = proj[:, 3 * D_MODEL:]
        vt = proj_t
        k_ref[...] = k.reshape(TOKEN_TILE, N_HEADS, HEAD_DIM)
        v_ref[...] = proj[:, 2 * D_MODEL:3 * D_MODEL].reshape(TOKEN_TILE, N_HEADS, HEAD_DIM)
    for s in range(N_SLABS):
        cols = slice(s * LANES, (s + 1) * LANES)
        q_ref[s] = (proj_t[cols, :] if time_minor else proj[:, cols]).astype(BF16)
        kb_ref[s] = k[:, cols].astype(BF16)
        for j in range(TOKEN_TILE // FOX_TK):
            vt_ref[s, j] = vt[cols, j * FOX_TK:(j + 1) * FOX_TK].astype(BF16)
    lf_ref[...] = -_softplus(-(z + bf_ref[...]))


def _fox_proj_call(x, w, w_t, b_f, *, seq_len, time_minor):
    n = x.shape[0]
    row = lambda i: (i, 0)
    wide = pl.BlockSpec((TOKEN_TILE, D_MODEL), row)
    slabs = pl.BlockSpec((N_SLABS, TOKEN_TILE, LANES), lambda i: (0, i, 0))
    kb_per_tile = TOKEN_TILE // FOX_TK
    if time_minor:
        t_per_seq = seq_len // TOKEN_TILE
        kv_shape = jax.ShapeDtypeStruct((n // seq_len, D_MODEL, seq_len), F32)
        kv_spec = pl.BlockSpec((1, D_MODEL, TOKEN_TILE), lambda i: (i // t_per_seq, 0, i % t_per_seq))
        q_shape = jax.ShapeDtypeStruct((N_SLABS, LANES, n), BF16)
        q_spec = pl.BlockSpec((N_SLABS, LANES, TOKEN_TILE), lambda i: (0, 0, i))
    else:
        kv_shape = jax.ShapeDtypeStruct((n, N_HEADS, HEAD_DIM), F32)
        kv_spec = pl.BlockSpec((TOKEN_TILE, N_HEADS, HEAD_DIM), lambda i: (i, 0, 0))
        q_shape = jax.ShapeDtypeStruct((N_SLABS, n, LANES), BF16)
        q_spec = slabs
    return pl.pallas_call(
        functools.partial(_fox_proj_kernel, time_minor=time_minor),
        out_shape=(q_shape, kv_shape, kv_shape,
                   jax.ShapeDtypeStruct((N_SLABS, n, LANES), BF16),
                   jax.ShapeDtypeStruct((N_SLABS, n // FOX_TK, LANES, FOX_TK), BF16),
                   jax.ShapeDtypeStruct((n, LANES), F32)),
        grid=(n // TOKEN_TILE,),
        in_specs=[wide, _resident(w.shape), _resident(w_t.shape), _resident((1, LANES))],
        out_specs=(q_spec, kv_spec, kv_spec, slabs,
                   pl.BlockSpec((N_SLABS, kb_per_tile, LANES, FOX_TK), lambda i: (0, i, 0, 0)),
                   pl.BlockSpec((TOKEN_TILE, LANES), row)),
        compiler_params=_params(("parallel",)),
        name="fox_proj",
    )(x, w, w_t, b_f)


def _lane_split3(x, lane):
    hi = x.astype(BF16)
    r1 = x - hi.astype(F32)
    mid = r1.astype(BF16)
    lo = (r1 - mid.astype(F32)).astype(BF16)
    return jnp.where(lane < N_HEADS, hi, jnp.where(lane < 2 * N_HEADS, mid, lo))


def _fox_bias_kernel(lf_ref, tri_ref, place_ref, const_ref, bq_ref, bk_ref, carry_scr):
    t = pl.program_id(1)

    @pl.when(t == 0)
    def _():
        carry_scr[...] = jnp.zeros_like(carry_scr)

    lane = lax.broadcasted_iota(jnp.int32, (BIAS_TILE, LANES), 1)
    sums = carry_scr[...] + jnp.dot(tri_ref[...], _lane_split3(lf_ref[0], lane),
                                    preferred_element_type=F32)
    carry_scr[...] = sums[BIAS_TILE - 1:BIAS_TILE, :]
    c = sums + pltpu.roll(sums, LANES - N_HEADS, 1) + pltpu.roll(sums, LANES - 2 * N_HEADS, 1)
    c = jnp.where(lane < N_HEADS, c, 0.0)
    c = c + pltpu.roll(c, N_HEADS, 1) + pltpu.roll(c, 2 * N_HEADS, 1)
    both = const_ref[...] + jnp.dot(_lane_split3(c, lane), place_ref[...],
                                    preferred_element_type=F32)
    bq_ref[0] = both[:, :LANES].astype(BF16)
    bk_ref[0] = both[:, LANES:].astype(BF16)


def _fox_bias_tables():
    place = [[0.0] * (2 * LANES) for _ in range(LANES)]
    const = [0.0] * (2 * LANES)
    for i in range(3):
        for h in range(N_HEADS):
            const[N_HEADS * i + h] = 1.0
            place[N_HEADS * i + h][N_HEADS * (3 + i) + h] = 1.0
            place[N_HEADS * i + h][LANES + N_HEADS * i + h] = -1.0
            const[LANES + N_HEADS * (3 + i) + h] = 1.0
    return jnp.array(place, BF16), jnp.array([const], F32)


def _fox_bias_call(logf):
    n_seq, n_rows, _ = logf.shape
    place, const = _fox_bias_tables()
    tri = jnp.tril(jnp.ones((BIAS_TILE, BIAS_TILE), BF16))
    blk = pl.BlockSpec((1, BIAS_TILE, LANES), lambda b, t: (b, t, 0))
    return pl.pallas_call(
        _fox_bias_kernel,
        out_shape=(jax.ShapeDtypeStruct((n_seq, n_rows, LANES), BF16),
                   jax.ShapeDtypeStruct((n_seq, n_rows, LANES), BF16)),
        grid=(n_seq, n_rows // BIAS_TILE),
        in_specs=[blk, _resident((BIAS_TILE, BIAS_TILE)), _resident((LANES, 2 * LANES)),
                  _resident((1, 2 * LANES))],
        out_specs=(blk, blk),
        scratch_shapes=[pltpu.VMEM((1, LANES), F32)],
        compiler_params=_params(("parallel", "arbitrary")),
        name="fox_bias",
    )(logf, tri, place, const)


def _fox_attn_core(q_ref, bq_ref, bk_ref, o_ref, scratch, *, tq, tk, first_q, q_time_minor, run_blocks):
    qaug_scr, m_scr, acc_scr, s_even, s_odd, smax_even, smax_odd = scratch
    k_off = lax.broadcasted_iota(jnp.int32, (tk, tq), 0)
    q_pos = first_q + lax.broadcasted_iota(jnp.int32, (tk, tq), 1)
    ones_rows = jnp.ones((FOX_ONES_ROWS, tk), BF16)

    feat = lax.broadcasted_iota(jnp.int32, (LANES, tq) if q_time_minor else (tq, LANES),
                                0 if q_time_minor else 1)
    low_feat = feat < HEAD_DIM
    bias_feat = feat < 6 * N_HEADS
    bq = bq_ref[0]
    if q_time_minor:
        bq = bq.astype(F32).T.astype(BF16)
    zero = jnp.zeros_like(bq)
    for head in range(N_HEADS):
        q_slab = q_ref[head // HEADS_PER_SLAB]
        own_half = low_feat if head % HEADS_PER_SLAB == 0 else jnp.logical_not(low_feat)
        sel = jnp.logical_and((feat & (N_HEADS - 1)) == head, bias_feat)
        qaug_scr[head] = jnp.concatenate([jnp.where(own_half, q_slab, zero), jnp.where(sel, bq, zero)],
                                         axis=0 if q_time_minor else 1)
    q_contract = 0 if q_time_minor else 1
    m_scr[...] = jnp.full_like(m_scr, -jnp.inf)
    acc_scr[...] = jnp.zeros_like(acc_scr)
    slots = ((s_even, smax_even), (s_odd, smax_odd))

    def key_block(kb, masked, k_of, vt_of):
        bk_blk = bk_ref[0, pl.ds(pl.multiple_of(kb * tk, tk), tk), :]

        def scores(hp):
            s_buf, smax_buf = slots[hp % 2]
            k_aug = jnp.concatenate([k_of(hp), bk_blk], axis=1)
            for par in range(HEADS_PER_SLAB):
                s = lax.dot_general(k_aug, qaug_scr[hp * HEADS_PER_SLAB + par],
                                    (((1,), (q_contract,)), ((), ())),
                                    preferred_element_type=F32)
                if masked:
                    s = jnp.where(kb * tk + k_off <= q_pos, s, -jnp.inf)
                s_buf[par] = s
                smax_buf[par] = jnp.max(s, axis=0, keepdims=True)

        def absorb(hp):
            s_buf, smax_buf = slots[hp % 2]
            vt = vt_of(hp)
            for par in range(HEADS_PER_SLAB):
                head = hp * HEADS_PER_SLAB + par
                m = m_scr[head]
                m_new = jnp.maximum(m, smax_buf[par])
                p = jnp.exp(s_buf[par] - m_new).astype(BF16)
                vt_aug = jnp.concatenate([vt[par * HEAD_DIM:(par + 1) * HEAD_DIM, :], ones_rows], axis=0)
                acc_scr[head] = (jnp.exp(m - m_new) * acc_scr[head]
                                 + jnp.dot(vt_aug, p, preferred_element_type=F32))
                m_scr[head] = m_new

        scores(0)
        for hp in range(N_SLABS):
            if hp + 1 < N_SLABS:
                scores(hp + 1)
            absorb(hp)

    run_blocks(key_block)

    for hp in range(N_SLABS):
        halves = []
        for par in range(HEADS_PER_SLAB):
            acc = acc_scr[hp * HEADS_PER_SLAB + par]
            halves.append(acc[:HEAD_DIM] / acc[HEAD_DIM:HEAD_DIM + 1])
        o_ref[hp] = jnp.concatenate(halves, axis=0).T.astype(BF16)


def _fox_attn_scratch(tq, tk, q_time_minor):
    return [pltpu.VMEM((N_HEADS, 2 * LANES, tq) if q_time_minor else (N_HEADS, tq, 2 * LANES), BF16),
            pltpu.VMEM((N_HEADS, 1, tq), F32),
            pltpu.VMEM((N_HEADS, HEAD_DIM + FOX_ONES_ROWS, tq), F32),
            pltpu.VMEM((HEADS_PER_SLAB, tk, tq), F32),
            pltpu.VMEM((HEADS_PER_SLAB, tk, tq), F32),
            pltpu.VMEM((HEADS_PER_SLAB, 1, tq), F32),
            pltpu.VMEM((HEADS_PER_SLAB, 1, tq), F32)]


def _fox_attn_kernel(q_ref, bq_ref, k_ref, bk_ref, vt_ref, o_ref, *scratch, tq, tk):
    first_q = pl.program_id(1) * tq
    n_full = first_q // tk

    def run_blocks(key_block):
        def block(kb, masked):
            r0 = pl.multiple_of(kb * tk, tk)
            key_block(kb, masked, lambda hp: k_ref[hp, pl.ds(r0, tk), :], lambda hp: vt_ref[hp, kb])

        def full_block(kb, carry):
            block(kb, False)
            return carry

        lax.fori_loop(0, n_full, full_block, 0)
        block(n_full, True)

    _fox_attn_core(q_ref, bq_ref, bk_ref, o_ref, scratch, tq=tq, tk=tk, first_q=first_q,
                   q_time_minor=True, run_blocks=run_blocks)


def _fox_attn_call(q, bq, k, bk, vt, *, n_seq, seq_len, tq, tk):
    assert tq <= tk and tk % tq == 0 and seq_len % tk == 0
    n_qt = seq_len // tq
    n_kb = seq_len // tk
    return pl.pallas_call(
        functools.partial(_fox_attn_kernel, tq=tq, tk=tk),
        out_shape=jax.ShapeDtypeStruct((N_SLABS, n_seq * seq_len, LANES), BF16),
        grid=(n_seq, n_qt),
        in_specs=[pl.BlockSpec((N_SLABS, LANES, tq), lambda b, i: (0, 0, b * n_qt + i)),
                  pl.BlockSpec((1, tq, LANES), lambda b, i: (b, i, 0)),
                  pl.BlockSpec((N_SLABS, seq_len, LANES), lambda b, i: (0, b, 0)),
                  pl.BlockSpec((1, seq_len, LANES), lambda b, i: (b, 0, 0)),
                  pl.BlockSpec((N_SLABS, n_kb, LANES, tk), lambda b, i: (0, b, 0, 0))],
        out_specs=pl.BlockSpec((N_SLABS, tq, LANES), lambda b, i: (0, b * n_qt + i, 0)),
        scratch_shapes=_fox_attn_scratch(tq, tk, True),
        compiler_params=_params(("parallel", "arbitrary")),
        name="fox_attn",
    )(q, bq, k, bk, vt)


def _fox_step_attn_kernel(q_ref, bq_ref, kold_ref, knew_ref, bk_ref, vtold_ref, vtnew_ref, o_ref,
                          *scratch, n_new, n_old, tk):
    def run_blocks(key_block):
        for kb in range(n_old // tk):
            rows = slice(kb * tk, (kb + 1) * tk)
            key_block(kb, False, lambda hp, rows=rows: kold_ref[0, hp, :, rows].T.astype(BF16),
                      lambda hp, rows=rows: vtold_ref[0, hp, :, rows].astype(BF16))
        key_block(n_old // tk, True, lambda hp: knew_ref[hp], lambda hp: vtnew_ref[hp, 0])

    _fox_attn_core(q_ref, bq_ref, bk_ref, o_ref, scratch, tq=n_new, tk=tk, first_q=n_old,
                   q_time_minor=False, run_blocks=run_blocks)


def _fox_step_attn_call(q, bq, kt_old, k_new, bk, vt_old, vt_new, *, n_seq, n_new, n_old, tk):
    assert n_new <= tk and n_old % tk == 0 and n_old % n_new == 0
    return pl.pallas_call(
        functools.partial(_fox_step_attn_kernel, n_new=n_new, n_old=n_old, tk=tk),
        out_shape=jax.ShapeDtypeStruct((N_SLABS, n_seq * n_new, LANES), BF16),
        grid=(n_seq,),
        in_specs=[pl.BlockSpec((N_SLABS, n_new, LANES), lambda b: (0, b, 0)),
                  pl.BlockSpec((1, n_new, LANES), lambda b: (b, n_old // n_new, 0)),
                  pl.BlockSpec((1, N_SLABS, LANES, n_old), lambda b: (b, 0, 0, 0)),
                  pl.BlockSpec((N_SLABS, tk, LANES), lambda b: (0, b, 0)),
                  pl.BlockSpec((1, n_old + tk, LANES), lambda b: (b, 0, 0)),
                  pl.BlockSpec((1, N_SLABS, LANES, n_old), lambda b: (b, 0, 0, 0)),
                  pl.BlockSpec((N_SLABS, 1, LANES, tk), lambda b: (0, b, 0, 0))],
        out_specs=pl.BlockSpec((N_SLABS, n_new, LANES), lambda b: (0, b, 0)),
        scratch_shapes=_fox_attn_scratch(n_new, tk, False),
        compiler_params=_params(("parallel",)),
        name="fox_step_attn",
    )(q, bq, kt_old, k_new, bk, vt_old, vt_new)


def _rg_layer(xs, conv_state, h_state, w_in, gate_w, small, layer):
    x_p, x_s = xs
    o_p, nb_p, h_p = _rg_call(
        x_p, jnp.zeros((BATCH, CONV_WIDTH - 1, LRU_WIDTH), F32), jnp.zeros((BATCH, 1, LRU_WIDTH), F32),
        w_in, gate_w, small, layer, n_seq=BATCH, seq_len=SEQ, tile=RG_TILE)
    o_s, nb_s, h_s = _rg_call(
        x_s, conv_state, h_state.reshape(DEC_BATCH, 1, LRU_WIDTH),
        w_in, gate_w, small, layer, n_seq=DEC_BATCH, seq_len=DEC_SEQ, tile=DEC_SEQ)
    return (o_p, o_s), (nb_p, nb_s, h_p, h_s)


def _swa_layer(xs, k_cache, v_cache, w_qkv, sinks, table):
    kvd = SWA_KV_HEADS * HEAD_DIM
    order = jnp.array(_SWA_HEAD_ORDER)
    w_q = (w_qkv[:, :D_MODEL] * ATTN_SCALE).reshape(D_MODEL, N_HEADS, HEAD_DIM)[:, order]
    w = jnp.concatenate([w_q.reshape(D_MODEL, D_MODEL), w_qkv[:, D_MODEL:]], axis=1).astype(BF16)
    bias, sink = _swa_row_tables(table, sinks)

    q_p, k_p, v_p, kb_p, vb_p = _swa_proj_call(xs[0], w)
    pad = ((0, 0), (WINDOW, 0), (0, 0))
    o_p = _swa_attn_call(q_p.reshape(BATCH, SEQ, D_MODEL),
                         jnp.pad(kb_p.reshape(BATCH, SEQ, kvd), pad),
                         jnp.pad(vb_p.reshape(BATCH, SEQ, kvd), pad),
                         bias, sink, n_seq=BATCH, seq_len=SEQ, tile=SWA_TILE, n_invalid=WINDOW)

    q_s, k_s, v_s, kb_s, vb_s = _swa_proj_call(xs[1], w)
    kc = k_cache.reshape(DEC_BATCH, WINDOW, kvd)
    vc = v_cache.reshape(DEC_BATCH, WINDOW, kvd)
    o_s = _swa_attn_call(q_s.reshape(DEC_BATCH, DEC_SEQ, D_MODEL),
                         jnp.concatenate([kc.astype(BF16), kb_s.reshape(DEC_BATCH, DEC_SEQ, kvd)], axis=1),
                         jnp.concatenate([vc.astype(BF16), vb_s.reshape(DEC_BATCH, DEC_SEQ, kvd)], axis=1),
                         bias, sink, n_seq=DEC_BATCH, seq_len=DEC_SEQ, tile=DEC_SEQ, n_invalid=0)

    def tails(new_p, new_s, cache):
        tail_p = new_p.reshape(BATCH, SEQ, kvd)[:, SEQ - WINDOW:]
        tail_s = jnp.concatenate([cache.reshape(DEC_BATCH, WINDOW, kvd)[:, DEC_SEQ:],
                                  new_s.reshape(DEC_BATCH, DEC_SEQ, kvd)], axis=1)
        return (tail_p.reshape(BATCH, WINDOW, SWA_KV_HEADS, HEAD_DIM),
                tail_s.reshape(DEC_BATCH, WINDOW, SWA_KV_HEADS, HEAD_DIM))

    k_tp, k_ts = tails(k_p, k_s, k_cache)
    v_tp, v_ts = tails(v_p, v_s, v_cache)
    return ((o_p.reshape(N_PROMPT, D_MODEL), o_s.reshape(N_SAMPLE, D_MODEL)),
            (k_tp, k_ts, v_tp, v_ts))


def _fox_layer(xs, k_cache, v_cache, logf_cache, w_in, b_f):
    hd = N_HEADS * HEAD_DIM
    def thrice(a):
        rep = jnp.concatenate([a, a, a], axis=-1)
        return jnp.pad(rep, [(0, 0)] * (a.ndim - 1) + [(0, LANES - 3 * N_HEADS)])

    w_q = w_in[:, :hd] * ATTN_SCALE
    w_f = thrice(w_in[:, 3 * hd:])
    w_kf = jnp.concatenate([w_in[:, hd:2 * hd], w_f], axis=1).astype(BF16)
    w = jnp.concatenate([w_q, w_in[:, hd:3 * hd], w_f], axis=1).astype(BF16)
    bf = thrice(b_f.astype(F32)).reshape(1, LANES)
    w_vt = w_in[:, 2 * hd:3 * hd].T.astype(BF16)
    w_qvt = jnp.concatenate([w_q.T.astype(BF16), w_vt], axis=0)

    q_p, kt_p, vt32_p, kb_p, vt_p, lf_p = _fox_proj_call(xs[0], w_kf, w_qvt, bf, seq_len=SEQ,
                                                         time_minor=True)
    lf_p = lf_p.reshape(BATCH, SEQ, LANES)
    bq_p, bk_p = _fox_bias_call(lf_p)
    o_p = _fox_attn_call(q_p, bq_p, kb_p, bk_p, vt_p, n_seq=BATCH, seq_len=SEQ, tq=FOX_TQ, tk=FOX_TK)

    q_s, k_s, v_s, kb_s, vt_s, lf_s = _fox_proj_call(xs[1], w, w_vt, bf, seq_len=DEC_SEQ,
                                                     time_minor=False)
    lf_s = lf_s.reshape(DEC_BATCH, DEC_SEQ, LANES)
    tail = FOX_TK - DEC_SEQ
    bq_s, bk_s = _fox_bias_call(jnp.pad(jnp.concatenate([thrice(logf_cache.astype(F32)), lf_s], axis=1),
                                        ((0, 0), (0, tail), (0, 0))))
    k_new = jnp.pad(kb_s.reshape(N_SLABS, DEC_BATCH, DEC_SEQ, LANES), ((0, 0), (0, 0), (0, tail), (0, 0)))
    k_new = k_new.reshape(N_SLABS, DEC_BATCH * FOX_TK, LANES)

    def time_minor(cache):
        return cache.astype(F32).reshape(DEC_BATCH, PAST_LEN, N_SLABS, LANES).transpose(0, 2, 3, 1)

    kt_old = time_minor(k_cache)
    vt_old = time_minor(v_cache)
    per_blk = FOX_TK // DEC_SEQ
    vt_new = vt_s.reshape(N_SLABS, N_SAMPLE // FOX_TK, LANES, per_blk, DEC_SEQ)
    vt_new = vt_new.transpose(0, 1, 3, 2, 4).reshape(N_SLABS, DEC_BATCH, LANES, DEC_SEQ)
    vt_new = jnp.pad(vt_new, ((0, 0), (0, 0), (0, 0), (0, tail)))
    o_s = _fox_step_attn_call(q_s, bq_s, kt_old, k_new, bk_s, vt_old, vt_new, n_seq=DEC_BATCH,
                              n_new=DEC_SEQ, n_old=PAST_LEN, tk=FOX_TK)

    def heads_last(a):
        return a.reshape(BATCH, N_HEADS, HEAD_DIM, SEQ).transpose(0, 3, 1, 2)

    outs = (heads_last(kt_p), k_s.reshape(DEC_BATCH, DEC_SEQ, N_HEADS, HEAD_DIM),
            heads_last(vt32_p), v_s.reshape(DEC_BATCH, DEC_SEQ, N_HEADS, HEAD_DIM),
            lf_p[..., :N_HEADS], lf_s[..., :N_HEADS])
    return (o_p, o_s), outs


def kernel(x_prompt, x_sample, state_rg_conv, state_rg_h, cache_swa_k, cache_swa_v, cache_fox_k, cache_fox_v, cache_fox_logf, ln_gain, ln_bias, ffn_w_up, ffn_w_down, rg_w_in, rg_conv_w, rg_conv_b, rg_gate_w, rg_gate_b, rg_lambda, rg_w_out, swa_w_qkv, swa_sinks, swa_w_out, rel_bias_table, fox_w_in, fox_b_f, fox_w_out):
    xs = (x_prompt.reshape(N_PROMPT, D_MODEL), x_sample.reshape(N_SAMPLE, D_MODEL))
    w_up = ffn_w_up.astype(BF16)
    w_down = ffn_w_down.astype(BF16)
    rg_in = rg_w_in.astype(BF16)
    rg_gate = rg_gate_w.astype(BF16)
    swa_order = jnp.array(_SWA_HEAD_ORDER)
    n_swa = swa_w_out.shape[0]
    w_out = {0: rg_w_out.astype(BF16),
             1: swa_w_out.reshape(n_swa, N_HEADS, HEAD_DIM, D_MODEL)[:, swa_order]
                         .reshape(n_swa, D_MODEL, D_MODEL).astype(BF16),
             2: fox_w_out.astype(BF16)}
    rg_out, swa_out, fox_out = [], [], []
    for i in range(DEPTH):
        kind, j = i % 3, i // 3
        if kind == 0:
            small = (rg_conv_w[j], rg_conv_b[j].reshape(1, LRU_WIDTH),
                     rg_gate_b[j].reshape(LRU_BLOCKS, 1, 2 * LRU_BLOCK_W), rg_lambda[j].reshape(1, LRU_WIDTH))
            os_, extra = _rg_layer(xs, state_rg_conv[j], state_rg_h[j], rg_in, rg_gate, small, j)
            rg_out.append(extra)
        elif kind == 1:
            os_, extra = _swa_layer(xs, cache_swa_k[j], cache_swa_v[j], swa_w_qkv[j], swa_sinks[j],
                                    rel_bias_table)
            swa_out.append(extra)
        else:
            os_, extra = _fox_layer(xs, cache_fox_k[j], cache_fox_v[j], cache_fox_logf[j], fox_w_in[j],
                                    fox_b_f[j])
            fox_out.append(extra)
        xs = tuple(_post_call(x, o, w_out[kind], j, w_up, w_down, ln_gain, ln_bias, i)
                   for x, o in zip(xs, os_))

    def stack(items, idx):
        return jnp.stack([it[idx] for it in items])

    return (xs[0].reshape(BATCH, SEQ, D_MODEL), xs[1].reshape(DEC_BATCH, DEC_SEQ, D_MODEL),
            stack(rg_out, 0), stack(rg_out, 1), stack(rg_out, 2), stack(rg_out, 3),
            stack(swa_out, 0), stack(swa_out, 1), stack(swa_out, 2), stack(swa_out, 3),
            stack(fox_out, 0), stack(fox_out, 1), stack(fox_out, 2), stack(fox_out, 3),
            stack(fox_out, 4), stack(fox_out, 5))
```

```python
import functools
import math

import jax
import jax.numpy as jnp
from jax import lax
from jax.experimental import pallas as pl
from jax.experimental.pallas import tpu as pltpu

F32 = jnp.float32
BF16 = jnp.bfloat16

D_MODEL = 1024
BATCH = 4
SEQ = 4096
DEPTH = 4
DEC_BATCH = 16
DEC_SEQ = 64
PAST_LEN = 1024
CHUNK = 64
D_FF = 4 * D_MODEL
HEAD_DIM = 64
N_HEADS = 16
SWA_KV_HEADS = 4
SWA_GROUP = 4
WINDOW = 128
LRU_WIDTH = D_MODEL
LRU_BLOCKS = 4
LRU_BLOCK_W = LRU_WIDTH // LRU_BLOCKS
CONV_WIDTH = 4
LRU_C = 8.0
N_BUCKETS = 32
MAX_DISTANCE = 128
ALPHA = (2.0 * DEPTH) ** 0.25
LN_EPS = 1e-5
ATTN_SCALE = HEAD_DIM ** -0.5

N_PROMPT = BATCH * SEQ
N_SAMPLE = DEC_BATCH * DEC_SEQ

LANES = 128
HEADS_PER_SLAB = LANES // HEAD_DIM
N_SLABS = D_MODEL // LANES

TOKEN_TILE = 512
POST_TILE = 1024
POST_ROWS = 256
FF_CHUNK = 1024
RG_TILE = 256
RG_PROJ_CHUNKS = 8
SWA_TILE = 512
SWA_KEYS = WINDOW + CHUNK
FOX_TQ = 512
FOX_TK = 512
FOX_ONES_ROWS = 16
BIAS_TILE = 512
VMEM_LIMIT = 56 * 1024 * 1024


def _resident(shape, layer=None):
    zeros = (0,) * len(shape)
    if layer is None:
        return pl.BlockSpec(shape, lambda *_: zeros, pipeline_mode=pl.Buffered(1))
    return pl.BlockSpec((None,) + tuple(shape), lambda *_: (layer,) + zeros,
                        pipeline_mode=pl.Buffered(1))


def _params(semantics):
    return pltpu.CompilerParams(dimension_semantics=semantics, vmem_limit_bytes=VMEM_LIMIT)


def _softplus(x):
    return jnp.maximum(x, 0.0) + jnp.log1p(jnp.exp(-jnp.abs(x)))


def _layer_norm(z, g, b):
    mu = jnp.mean(z, axis=-1, keepdims=True)
    zc = z - mu
    var = jnp.mean(zc * zc, axis=-1, keepdims=True)
    return zc * lax.rsqrt(var + LN_EPS) * g + b


def _post_kernel(x_ref, o_ref, wout_ref, wup_ref, wdn_ref, g_ref, b_ref, out_ref, *, slab_major):
    groups = [slice(r * POST_ROWS, (r + 1) * POST_ROWS) for r in range(POST_TILE // POST_ROWS)]
    y = []
    for rows in groups:
        if slab_major:
            o = jnp.concatenate([o_ref[s, rows, :] for s in range(N_SLABS)], axis=1)
        else:
            o = o_ref[rows, :]
        y.append(jnp.dot(o, wout_ref[...], preferred_element_type=F32))
    x1 = [_layer_norm(ALPHA * x_ref[rows, :] + y_r, g_ref[0:1, :], b_ref[0:1, :])
          for rows, y_r in zip(groups, y)]
    for rows, x1_r in zip(groups, x1):
        x1b = x1_r.astype(BF16)
        acc = jnp.zeros_like(x1_r)
        for c in range(D_FF // FF_CHUNK):
            cols = slice(c * FF_CHUNK, (c + 1) * FF_CHUNK)
            h = jnp.dot(x1b, wup_ref[:, cols], preferred_element_type=F32)
            a = jnp.square(jnp.maximum(h, 0.0)).astype(BF16)
            acc = acc + jnp.dot(a, wdn_ref[cols, :], preferred_element_type=F32)
        out_ref[rows, :] = _layer_norm(ALPHA * x1_r + acc, g_ref[1:2, :], b_ref[1:2, :])


def _post_call(x, o, w_out, mixer_idx, w_up, w_down, gain, bias, layer):
    n = x.shape[0]
    tile = pl.BlockSpec((POST_TILE, D_MODEL), lambda i: (i, 0))
    slab_major = o.ndim == 3
    o_tile = pl.BlockSpec((N_SLABS, POST_TILE, LANES), lambda i: (0, i, 0)) if slab_major else tile
    return pl.pallas_call(
        functools.partial(_post_kernel, slab_major=slab_major),
        out_shape=jax.ShapeDtypeStruct((n, D_MODEL), F32),
        grid=(n // POST_TILE,),
        in_specs=[tile, o_tile, _resident((D_MODEL, D_MODEL), mixer_idx),
                  _resident((D_MODEL, D_FF), layer), _resident((D_FF, D_MODEL), layer),
                  _resident((2, D_MODEL), layer), _resident((2, D_MODEL), layer)],
        out_specs=tile,
        compiler_params=_params(("parallel",)),
        name="post",
    )(x, o, w_out, w_up, w_down, gain, bias)


def _run(*staged, order=None):
    live = {i: iter(s) for i, s in enumerate(staged)}
    plan = list(order or [])
    while live:
        turn = plan.pop(0) if plan else None
        for i in ([turn] if turn in live else list(live)):
            try:
                next(live[i])
            except StopIteration:
                del live[i]


def _rg_project(x_ref, perm_ref, win_ref, gu_ref):
    xb = x_ref[...].astype(BF16)
    xb = jnp.dot(perm_ref[...], xb, preferred_element_type=F32).astype(BF16)
    yield
    width = 2 * LRU_WIDTH // RG_PROJ_CHUNKS
    for c in range(RG_PROJ_CHUNKS):
        cols = slice(c * width, (c + 1) * width)
        gu_ref[:, cols] = jnp.dot(xb, win_ref[:, cols], preferred_element_type=F32)
        yield


def _rg_mix(gu_ref, unperm_ref, cw_ref, cb_ref, gw_ref, gb_ref, lam_ref, o_ref, nbuf_ref, rows,
            scratch, *, tile):
    u_scr, gg_scr, h_scr, ac_scr, hist_scr, hc_scr = scratch
    hist = CONV_WIDTH - 1
    seg = tile // 8
    lead = 8 * hist

    u = gu_ref[:, LRU_WIDTH:]
    u_scr[lead:lead + tile, :] = u
    gg_scr[...] = jax.nn.gelu(gu_ref[:, :LRU_WIDTH])
    yield
    sub = lax.broadcasted_iota(jnp.int32, (8, LRU_WIDTH), 0)
    for d in range(1, hist + 1):
        prev_tail = pltpu.roll(u_scr[lead + 8 * (seg - d):lead + 8 * (seg - d) + 8, :], 1, 0)
        u_scr[lead - 8 * d:lead - 8 * d + 8, :] = jnp.where(sub == 0, hist_scr[8 - d:8 - d + 1, :],
                                                           prev_tail)
    conv = cb_ref[...] + u * cw_ref[hist:hist + 1, :]
    for k in range(hist):
        conv = conv + u_scr[8 * k:8 * k + tile, :] * cw_ref[k:k + 1, :]
    for d in range(1, hist + 1):
        last = u_scr[lead + 8 * (seg - d) + 7:lead + 8 * (seg - d) + 8, :]
        hist_scr[8 - d:8 - d + 1, :] = last
        nbuf_ref[0, hist - d:hist - d + 1, :] = last
    yield

    sub_blk = lax.broadcasted_iota(jnp.int32, (8, LRU_BLOCK_W), 0)
    for n in range(LRU_BLOCKS):
        cols = slice(n * LRU_BLOCK_W, (n + 1) * LRU_BLOCK_W)
        cn = conv[:, cols]
        g = jnp.dot(cn.astype(BF16), gw_ref[n], preferred_element_type=F32) + gb_ref[n]
        r = jax.nn.sigmoid(g[:, :LRU_BLOCK_W])
        ig = jax.nn.sigmoid(g[:, LRU_BLOCK_W:])
        log_a = -LRU_C * r * _softplus(-lam_ref[:, cols])
        a_blk = jnp.exp(log_a)
        var = 1.0 - a_blk * a_blk
        root = jnp.where(var > 0.0, var * lax.rsqrt(var), 0.0)
        b_blk = root * (ig * cn)
        h = jnp.zeros((8, LRU_BLOCK_W), F32)
        prod = jnp.ones((8, LRU_BLOCK_W), F32)
        for j in range(seg):
            a = a_blk[8 * j:8 * j + 8, :]
            h = a * h + b_blk[8 * j:8 * j + 8, :]
            prod = a * prod
            h_scr[8 * j:8 * j + 8, cols] = h
            ac_scr[8 * j:8 * j + 8, cols] = prod
        carry = jnp.where(sub_blk == 0, hc_scr[:, cols], 0.0)
        for i in range(1, 8):
            carry = jnp.where(sub_blk == i, pltpu.roll(prod * carry + h, 1, 0), carry)
        hc_scr[:, cols] = (prod * carry + h)[7:8, :]
        for j in range(seg):
            h_scr[8 * j:8 * j + 8, cols] = (h_scr[8 * j:8 * j + 8, cols]
                                            + ac_scr[8 * j:8 * j + 8, cols] * carry)
        yield

    o = (h_scr[...] * gg_scr[...]).astype(BF16)
    o_ref[rows, :] = jnp.dot(unperm_ref[...], o, preferred_element_type=F32).astype(BF16)


def _rg_kernel(*refs, tile, pipelined):
    n_x = 3 if pipelined else 1
    x_refs = refs[:n_x]
    (cbuf_ref, h0_ref, perm_ref, unperm_ref, win_ref, cw_ref, cb_ref, gw_ref, gb_ref, lam_ref,
     o_ref, nbuf_ref, hlast_ref, gu_a, gu_b) = refs[n_x:n_x + 15]
    scratch = refs[n_x + 15:]
    hist_scr, hc_scr = scratch[-2:]
    hist = CONV_WIDTH - 1

    @pl.when(pl.program_id(1) == 0)
    def _():
        hist_scr[8 - hist:8, :] = cbuf_ref[0]
        hc_scr[...] = h0_ref[0]
        _run(_rg_project(x_refs[0], perm_ref, win_ref, gu_a))

    mix = functools.partial(_rg_mix, unperm_ref=unperm_ref, cw_ref=cw_ref, cb_ref=cb_ref, gw_ref=gw_ref,
                            gb_ref=gb_ref, lam_ref=lam_ref, o_ref=o_ref, nbuf_ref=nbuf_ref,
                            scratch=scratch, tile=tile)
    if pipelined:
        order = [1, 0, 1, 1, 0, 0, 1, 0, 0, 1, 0, 0, 1, 0, 0, 1]
        _run(_rg_project(x_refs[1], perm_ref, win_ref, gu_b), mix(gu_a, rows=slice(0, tile)),
             order=order)
        _run(_rg_project(x_refs[2], perm_ref, win_ref, gu_a), mix(gu_b, rows=slice(tile, 2 * tile)),
             order=order)
    else:
        _run(mix(gu_a, rows=slice(0, tile)))
    hlast_ref[0] = hc_scr[...]


def _segment_interleave(tile):
    seg = tile // 8
    src = [(p % 8) * seg + p // 8 for p in range(tile)]
    return jnp.zeros((tile, tile), BF16).at[jnp.arange(tile), jnp.array(src)].set(1.0)


def _rg_call(x, conv_buf, h0, w_in, gate_w, small, layer, *, n_seq, seq_len, tile):
    conv_w, conv_b, gate_b, lam = small
    n_t = seq_len // tile
    pipelined = n_t >= 2 and n_t % 2 == 0
    tiles_per_step = 2 if pipelined else 1
    n_steps = n_t // tiles_per_step
    perm = _segment_interleave(tile)
    if pipelined:
        x_specs = [pl.BlockSpec((tile, D_MODEL), lambda b, s: (b * n_t, 0)),
                   pl.BlockSpec((tile, D_MODEL), lambda b, s: (b * n_t + 2 * s + 1, 0)),
                   pl.BlockSpec((tile, D_MODEL),
                                lambda b, s: (b * n_t + jnp.minimum(2 * s + 2, n_t - 1), 0))]
    else:
        x_specs = [pl.BlockSpec((tile, D_MODEL), lambda b, s: (b * n_t + s, 0))]
    kern = functools.partial(_rg_kernel, tile=tile, pipelined=pipelined)
    o, nbuf, hlast = pl.pallas_call(
        kern,
        out_shape=(jax.ShapeDtypeStruct((n_seq * seq_len, D_MODEL), BF16),
                   jax.ShapeDtypeStruct((n_seq, CONV_WIDTH - 1, LRU_WIDTH), F32),
                   jax.ShapeDtypeStruct((n_seq, 1, LRU_WIDTH), F32)),
        grid=(n_seq, n_steps),
        in_specs=x_specs + [
            pl.BlockSpec((1, CONV_WIDTH - 1, LRU_WIDTH), lambda b, s: (b, 0, 0)),
            pl.BlockSpec((1, 1, LRU_WIDTH), lambda b, s: (b, 0, 0)),
            _resident((tile, tile)), _resident((tile, tile)),
            _resident((D_MODEL, 2 * LRU_WIDTH), layer),
            _resident((CONV_WIDTH, LRU_WIDTH)),
            _resident((1, LRU_WIDTH)),
            _resident((LRU_BLOCKS, LRU_BLOCK_W, 2 * LRU_BLOCK_W), layer),
            _resident((LRU_BLOCKS, 1, 2 * LRU_BLOCK_W)),
            _resident((1, LRU_WIDTH))],
        out_specs=(pl.BlockSpec((tiles_per_step * tile, D_MODEL), lambda b, s: (b * n_steps + s, 0)),
                   pl.BlockSpec((1, CONV_WIDTH - 1, LRU_WIDTH), lambda b, s: (b, 0, 0)),
                   pl.BlockSpec((1, 1, LRU_WIDTH), lambda b, s: (b, 0, 0))),
        scratch_shapes=[pltpu.VMEM((tile, 2 * LRU_WIDTH), F32),
                        pltpu.VMEM((tile, 2 * LRU_WIDTH), F32),
                        pltpu.VMEM((tile + 8 * (CONV_WIDTH - 1), LRU_WIDTH), F32),
                        pltpu.VMEM((tile, LRU_WIDTH), F32),
                        pltpu.VMEM((tile, LRU_WIDTH), F32),
                        pltpu.VMEM((tile, LRU_WIDTH), F32),
                        pltpu.VMEM((8, LRU_WIDTH), F32),
                        pltpu.VMEM((1, LRU_WIDTH), F32)],
        compiler_params=_params(("parallel", "arbitrary")),
        name="rg_mixer",
    )(*([x] * len(x_specs)), conv_buf, h0, perm, perm.T, w_in, conv_w, conv_b, gate_w, gate_b, lam)
    return o, nbuf, hlast[:, 0, :]


def _swa_proj_kernel(x_ref, w_ref, q_ref, k_ref, v_ref, kb_ref, vb_ref):
    xb = x_ref[...].astype(BF16)
    qkv = jnp.dot(xb, w_ref[...], preferred_element_type=F32)
    kvd = SWA_KV_HEADS * HEAD_DIM
    q_ref[...] = qkv[:, :D_MODEL].astype(BF16)
    k = qkv[:, D_MODEL:D_MODEL + kvd]
    v = qkv[:, D_MODEL + kvd:]
    k_ref[...] = k
    v_ref[...] = v
    kb_ref[...] = k.astype(BF16)
    vb_ref[...] = v.astype(BF16)


def _swa_proj_call(x, w):
    n = x.shape[0]
    kvd = SWA_KV_HEADS * HEAD_DIM
    row = lambda i: (i, 0)
    return pl.pallas_call(
        _swa_proj_kernel,
        out_shape=(jax.ShapeDtypeStruct((n, D_MODEL), BF16),
                   jax.ShapeDtypeStruct((n, kvd), F32),
                   jax.ShapeDtypeStruct((n, kvd), F32),
                   jax.ShapeDtypeStruct((n, kvd), BF16),
                   jax.ShapeDtypeStruct((n, kvd), BF16)),
        grid=(n // TOKEN_TILE,),
        in_specs=[pl.BlockSpec((TOKEN_TILE, D_MODEL), row), _resident((D_MODEL, D_MODEL + 2 * kvd))],
        out_specs=(pl.BlockSpec((TOKEN_TILE, D_MODEL), row),
                   pl.BlockSpec((TOKEN_TILE, kvd), row), pl.BlockSpec((TOKEN_TILE, kvd), row),
                   pl.BlockSpec((TOKEN_TILE, kvd), row), pl.BlockSpec((TOKEN_TILE, kvd), row)),
        compiler_params=_params(("parallel",)),
        name="swa_proj",
    )(x, w)


def _swa_attn_kernel(q_ref, k_ref, v_ref, bias_ref, sink_ref, o_ref, s_even, s_odd,
                     *, tile, n_invalid):
    ti = pl.program_id(1)
    cols = SWA_GROUP * HEADS_PER_SLAB * CHUNK
    half_cols = cols // HEADS_PER_SLAB
    lane_q = lax.broadcasted_iota(jnp.int32, (CHUNK, LANES), 1)
    low_q = lane_q < HEAD_DIM
    key_idx = lax.broadcasted_iota(jnp.int32, (SWA_KEYS, cols), 0)
    n_kv_slabs = SWA_KV_HEADS // HEADS_PER_SLAB
    units = [(c, t) for c in range(tile // CHUNK) for t in range(n_kv_slabs)]
    s_bufs = (s_even, s_odd)

    def window_start(c):
        return pl.multiple_of(ti * tile + c * CHUNK, CHUNK)

    def scores(u):
        c, t = units[u]
        row0 = window_start(c)
        kw = k_ref[0, pl.ds(row0, SWA_KEYS), t * LANES:(t + 1) * LANES]
        slabs = [q_ref[0, c * CHUNK:(c + 1) * CHUNK,
                       (SWA_GROUP * t + g) * LANES:(SWA_GROUP * t + g + 1) * LANES]
                 for g in range(SWA_GROUP)]
        zero = jnp.zeros_like(slabs[0])
        q_stack = jnp.concatenate([jnp.where(low_q, s_, zero) for s_ in slabs]
                                  + [jnp.where(low_q, zero, s_) for s_ in slabs], axis=0)
        s = lax.dot_general(kw, q_stack, (((1,), (1,)), ((), ())), preferred_element_type=F32)
        s = s + bias_ref[t]
        if n_invalid > 0 and c * CHUNK < n_invalid:
            s = jnp.where(key_idx + row0 >= n_invalid, s, -jnp.inf)
        s_bufs[u % 2][...] = s

    def attend(u):
        c, t = units[u]
        s = s_bufs[u % 2][...]
        vw = v_ref[0, pl.ds(window_start(c), SWA_KEYS), t * LANES:(t + 1) * LANES]
        sink = sink_ref[t]
        m = jnp.maximum(jnp.max(s, axis=0, keepdims=True), sink)
        p = jnp.exp(s - m)
        denom = jnp.sum(p, axis=0, keepdims=True) + jnp.exp(sink - m)
        out_t = lax.dot_general(vw, p.astype(BF16), (((0,), (0,)), ((), ())),
                                preferred_element_type=F32) / denom
        pair = jnp.concatenate([out_t[:HEAD_DIM, :half_cols], out_t[HEAD_DIM:, half_cols:]], axis=0)
        pair = pair.T.astype(BF16)
        for g in range(SWA_GROUP):
            s_idx = SWA_GROUP * t + g
            o_ref[0, c * CHUNK:(c + 1) * CHUNK, s_idx * LANES:(s_idx + 1) * LANES] = (
                pair[g * CHUNK:(g + 1) * CHUNK, :])

    scores(0)
    for u in range(len(units)):
        if u + 1 < len(units):
            scores(u + 1)
        attend(u)


def _swa_attn_call(q, k_win, v_win, bias, sink, *, n_seq, seq_len, tile, n_invalid):
    kvd = SWA_KV_HEADS * HEAD_DIM
    n_keys = k_win.shape[1]
    cols = SWA_GROUP * HEADS_PER_SLAB * CHUNK
    n_kv_slabs = SWA_KV_HEADS // HEADS_PER_SLAB
    kern = functools.partial(_swa_attn_kernel, tile=tile, n_invalid=n_invalid)
    return pl.pallas_call(
        kern,
        out_shape=jax.ShapeDtypeStruct((n_seq, seq_len, D_MODEL), BF16),
        grid=(n_seq, seq_len // tile),
        in_specs=[pl.BlockSpec((1, tile, D_MODEL), lambda b, t: (b, t, 0)),
                  pl.BlockSpec((1, n_keys, kvd), lambda b, t: (b, 0, 0)),
                  pl.BlockSpec((1, n_keys, kvd), lambda b, t: (b, 0, 0)),
                  _resident((n_kv_slabs, SWA_KEYS, cols)),
                  _resident((n_kv_slabs, 1, cols))],
        out_specs=pl.BlockSpec((1, tile, D_MODEL), lambda b, t: (b, t, 0)),
        scratch_shapes=[pltpu.VMEM((SWA_KEYS, cols), F32), pltpu.VMEM((SWA_KEYS, cols), F32)],
        compiler_params=_params(("parallel", "arbitrary")),
        name="swa_attn",
    )(q, k_win, v_win, bias, sink)


def _t5_bucket(rel):
    half = N_BUCKETS // 2
    max_exact = half // 2
    n = jnp.abs(rel)
    n_f = jnp.maximum(n, 1).astype(jnp.float32)
    large = max_exact + (jnp.log(n_f / max_exact) / math.log(MAX_DISTANCE / max_exact)
                         * (half - max_exact)).astype(jnp.int32)
    large = jnp.minimum(large, half - 1)
    return jnp.where(rel > 0, half, 0) + jnp.where(n < max_exact, n, large)


_SWA_HEAD_ORDER = [8 * t + 4 * p + g for t in range(2) for g in range(4) for p in range(2)]


def _swa_row_tables(table, sinks):
    rel = jnp.arange(SWA_KEYS)[None, :] - WINDOW - jnp.arange(CHUNK)[:, None]
    bucket = _t5_bucket(rel)
    tbl = table.astype(F32)
    hit = bucket[None, None] == jnp.arange(N_BUCKETS)[:, None, None, None]
    bias = jnp.sum(jnp.where(hit, tbl[:, :, None, None], 0.0), axis=0)
    n_kv_slabs = SWA_KV_HEADS // HEADS_PER_SLAB
    cols = SWA_GROUP * HEADS_PER_SLAB * CHUNK
    bias = bias.reshape(n_kv_slabs, cols, SWA_KEYS).transpose(0, 2, 1)
    sink = jnp.repeat(sinks.astype(F32), CHUNK).reshape(n_kv_slabs, 1, cols)
    return bias, sink


def _fox_proj_kernel(x_ref, w_ref, wt_ref, bf_ref, q_ref, k_ref, v_ref, kb_ref, vt_ref, lf_ref,
                     *, time_minor):
    xb = x_ref[...].astype(BF16)
    proj = jnp.dot(xb, w_ref[...], preferred_element_type=F32)
    proj_t = lax.dot_general(wt_ref[...], xb, (((1,), (1,)), ((), ())), preferred_element_type=F32)
    if time_minor:
        k = proj[:, :D_MODEL]
        z = proj[:, D_MODEL:]
        vt = proj_t[D_MODEL:, :]
        k_ref[0] = k.T
        v_ref[0] = vt
    else:
        k = proj[:, D_MODEL:2 * D_MODEL]
        z = proj[:, 3 * D_MODEL:]
        vt = proj_t
        k_ref[...] = k.reshape(TOKEN_TILE, N_HEADS, HEAD_DIM)
        v_ref[...] = proj[:, 2 * D_MODEL:3 * D_MODEL].reshape(TOKEN_TILE, N_HEADS, HEAD_DIM)
    for s in range(N_SLABS):
        cols = slice(s * LANES, (s + 1) * LANES)
        q_ref[s] = (proj_t[cols, :] if time_minor else proj[:, cols]).astype(BF16)
        kb_ref[s] = k[:, cols].astype(BF16)
        for j in range(TOKEN_TILE // FOX_TK):
            vt_ref[s, j] = vt[cols, j * FOX_TK:(j + 1) * FOX_TK].astype(BF16)
    lf_ref[...] = -_softplus(-(z + bf_ref[...]))


def _fox_proj_call(x, w, w_t, b_f, *, seq_len, time_minor):
    n = x.shape[0]
    row = lambda i: (i, 0)
    wide = pl.BlockSpec((TOKEN_TILE, D_MODEL), row)
    slabs = pl.BlockSpec((N_SLABS, TOKEN_TILE, LANES), lambda i: (0, i, 0))
    kb_per_tile = TOKEN_TILE // FOX_TK
    if time_minor:
        t_per_seq = seq_len // TOKEN_TILE
        kv_shape = jax.ShapeDtypeStruct((n // seq_len, D_MODEL, seq_len), F32)
        kv_spec = pl.BlockSpec((1, D_MODEL, TOKEN_TILE), lambda i: (i // t_per_seq, 0, i % t_per_seq))
        q_shape = jax.ShapeDtypeStruct((N_SLABS, LANES, n), BF16)
        q_spec = pl.BlockSpec((N_SLABS, LANES, TOKEN_TILE), lambda i: (0, 0, i))
    else:
        kv_shape = jax.ShapeDtypeStruct((n, N_HEADS, HEAD_DIM), F32)
        kv_spec = pl.BlockSpec((TOKEN_TILE, N_HEADS, HEAD_DIM), lambda i: (i, 0, 0))
        q_shape = jax.ShapeDtypeStruct((N_SLABS, n, LANES), BF16)
        q_spec = slabs
    return pl.pallas_call(
        functools.partial(_fox_proj_kernel, time_minor=time_minor),
        out_shape=(q_shape, kv_shape, kv_shape,
                   jax.ShapeDtypeStruct((N_SLABS, n, LANES), BF16),
                   jax.ShapeDtypeStruct((N_SLABS, n // FOX_TK, LANES, FOX_TK), BF16),
                   jax.ShapeDtypeStruct((n, LANES), F32)),
        grid=(n // TOKEN_TILE,),
        in_specs=[wide, _resident(w.shape), _resident(w_t.shape), _resident((1, LANES))],
        out_specs=(q_spec, kv_spec, kv_spec, slabs,
                   pl.BlockSpec((N_SLABS, kb_per_tile, LANES, FOX_TK), lambda i: (0, i, 0, 0)),
                   pl.BlockSpec((TOKEN_TILE, LANES), row)),
        compiler_params=_params(("parallel",)),
        name="fox_proj",
    )(x, w, w_t, b_f)


def _lane_split3(x, lane):
    hi = x.astype(BF16)
    r1 = x - hi.astype(F32)
    mid = r1.astype(BF16)
    lo = (r1 - mid.astype(F32)).astype(BF16)
    return jnp.where(lane < N_HEADS, hi, jnp.where(lane < 2 * N_HEADS, mid, lo))


def _fox_bias_kernel(lf_ref, tri_ref, place_ref, const_ref, bq_ref, bk_ref, carry_scr):
    t = pl.program_id(1)

    @pl.when(t == 0)
    def _():
        carry_scr[...] = jnp.zeros_like(carry_scr)

    lane = lax.broadcasted_iota(jnp.int32, (BIAS_TILE, LANES), 1)
    sums = carry_scr[...] + jnp.dot(tri_ref[...], _lane_split3(lf_ref[0], lane),
                                    preferred_element_type=F32)
    carry_scr[...] = sums[BIAS_TILE - 1:BIAS_TILE, :]
    c = sums + pltpu.roll(sums, LANES - N_HEADS, 1) + pltpu.roll(sums, LANES - 2 * N_HEADS, 1)
    c = jnp.where(lane < N_HEADS, c, 0.0)
    c = c + pltpu.roll(c, N_HEADS, 1) + pltpu.roll(c, 2 * N_HEADS, 1)
    both = const_ref[...] + jnp.dot(_lane_split3(c, lane), place_ref[...],
                                    preferred_element_type=F32)
    bq_ref[0] = both[:, :LANES].astype(BF16)
    bk_ref[0] = both[:, LANES:].astype(BF16)


def _fox_bias_tables():
    place = [[0.0] * (2 * LANES) for _ in range(LANES)]
    const = [0.0] * (2 * LANES)
    for i in range(3):
        for h in range(N_HEADS):
            const[N_HEADS * i + h] = 1.0
            place[N_HEADS * i + h][N_HEADS * (3 + i) + h] = 1.0
            place[N_HEADS * i + h][LANES + N_HEADS * i + h] = -1.0
            const[LANES + N_HEADS * (3 + i) + h] = 1.0
    return jnp.array(place, BF16), jnp.array([const], F32)


def _fox_bias_call(logf):
    n_seq, n_rows, _ = logf.shape
    place, const = _fox_bias_tables()
    tri = jnp.tril(jnp.ones((BIAS_TILE, BIAS_TILE), BF16))
    blk = pl.BlockSpec((1, BIAS_TILE, LANES), lambda b, t: (b, t, 0))
    return pl.pallas_call(
        _fox_bias_kernel,
        out_shape=(jax.ShapeDtypeStruct((n_seq, n_rows, LANES), BF16),
                   jax.ShapeDtypeStruct((n_seq, n_rows, LANES), BF16)),
        grid=(n_seq, n_rows // BIAS_TILE),
        in_specs=[blk, _resident((BIAS_TILE, BIAS_TILE)), _resident((LANES, 2 * LANES)),
                  _resident((1, 2 * LANES))],
        out_specs=(blk, blk),
        scratch_shapes=[pltpu.VMEM((1, LANES), F32)],
        compiler_params=_params(("parallel", "arbitrary")),
        name="fox_bias",
    )(logf, tri, place, const)


def _fox_attn_core(q_ref, bq_ref, bk_ref, o_ref, scratch, *, tq, tk, first_q, q_time_minor, run_blocks):
    qaug_scr, m_scr, acc_scr, s_even, s_odd, smax_even, smax_odd = scratch
    k_off = lax.broadcasted_iota(jnp.int32, (tk, tq), 0)
    q_pos = first_q + lax.broadcasted_iota(jnp.int32, (tk, tq), 1)
    ones_rows = jnp.ones((FOX_ONES_ROWS, tk), BF16)

    feat = lax.broadcasted_iota(jnp.int32, (LANES, tq) if q_time_minor else (tq, LANES),
                                0 if q_time_minor else 1)
    low_feat = feat < HEAD_DIM
    bias_feat = feat < 6 * N_HEADS
    bq = bq_ref[0]
    if q_time_minor:
        bq = bq.astype(F32).T.astype(BF16)
    zero = jnp.zeros_like(bq)
    for head in range(N_HEADS):
        q_slab = q_ref[head // HEADS_PER_SLAB]
        own_half = low_feat if head % HEADS_PER_SLAB == 0 else jnp.logical_not(low_feat)
        sel = jnp.logical_and((feat & (N_HEADS - 1)) == head, bias_feat)
        qaug_scr[head] = jnp.concatenate([jnp.where(own_half, q_slab, zero), jnp.where(sel, bq, zero)],
                                         axis=0 if q_time_minor else 1)
    q_contract = 0 if q_time_minor else 1
    m_scr[...] = jnp.full_like(m_scr, -jnp.inf)
    acc_scr[...] = jnp.zeros_like(acc_scr)
    slots = ((s_even, smax_even), (s_odd, smax_odd))

    def key_block(kb, masked, k_of, vt_of):
        bk_blk = bk_ref[0, pl.ds(pl.multiple_of(kb * tk, tk), tk), :]

        def scores(hp):
            s_buf, smax_buf = slots[hp % 2]
            k_aug = jnp.concatenate([k_of(hp), bk_blk], axis=1)
            for par in range(HEADS_PER_SLAB):
                s = lax.dot_general(k_aug, qaug_scr[hp * HEADS_PER_SLAB + par],
                                    (((1,), (q_contract,)), ((), ())),
                                    preferred_element_type=F32)
                if masked:
                    s = jnp.where(kb * tk + k_off <= q_pos, s, -jnp.inf)
                s_buf[par] = s
                smax_buf[par] = jnp.max(s, axis=0, keepdims=True)

        def absorb(hp):
            s_buf, smax_buf = slots[hp % 2]
            vt = vt_of(hp)
            for par in range(HEADS_PER_SLAB):
                head = hp * HEADS_PER_SLAB + par
                m = m_scr[head]
                m_new = jnp.maximum(m, smax_buf[par])
                p = jnp.exp(s_buf[par] - m_new).astype(BF16)
                vt_aug = jnp.concatenate([vt[par * HEAD_DIM:(par + 1) * HEAD_DIM, :], ones_rows], axis=0)
                acc_scr[head] = (jnp.exp(m - m_new) * acc_scr[head]
                                 + jnp.dot(vt_aug, p, preferred_element_type=F32))
                m_scr[head] = m_new

        scores(0)
        for hp in range(N_SLABS):
            if hp + 1 < N_SLABS:
                scores(hp + 1)
            absorb(hp)

    run_blocks(key_block)

    for hp in range(N_SLABS):
        halves = []
        for par in range(HEADS_PER_SLAB):
            acc = acc_scr[hp * HEADS_PER_SLAB + par]
            halves.append(acc[:HEAD_DIM] / acc[HEAD_DIM:HEAD_DIM + 1])
        o_ref[hp] = jnp.concatenate(halves, axis=0).T.astype(BF16)


def _fox_attn_scratch(tq, tk, q_time_minor):
    return [pltpu.VMEM((N_HEADS, 2 * LANES, tq) if q_time_minor else (N_HEADS, tq, 2 * LANES), BF16),
            pltpu.VMEM((N_HEADS, 1, tq), F32),
            pltpu.VMEM((N_HEADS, HEAD_DIM + FOX_ONES_ROWS, tq), F32),
            pltpu.VMEM((HEADS_PER_SLAB, tk, tq), F32),
            pltpu.VMEM((HEADS_PER_SLAB, tk, tq), F32),
            pltpu.VMEM((HEADS_PER_SLAB, 1, tq), F32),
            pltpu.VMEM((HEADS_PER_SLAB, 1, tq), F32)]


def _fox_attn_kernel(q_ref, bq_ref, k_ref, bk_ref, vt_ref, o_ref, *scratch, tq, tk):
    first_q = pl.program_id(1) * tq
    n_full = first_q // tk

    def run_blocks(key_block):
        def block(kb, masked):
            r0 = pl.multiple_of(kb * tk, tk)
            key_block(kb, masked, lambda hp: k_ref[hp, pl.ds(r0, tk), :], lambda hp: vt_ref[hp, kb])

        def full_block(kb, carry):
            block(kb, False)
            return carry

        lax.fori_loop(0, n_full, full_block, 0)
        block(n_full, True)

    _fox_attn_core(q_ref, bq_ref, bk_ref, o_ref, scratch, tq=tq, tk=tk, first_q=first_q,
                   q_time_minor=True, run_blocks=run_blocks)


def _fox_attn_call(q, bq, k, bk, vt, *, n_seq, seq_len, tq, tk):
    assert tq <= tk and tk % tq == 0 and seq_len % tk == 0
    n_qt = seq_len // tq
    n_kb = seq_len // tk
    return pl.pallas_call(
        functools.partial(_fox_attn_kernel, tq=tq, tk=tk),
        out_shape=jax.ShapeDtypeStruct((N_SLABS, n_seq * seq_len, LANES), BF16),
        grid=(n_seq, n_qt),
        in_specs=[pl.BlockSpec((N_SLABS, LANES, tq), lambda b, i: (0, 0, b * n_qt + i)),
                  pl.BlockSpec((1, tq, LANES), lambda b, i: (b, i, 0)),
                  pl.BlockSpec((N_SLABS, seq_len, LANES), lambda b, i: (0, b, 0)),
                  pl.BlockSpec((1, seq_len, LANES), lambda b, i: (b, 0, 0)),
                  pl.BlockSpec((N_SLABS, n_kb, LANES, tk), lambda b, i: (0, b, 0, 0))],
        out_specs=pl.BlockSpec((N_SLABS, tq, LANES), lambda b, i: (0, b * n_qt + i, 0)),
        scratch_shapes=_fox_attn_scratch(tq, tk, True),
        compiler_params=_params(("parallel", "arbitrary")),
        name="fox_attn",
    )(q, bq, k, bk, vt)


def _fox_step_attn_kernel(q_ref, bq_ref, kold_ref, knew_ref, bk_ref, vtold_ref, vtnew_ref, o_ref,
                          *scratch, n_new, n_old, tk):
    def run_blocks(key_block):
        for kb in range(n_old // tk):
            rows = slice(kb * tk, (kb + 1) * tk)
            key_block(kb, False, lambda hp, rows=rows: kold_ref[0, hp, :, rows].T.astype(BF16),
                      lambda hp, rows=rows: vtold_ref[0, hp, :, rows].astype(BF16))
        key_block(n_old // tk, True, lambda hp: knew_ref[hp], lambda hp: vtnew_ref[hp, 0])

    _fox_attn_core(q_ref, bq_ref, bk_ref, o_ref, scratch, tq=n_new, tk=tk, first_q=n_old,
                   q_time_minor=False, run_blocks=run_blocks)


def _fox_step_attn_call(q, bq, kt_old, k_new, bk, vt_old, vt_new, *, n_seq, n_new, n_old, tk):
    assert n_new <= tk and n_old % tk == 0 and n_old % n_new == 0
    return pl.pallas_call(
        functools.partial(_fox_step_attn_kernel, n_new=n_new, n_old=n_old, tk=tk),
        out_shape=jax.ShapeDtypeStruct((N_SLABS, n_seq * n_new, LANES), BF16),
        grid=(n_seq,),
        in_specs=[pl.BlockSpec((N_SLABS, n_new, LANES), lambda b: (0, b, 0)),
                  pl.BlockSpec((1, n_new, LANES), lambda b: (b, n_old // n_new, 0)),
                  pl.BlockSpec((1, N_SLABS, LANES, n_old), lambda b: (b, 0, 0, 0)),
                  pl.BlockSpec((N_SLABS, tk, LANES), lambda b: (0, b, 0)),
                  pl.BlockSpec((1, n_old + tk, LANES), lambda b: (b, 0, 0)),
                  pl.BlockSpec((1, N_SLABS, LANES, n_old), lambda b: (b, 0, 0, 0)),
                  pl.BlockSpec((N_SLABS, 1, LANES, tk), lambda b: (0, b, 0, 0))],
        out_specs=pl.BlockSpec((N_SLABS, n_new, LANES), lambda b: (0, b, 0)),
        scratch_shapes=_fox_attn_scratch(n_new, tk, False),
        compiler_params=_params(("parallel",)),
        name="fox_step_attn",
    )(q, bq, kt_old, k_new, bk, vt_old, vt_new)


def _rg_layer(xs, conv_state, h_state, w_in, gate_w, small, layer):
    x_p, x_s = xs
    o_p, nb_p, h_p = _rg_call(
        x_p, jnp.zeros((BATCH, CONV_WIDTH - 1, LRU_WIDTH), F32), jnp.zeros((BATCH, 1, LRU_WIDTH), F32),
        w_in, gate_w, small, layer, n_seq=BATCH, seq_len=SEQ, tile=RG_TILE)
    o_s, nb_s, h_s = _rg_call(
        x_s, conv_state, h_state.reshape(DEC_BATCH, 1, LRU_WIDTH),
        w_in, gate_w, small, layer, n_seq=DEC_BATCH, seq_len=DEC_SEQ, tile=DEC_SEQ)
    return (o_p, o_s), (nb_p, nb_s, h_p, h_s)


def _swa_layer(xs, k_cache, v_cache, w_qkv, sinks, table):
    kvd = SWA_KV_HEADS * HEAD_DIM
    order = jnp.array(_SWA_HEAD_ORDER)
    w_q = (w_qkv[:, :D_MODEL] * ATTN_SCALE).reshape(D_MODEL, N_HEADS, HEAD_DIM)[:, order]
    w = jnp.concatenate([w_q.reshape(D_MODEL, D_MODEL), w_qkv[:, D_MODEL:]], axis=1).astype(BF16)
    bias, sink = _swa_row_tables(table, sinks)

    q_p, k_p, v_p, kb_p, vb_p = _swa_proj_call(xs[0], w)
    pad = ((0, 0), (WINDOW, 0), (0, 0))
    o_p = _swa_attn_call(q_p.reshape(BATCH, SEQ, D_MODEL),
                         jnp.pad(kb_p.reshape(BATCH, SEQ, kvd), pad),
                         jnp.pad(vb_p.reshape(BATCH, SEQ, kvd), pad),
                         bias, sink, n_seq=BATCH, seq_len=SEQ, tile=SWA_TILE, n_invalid=WINDOW)

    q_s, k_s, v_s, kb_s, vb_s = _swa_proj_call(xs[1], w)
    kc = k_cache.reshape(DEC_BATCH, WINDOW, kvd)
    vc = v_cache.reshape(DEC_BATCH, WINDOW, kvd)
    o_s = _swa_attn_call(q_s.reshape(DEC_BATCH, DEC_SEQ, D_MODEL),
                         jnp.concatenate([kc.astype(BF16), kb_s.reshape(DEC_BATCH, DEC_SEQ, kvd)], axis=1),
                         jnp.concatenate([vc.astype(BF16), vb_s.reshape(DEC_BATCH, DEC_SEQ, kvd)], axis=1),
                         bias, sink, n_seq=DEC_BATCH, seq_len=DEC_SEQ, tile=DEC_SEQ, n_invalid=0)

    def tails(new_p, new_s, cache):
        tail_p = new_p.reshape(BATCH, SEQ, kvd)[:, SEQ - WINDOW:]
        tail_s = jnp.concatenate([cache.reshape(DEC_BATCH, WINDOW, kvd)[:, DEC_SEQ:],
                                  new_s.reshape(DEC_BATCH, DEC_SEQ, kvd)], axis=1)
        return (tail_p.reshape(BATCH, WINDOW, SWA_KV_HEADS, HEAD_DIM),
                tail_s.reshape(DEC_BATCH, WINDOW, SWA_KV_HEADS, HEAD_DIM))

    k_tp, k_ts = tails(k_p, k_s, k_cache)
    v_tp, v_ts = tails(v_p, v_s, v_cache)
    return ((o_p.reshape(N_PROMPT, D_MODEL), o_s.reshape(N_SAMPLE, D_MODEL)),
            (k_tp, k_ts, v_tp, v_ts))


def _fox_layer(xs, k_cache, v_cache, logf_cache, w_in, b_f):
    hd = N_HEADS * HEAD_DIM
    def thrice(a):
        rep = jnp.concatenate([a, a, a], axis=-1)
        return jnp.pad(rep, [(0, 0)] * (a.ndim - 1) + [(0, LANES - 3 * N_HEADS)])

    w_q = w_in[:, :hd] * ATTN_SCALE
    w_f = thrice(w_in[:, 3 * hd:])
    w_kf = jnp.concatenate([w_in[:, hd:2 * hd], w_f], axis=1).astype(BF16)
    w = jnp.concatenate([w_q, w_in[:, hd:3 * hd], w_f], axis=1).astype(BF16)
    bf = thrice(b_f.astype(F32)).reshape(1, LANES)
    w_vt = w_in[:, 2 * hd:3 * hd].T.astype(BF16)
    w_qvt = jnp.concatenate([w_q.T.astype(BF16), w_vt], axis=0)

    q_p, kt_p, vt32_p, kb_p, vt_p, lf_p = _fox_proj_call(xs[0], w_kf, w_qvt, bf, seq_len=SEQ,
                                                         time_minor=True)
    lf_p = lf_p.reshape(BATCH, SEQ, LANES)
    bq_p, bk_p = _fox_bias_call(lf_p)
    o_p = _fox_attn_call(q_p, bq_p, kb_p, bk_p, vt_p, n_seq=BATCH, seq_len=SEQ, tq=FOX_TQ, tk=FOX_TK)

    q_s, k_s, v_s, kb_s, vt_s, lf_s = _fox_proj_call(xs[1], w, w_vt, bf, seq_len=DEC_SEQ,
                                                     time_minor=False)
    lf_s = lf_s.reshape(DEC_BATCH, DEC_SEQ, LANES)
    tail = FOX_TK - DEC_SEQ
    bq_s, bk_s = _fox_bias_call(jnp.pad(jnp.concatenate([thrice(logf_cache.astype(F32)), lf_s], axis=1),
                                        ((0, 0), (0, tail), (0, 0))))
    k_new = jnp.pad(kb_s.reshape(N_SLABS, DEC_BATCH, DEC_SEQ, LANES), ((0, 0), (0, 0), (0, tail), (0, 0)))
    k_new = k_new.reshape(N_SLABS, DEC_BATCH * FOX_TK, LANES)

    def time_minor(cache):
        return cache.astype(F32).reshape(DEC_BATCH, PAST_LEN, N_SLABS, LANES).transpose(0, 2, 3, 1)

    kt_old = time_minor(k_cache)
    vt_old = time_minor(v_cache)
    per_blk = FOX_TK // DEC_SEQ
    vt_new = vt_s.reshape(N_SLABS, N_SAMPLE // FOX_TK, LANES, per_blk, DEC_SEQ)
    vt_new = vt_new.transpose(0, 1, 3, 2, 4).reshape(N_SLABS, DEC_BATCH, LANES, DEC_SEQ)
    vt_new = jnp.pad(vt_new, ((0, 0), (0, 0), (0, 0), (0, tail)))
    o_s = _fox_step_attn_call(q_s, bq_s, kt_old, k_new, bk_s, vt_old, vt_new, n_seq=DEC_BATCH,
                              n_new=DEC_SEQ, n_old=PAST_LEN, tk=FOX_TK)

    def heads_last(a):
        return a.reshape(BATCH, N_HEADS, HEAD_DIM, SEQ).transpose(0, 3, 1, 2)

    outs = (heads_last(kt_p), k_s.reshape(DEC_BATCH, DEC_SEQ, N_HEADS, HEAD_DIM),
            heads_last(vt32_p), v_s.reshape(DEC_BATCH, DEC_SEQ, N_HEADS, HEAD_DIM),
            lf_p[..., :N_HEADS], lf_s[..., :N_HEADS])
    return (o_p, o_s), outs


def kernel(x_prompt, x_sample, state_rg_conv, state_rg_h, cache_swa_k, cache_swa_v, cache_fox_k, cache_fox_v, cache_fox_logf, ln_gain, ln_bias, ffn_w_up, ffn_w_down, rg_w_in, rg_conv_w, rg_conv_b, rg_gate_w, rg_gate_b, rg_lambda, rg_w_out, swa_w_qkv, swa_sinks, swa_w_out, rel_bias_table, fox_w_in, fox_b_f, fox_w_out):
    xs = (x_prompt.reshape(N_PROMPT, D_MODEL), x_sample.reshape(N_SAMPLE, D_MODEL))
    w_up = ffn_w_up.astype(BF16)
    w_down = ffn_w_down.astype(BF16)
    rg_in = rg_w_in.astype(BF16)
    rg_gate = rg_gate_w.astype(BF16)
    swa_order = jnp.array(_SWA_HEAD_ORDER)
    n_swa = swa_w_out.shape[0]
    w_out = {0: rg_w_out.astype(BF16),
             1: swa_w_out.reshape(n_swa, N_HEADS, HEAD_DIM, D_MODEL)[:, swa_order]
                         .reshape(n_swa, D_MODEL, D_MODEL).astype(BF16),
             2: fox_w_out.astype(BF16)}
    rg_out, swa_out, fox_out = [], [], []
    for i in range(DEPTH):
        kind, j = i % 3, i // 3
        if kind == 0:
            small = (rg_conv_w[j], rg_conv_b[j].reshape(1, LRU_WIDTH),
                     rg_gate_b[j].reshape(LRU_BLOCKS, 1, 2 * LRU_BLOCK_W), rg_lambda[j].reshape(1, LRU_WIDTH))
            os_, extra = _rg_layer(xs, state_rg_conv[j], state_rg_h[j], rg_in, rg_gate, small, j)
            rg_out.append(extra)
        elif kind == 1:
            os_, extra = _swa_layer(xs, cache_swa_k[j], cache_swa_v[j], swa_w_qkv[j], swa_sinks[j],
                                    rel_bias_table)
            swa_out.append(extra)
        else:
            os_, extra = _fox_layer(xs, cache_fox_k[j], cache_fox_v[j], cache_fox_logf[j], fox_w_in[j],
                                    fox_b_f[j])
            fox_out.append(extra)
        xs = tuple(_post_call(x, o, w_out[kind], j, w_up, w_down, ln_gain, ln_bias, i)
                   for x, o in zip(xs, os_))

    def stack(items, idx):
        return jnp.stack([it[idx] for it in items])

    return (xs[0].reshape(BATCH, SEQ, D_MODEL), xs[1].reshape(DEC_BATCH, DEC_SEQ, D_MODEL),
            stack(rg_out, 0), stack(rg_out, 1), stack(rg_out, 2), stack(rg_out, 3),
            stack(swa_out, 0), stack(swa_out, 1), stack(swa_out, 2), stack(swa_out, 3),
            stack(fox_out, 0), stack(fox_out, 1), stack(fox_out, 2), stack(fox_out, 3),
            stack(fox_out, 4), stack(fox_out, 5))
```

```python
import functools
import math

import jax
import jax.numpy as jnp
from jax import lax
from jax.experimental import pallas as pl
from jax.experimental.pallas import tpu as pltpu

F32 = jnp.float32
BF16 = jnp.bfloat16

D_MODEL = 1024
BATCH = 4
SEQ = 4096
DEPTH = 4
DEC_BATCH = 16
DEC_SEQ = 64
PAST_LEN = 1024
CHUNK = 64
D_FF = 4 * D_MODEL
HEAD_DIM = 64
N_HEADS = 16
SWA_KV_HEADS = 4
SWA_GROUP = 4
WINDOW = 128
LRU_WIDTH = D_MODEL
LRU_BLOCKS = 4
LRU_BLOCK_W = LRU_WIDTH // LRU_BLOCKS
CONV_WIDTH = 4
LRU_C = 8.0
N_BUCKETS = 32
MAX_DISTANCE = 128
ALPHA = (2.0 * DEPTH) ** 0.25
LN_EPS = 1e-5
ATTN_SCALE = HEAD_DIM ** -0.5

N_PROMPT = BATCH * SEQ
N_SAMPLE = DEC_BATCH * DEC_SEQ

LANES = 128
HEADS_PER_SLAB = LANES // HEAD_DIM
N_SLABS = D_MODEL // LANES

TOKEN_TILE = 512
POST_TILE = 1024
POST_ROWS = 256
FF_CHUNK = 1024
RG_TILE = 256
RG_PROJ_CHUNKS = 8
SWA_TILE = 512
SWA_KEYS = WINDOW + CHUNK
FOX_TQ = 512
FOX_TK = 512
FOX_ONES_ROWS = 16
FOX_STEP_KEYS = 128
BIAS_TILE = 512
VMEM_LIMIT = 56 * 1024 * 1024


def _resident(shape, layer=None):
    zeros = (0,) * len(shape)
    if layer is None:
        return pl.BlockSpec(shape, lambda *_: zeros, pipeline_mode=pl.Buffered(1))
    return pl.BlockSpec((None,) + tuple(shape), lambda *_: (layer,) + zeros,
                        pipeline_mode=pl.Buffered(1))


def _params(semantics):
    return pltpu.CompilerParams(dimension_semantics=semantics, vmem_limit_bytes=VMEM_LIMIT)


def _softplus(x):
    return jnp.maximum(x, 0.0) + jnp.log1p(jnp.exp(-jnp.abs(x)))


def _layer_norm(z, g, b):
    mu = jnp.mean(z, axis=-1, keepdims=True)
    zc = z - mu
    var = jnp.mean(zc * zc, axis=-1, keepdims=True)
    return zc * lax.rsqrt(var + LN_EPS) * g + b


def _post_kernel(x_ref, o_ref, wout_ref, wup_ref, wdn_ref, g_ref, b_ref, out_ref, *, slab_major):
    groups = [slice(r * POST_ROWS, (r + 1) * POST_ROWS) for r in range(POST_TILE // POST_ROWS)]
    y = []
    for rows in groups:
        if slab_major:
            o = jnp.concatenate([o_ref[s, rows, :] for s in range(N_SLABS)], axis=1)
        else:
            o = o_ref[rows, :]
        y.append(jnp.dot(o, wout_ref[...], preferred_element_type=F32))
    x1 = [_layer_norm(ALPHA * x_ref[rows, :] + y_r, g_ref[0:1, :], b_ref[0:1, :])
          for rows, y_r in zip(groups, y)]
    for rows, x1_r in zip(groups, x1):
        x1b = x1_r.astype(BF16)
        acc = jnp.zeros_like(x1_r)
        for c in range(D_FF // FF_CHUNK):
            cols = slice(c * FF_CHUNK, (c + 1) * FF_CHUNK)
            h = jnp.dot(x1b, wup_ref[:, cols], preferred_element_type=F32)
            a = jnp.square(jnp.maximum(h, 0.0)).astype(BF16)
            acc = acc + jnp.dot(a, wdn_ref[cols, :], preferred_element_type=F32)
        out_ref[rows, :] = _layer_norm(ALPHA * x1_r + acc, g_ref[1:2, :], b_ref[1:2, :])


def _post_call(x, o, w_out, mixer_idx, w_up, w_down, gain, bias, layer):
    n = x.shape[0]
    tile = pl.BlockSpec((POST_TILE, D_MODEL), lambda i: (i, 0))
    slab_major = o.ndim == 3
    o_tile = pl.BlockSpec((N_SLABS, POST_TILE, LANES), lambda i: (0, i, 0)) if slab_major else tile
    return pl.pallas_call(
        functools.partial(_post_kernel, slab_major=slab_major),
        out_shape=jax.ShapeDtypeStruct((n, D_MODEL), F32),
        grid=(n // POST_TILE,),
        in_specs=[tile, o_tile, _resident((D_MODEL, D_MODEL), mixer_idx),
                  _resident((D_MODEL, D_FF), layer), _resident((D_FF, D_MODEL), layer),
                  _resident((2, D_MODEL), layer), _resident((2, D_MODEL), layer)],
        out_specs=tile,
        compiler_params=_params(("parallel",)),
        name="post",
    )(x, o, w_out, w_up, w_down, gain, bias)


def _run(*staged, order=None):
    live = {i: iter(s) for i, s in enumerate(staged)}
    plan = list(order or [])
    while live:
        turn = plan.pop(0) if plan else None
        for i in ([turn] if turn in live else list(live)):
            try:
                next(live[i])
            except StopIteration:
                del live[i]


def _rg_project(x_ref, perm_ref, win_ref, gu_ref):
    xb = x_ref[...].astype(BF16)
    xb = jnp.dot(perm_ref[...], xb, preferred_element_type=F32).astype(BF16)
    yield
    width = 2 * LRU_WIDTH // RG_PROJ_CHUNKS
    for c in range(RG_PROJ_CHUNKS):
        cols = slice(c * width, (c + 1) * width)
        gu_ref[:, cols] = jnp.dot(xb, win_ref[:, cols], preferred_element_type=F32)
        yield


def _rg_mix(gu_ref, unperm_ref, cw_ref, cb_ref, gw_ref, gb_ref, lam_ref, o_ref, nbuf_ref, rows,
            scratch, *, tile):
    u_scr, gg_scr, h_scr, ac_scr, hist_scr, hc_scr = scratch
    hist = CONV_WIDTH - 1
    seg = tile // 8
    lead = 8 * hist

    u = gu_ref[:, LRU_WIDTH:]
    u_scr[lead:lead + tile, :] = u
    gg_scr[...] = jax.nn.gelu(gu_ref[:, :LRU_WIDTH])
    yield
    sub = lax.broadcasted_iota(jnp.int32, (8, LRU_WIDTH), 0)
    for d in range(1, hist + 1):
        prev_tail = pltpu.roll(u_scr[lead + 8 * (seg - d):lead + 8 * (seg - d) + 8, :], 1, 0)
        u_scr[lead - 8 * d:lead - 8 * d + 8, :] = jnp.where(sub == 0, hist_scr[8 - d:8 - d + 1, :],
                                                           prev_tail)
    conv = cb_ref[...] + u * cw_ref[hist:hist + 1, :]
    for k in range(hist):
        conv = conv + u_scr[8 * k:8 * k + tile, :] * cw_ref[k:k + 1, :]
    for d in range(1, hist + 1):
        last = u_scr[lead + 8 * (seg - d) + 7:lead + 8 * (seg - d) + 8, :]
        hist_scr[8 - d:8 - d + 1, :] = last
        nbuf_ref[0, hist - d:hist - d + 1, :] = last
    yield

    sub_blk = lax.broadcasted_iota(jnp.int32, (8, LRU_BLOCK_W), 0)
    for n in range(LRU_BLOCKS):
        cols = slice(n * LRU_BLOCK_W, (n + 1) * LRU_BLOCK_W)
        cn = conv[:, cols]
        g = jnp.dot(cn.astype(BF16), gw_ref[n], preferred_element_type=F32) + gb_ref[n]
        r = jax.nn.sigmoid(g[:, :LRU_BLOCK_W])
        ig = jax.nn.sigmoid(g[:, LRU_BLOCK_W:])
        log_a = -LRU_C * r * _softplus(-lam_ref[:, cols])
        a_blk = jnp.exp(log_a)
        var = 1.0 - a_blk * a_blk
        root = jnp.where(var > 0.0, var * lax.rsqrt(var), 0.0)
        b_blk = root * (ig * cn)
        h = jnp.zeros((8, LRU_BLOCK_W), F32)
        prod = jnp.ones((8, LRU_BLOCK_W), F32)
        for j in range(seg):
            a = a_blk[8 * j:8 * j + 8, :]
            h = a * h + b_blk[8 * j:8 * j + 8, :]
            prod = a * prod
            h_scr[8 * j:8 * j + 8, cols] = h
            ac_scr[8 * j:8 * j + 8, cols] = prod
        carry = jnp.where(sub_blk == 0, hc_scr[:, cols], 0.0)
        for i in range(1, 8):
            carry = jnp.where(sub_blk == i, pltpu.roll(prod * carry + h, 1, 0), carry)
        hc_scr[:, cols] = (prod * carry + h)[7:8, :]
        for j in range(seg):
            h_scr[8 * j:8 * j + 8, cols] = (h_scr[8 * j:8 * j + 8, cols]
                                            + ac_scr[8 * j:8 * j + 8, cols] * carry)
        yield

    o = (h_scr[...] * gg_scr[...]).astype(BF16)
    o_ref[rows, :] = jnp.dot(unperm_ref[...], o, preferred_element_type=F32).astype(BF16)


def _rg_kernel(*refs, tile, pipelined):
    n_x = 3 if pipelined else 1
    x_refs = refs[:n_x]
    (cbuf_ref, h0_ref, perm_ref, unperm_ref, win_ref, cw_ref, cb_ref, gw_ref, gb_ref, lam_ref,
     o_ref, nbuf_ref, hlast_ref, gu_a, gu_b) = refs[n_x:n_x + 15]
    scratch = refs[n_x + 15:]
    hist_scr, hc_scr = scratch[-2:]
    hist = CONV_WIDTH - 1

    @pl.when(pl.program_id(1) == 0)
    def _():
        hist_scr[8 - hist:8, :] = cbuf_ref[0]
        hc_scr[...] = h0_ref[0]
        _run(_rg_project(x_refs[0], perm_ref, win_ref, gu_a))

    mix = functools.partial(_rg_mix, unperm_ref=unperm_ref, cw_ref=cw_ref, cb_ref=cb_ref, gw_ref=gw_ref,
                            gb_ref=gb_ref, lam_ref=lam_ref, o_ref=o_ref, nbuf_ref=nbuf_ref,
                            scratch=scratch, tile=tile)
    if pipelined:
        order = [1, 0, 1, 1, 0, 0, 1, 0, 0, 1, 0, 0, 1, 0, 0, 1]
        _run(_rg_project(x_refs[1], perm_ref, win_ref, gu_b), mix(gu_a, rows=slice(0, tile)),
             order=order)
        _run(_rg_project(x_refs[2], perm_ref, win_ref, gu_a), mix(gu_b, rows=slice(tile, 2 * tile)),
             order=order)
    else:
        _run(mix(gu_a, rows=slice(0, tile)))
    hlast_ref[0] = hc_scr[...]


def _segment_interleave(tile):
    seg = tile // 8
    src = [(p % 8) * seg + p // 8 for p in range(tile)]
    return jnp.zeros((tile, tile), BF16).at[jnp.arange(tile), jnp.array(src)].set(1.0)


def _rg_call(x, conv_buf, h0, w_in, gate_w, small, layer, *, n_seq, seq_len, tile):
    conv_w, conv_b, gate_b, lam = small
    n_t = seq_len // tile
    pipelined = n_t >= 2 and n_t % 2 == 0
    tiles_per_step = 2 if pipelined else 1
    n_steps = n_t // tiles_per_step
    perm = _segment_interleave(tile)
    if pipelined:
        x_specs = [pl.BlockSpec((tile, D_MODEL), lambda b, s: (b * n_t, 0)),
                   pl.BlockSpec((tile, D_MODEL), lambda b, s: (b * n_t + 2 * s + 1, 0)),
                   pl.BlockSpec((tile, D_MODEL),
                                lambda b, s: (b * n_t + jnp.minimum(2 * s + 2, n_t - 1), 0))]
    else:
        x_specs = [pl.BlockSpec((tile, D_MODEL), lambda b, s: (b * n_t + s, 0))]
    kern = functools.partial(_rg_kernel, tile=tile, pipelined=pipelined)
    o, nbuf, hlast = pl.pallas_call(
        kern,
        out_shape=(jax.ShapeDtypeStruct((n_seq * seq_len, D_MODEL), BF16),
                   jax.ShapeDtypeStruct((n_seq, CONV_WIDTH - 1, LRU_WIDTH), F32),
                   jax.ShapeDtypeStruct((n_seq, 1, LRU_WIDTH), F32)),
        grid=(n_seq, n_steps),
        in_specs=x_specs + [
            pl.BlockSpec((1, CONV_WIDTH - 1, LRU_WIDTH), lambda b, s: (b, 0, 0)),
            pl.BlockSpec((1, 1, LRU_WIDTH), lambda b, s: (b, 0, 0)),
            _resident((tile, tile)), _resident((tile, tile)),
            _resident((D_MODEL, 2 * LRU_WIDTH), layer),
            _resident((CONV_WIDTH, LRU_WIDTH)),
            _resident((1, LRU_WIDTH)),
            _resident((LRU_BLOCKS, LRU_BLOCK_W, 2 * LRU_BLOCK_W), layer),
            _resident((LRU_BLOCKS, 1, 2 * LRU_BLOCK_W)),
            _resident((1, LRU_WIDTH))],
        out_specs=(pl.BlockSpec((tiles_per_step * tile, D_MODEL), lambda b, s: (b * n_steps + s, 0)),
                   pl.BlockSpec((1, CONV_WIDTH - 1, LRU_WIDTH), lambda b, s: (b, 0, 0)),
                   pl.BlockSpec((1, 1, LRU_WIDTH), lambda b, s: (b, 0, 0))),
        scratch_shapes=[pltpu.VMEM((tile, 2 * LRU_WIDTH), F32),
                        pltpu.VMEM((tile, 2 * LRU_WIDTH), F32),
                        pltpu.VMEM((tile + 8 * (CONV_WIDTH - 1), LRU_WIDTH), F32),
                        pltpu.VMEM((tile, LRU_WIDTH), F32),
                        pltpu.VMEM((tile, LRU_WIDTH), F32),
                        pltpu.VMEM((tile, LRU_WIDTH), F32),
                        pltpu.VMEM((8, LRU_WIDTH), F32),
                        pltpu.VMEM((1, LRU_WIDTH), F32)],
        compiler_params=_params(("parallel", "arbitrary")),
        name="rg_mixer",
    )(*([x] * len(x_specs)), conv_buf, h0, perm, perm.T, w_in, conv_w, conv_b, gate_w, gate_b, lam)
    return o, nbuf, hlast[:, 0, :]


def _swa_proj_kernel(x_ref, w_ref, q_ref, k_ref, v_ref, kb_ref, vb_ref):
    xb = x_ref[...].astype(BF16)
    qkv = jnp.dot(xb, w_ref[...], preferred_element_type=F32)
    kvd = SWA_KV_HEADS * HEAD_DIM
    q_ref[...] = qkv[:, :D_MODEL].astype(BF16)
    k = qkv[:, D_MODEL:D_MODEL + kvd]
    v = qkv[:, D_MODEL + kvd:]
    k_ref[...] = k
    v_ref[...] = v
    kb_ref[...] = k.astype(BF16)
    vb_ref[...] = v.astype(BF16)


def _swa_proj_call(x, w):
    n = x.shape[0]
    kvd = SWA_KV_HEADS * HEAD_DIM
    row = lambda i: (i, 0)
    return pl.pallas_call(
        _swa_proj_kernel,
        out_shape=(jax.ShapeDtypeStruct((n, D_MODEL), BF16),
                   jax.ShapeDtypeStruct((n, kvd), F32),
                   jax.ShapeDtypeStruct((n, kvd), F32),
                   jax.ShapeDtypeStruct((n, kvd), BF16),
                   jax.ShapeDtypeStruct((n, kvd), BF16)),
        grid=(n // TOKEN_TILE,),
        in_specs=[pl.BlockSpec((TOKEN_TILE, D_MODEL), row), _resident((D_MODEL, D_MODEL + 2 * kvd))],
        out_specs=(pl.BlockSpec((TOKEN_TILE, D_MODEL), row),
                   pl.BlockSpec((TOKEN_TILE, kvd), row), pl.BlockSpec((TOKEN_TILE, kvd), row),
                   pl.BlockSpec((TOKEN_TILE, kvd), row), pl.BlockSpec((TOKEN_TILE, kvd), row)),
        compiler_params=_params(("parallel",)),
        name="swa_proj",
    )(x, w)


def _swa_attn_kernel(q_ref, k_ref, v_ref, bias_ref, sink_ref, o_ref, s_even, s_odd,
                     *, tile, n_invalid):
    ti = pl.program_id(1)
    cols = SWA_GROUP * HEADS_PER_SLAB * CHUNK
    half_cols = cols // HEADS_PER_SLAB
    lane_q = lax.broadcasted_iota(jnp.int32, (CHUNK, LANES), 1)
    low_q = lane_q < HEAD_DIM
    key_idx = lax.broadcasted_iota(jnp.int32, (SWA_KEYS, cols), 0)
    n_kv_slabs = SWA_KV_HEADS // HEADS_PER_SLAB
    units = [(c, t) for c in range(tile // CHUNK) for t in range(n_kv_slabs)]
    s_bufs = (s_even, s_odd)

    def window_start(c):
        return pl.multiple_of(ti * tile + c * CHUNK, CHUNK)

    def scores(u):
        c, t = units[u]
        row0 = window_start(c)
        kw = k_ref[0, pl.ds(row0, SWA_KEYS), t * LANES:(t + 1) * LANES]
        slabs = [q_ref[0, c * CHUNK:(c + 1) * CHUNK,
                       (SWA_GROUP * t + g) * LANES:(SWA_GROUP * t + g + 1) * LANES]
                 for g in range(SWA_GROUP)]
        zero = jnp.zeros_like(slabs[0])
        q_stack = jnp.concatenate([jnp.where(low_q, s_, zero) for s_ in slabs]
                                  + [jnp.where(low_q, zero, s_) for s_ in slabs], axis=0)
        s = lax.dot_general(kw, q_stack, (((1,), (1,)), ((), ())), preferred_element_type=F32)
        s = s + bias_ref[t]
        if n_invalid > 0 and c * CHUNK < n_invalid:
            s = jnp.where(key_idx + row0 >= n_invalid, s, -jnp.inf)
        s_bufs[u % 2][...] = s

    def attend(u):
        c, t = units[u]
        s = s_bufs[u % 2][...]
        vw = v_ref[0, pl.ds(window_start(c), SWA_KEYS), t * LANES:(t + 1) * LANES]
        sink = sink_ref[t]
        m = jnp.maximum(jnp.max(s, axis=0, keepdims=True), sink)
        p = jnp.exp(s - m)
        denom = jnp.sum(p, axis=0, keepdims=True) + jnp.exp(sink - m)
        out_t = lax.dot_general(vw, p.astype(BF16), (((0,), (0,)), ((), ())),
                                preferred_element_type=F32) / denom
        pair = jnp.concatenate([out_t[:HEAD_DIM, :half_cols], out_t[HEAD_DIM:, half_cols:]], axis=0)
        pair = pair.T.astype(BF16)
        for g in range(SWA_GROUP):
            s_idx = SWA_GROUP * t + g
            o_ref[0, c * CHUNK:(c + 1) * CHUNK, s_idx * LANES:(s_idx + 1) * LANES] = (
                pair[g * CHUNK:(g + 1) * CHUNK, :])

    scores(0)
    for u in range(len(units)):
        if u + 1 < len(units):
            scores(u + 1)
        attend(u)


def _swa_attn_call(q, k_win, v_win, bias, sink, *, n_seq, seq_len, tile, n_invalid):
    kvd = SWA_KV_HEADS * HEAD_DIM
    n_keys = k_win.shape[1]
    cols = SWA_GROUP * HEADS_PER_SLAB * CHUNK
    n_kv_slabs = SWA_KV_HEADS // HEADS_PER_SLAB
    kern = functools.partial(_swa_attn_kernel, tile=tile, n_invalid=n_invalid)
    return pl.pallas_call(
        kern,
        out_shape=jax.ShapeDtypeStruct((n_seq, seq_len, D_MODEL), BF16),
        grid=(n_seq, seq_len // tile),
        in_specs=[pl.BlockSpec((1, tile, D_MODEL), lambda b, t: (b, t, 0)),
                  pl.BlockSpec((1, n_keys, kvd), lambda b, t: (b, 0, 0)),
                  pl.BlockSpec((1, n_keys, kvd), lambda b, t: (b, 0, 0)),
                  _resident((n_kv_slabs, SWA_KEYS, cols)),
                  _resident((n_kv_slabs, 1, cols))],
        out_specs=pl.BlockSpec((1, tile, D_MODEL), lambda b, t: (b, t, 0)),
        scratch_shapes=[pltpu.VMEM((SWA_KEYS, cols), F32), pltpu.VMEM((SWA_KEYS, cols), F32)],
        compiler_params=_params(("parallel", "arbitrary")),
        name="swa_attn",
    )(q, k_win, v_win, bias, sink)


def _t5_bucket(rel):
    half = N_BUCKETS // 2
    max_exact = half // 2
    n = jnp.abs(rel)
    n_f = jnp.maximum(n, 1).astype(jnp.float32)
    large = max_exact + (jnp.log(n_f / max_exact) / math.log(MAX_DISTANCE / max_exact)
                         * (half - max_exact)).astype(jnp.int32)
    large = jnp.minimum(large, half - 1)
    return jnp.where(rel > 0, half, 0) + jnp.where(n < max_exact, n, large)


_SWA_HEAD_ORDER = [8 * t + 4 * p + g for t in range(2) for g in range(4) for p in range(2)]


def _swa_row_tables(table, sinks):
    rel = jnp.arange(SWA_KEYS)[None, :] - WINDOW - jnp.arange(CHUNK)[:, None]
    bucket = _t5_bucket(rel)
    tbl = table.astype(F32)
    hit = bucket[None, None] == jnp.arange(N_BUCKETS)[:, None, None, None]
    bias = jnp.sum(jnp.where(hit, tbl[:, :, None, None], 0.0), axis=0)
    n_kv_slabs = SWA_KV_HEADS // HEADS_PER_SLAB
    cols = SWA_GROUP * HEADS_PER_SLAB * CHUNK
    bias = bias.reshape(n_kv_slabs, cols, SWA_KEYS).transpose(0, 2, 1)
    sink = jnp.repeat(sinks.astype(F32), CHUNK).reshape(n_kv_slabs, 1, cols)
    return bias, sink


def _fox_proj_kernel(x_ref, w_ref, wt_ref, bf_ref, q_ref, k_ref, v_ref, kb_ref, vt_ref, lf_ref,
                     *, time_minor):
    xb = x_ref[...].astype(BF16)
    proj = jnp.dot(xb, w_ref[...], preferred_element_type=F32)
    proj_t = lax.dot_general(wt_ref[...], xb, (((1,), (1,)), ((), ())), preferred_element_type=F32)
    if time_minor:
        k = proj[:, :D_MODEL]
        z = proj[:, D_MODEL:]
        vt = proj_t[D_MODEL:, :]
        k_ref[0] = k.T
        v_ref[0] = vt
    else:
        k = proj[:, D_MODEL:2 * D_MODEL]
        z = proj[:, 3 * D_MODEL:]
        vt = proj_t
        k_ref[...] = k.reshape(TOKEN_TILE, N_HEADS, HEAD_DIM)
        v_ref[...] = proj[:, 2 * D_MODEL:3 * D_MODEL].reshape(TOKEN_TILE, N_HEADS, HEAD_DIM)
    for s in range(N_SLABS):
        cols = slice(s * LANES, (s + 1) * LANES)
        q_ref[s] = (proj_t[cols, :] if time_minor else proj[:, cols]).astype(BF16)
        kb_ref[s] = k[:, cols].astype(BF16)
        for j in range(TOKEN_TILE // FOX_TK):
            vt_ref[s, j] = vt[cols, j * FOX_TK:(j + 1) * FOX_TK].astype(BF16)
    lf_ref[...] = -_softplus(-(z + bf_ref[...]))


def _fox_proj_call(x, w, w_t, b_f, *, seq_len, time_minor):
    n = x.shape[0]
    row = lambda i: (i, 0)
    wide = pl.BlockSpec((TOKEN_TILE, D_MODEL), row)
    slabs = pl.BlockSpec((N_SLABS, TOKEN_TILE, LANES), lambda i: (0, i, 0))
    kb_per_tile = TOKEN_TILE // FOX_TK
    if time_minor:
        t_per_seq = seq_len // TOKEN_TILE
        kv_shape = jax.ShapeDtypeStruct((n // seq_len, D_MODEL, seq_len), F32)
        kv_spec = pl.BlockSpec((1, D_MODEL, TOKEN_TILE), lambda i: (i // t_per_seq, 0, i % t_per_seq))
        q_shape = jax.ShapeDtypeStruct((N_SLABS, LANES, n), BF16)
        q_spec = pl.BlockSpec((N_SLABS, LANES, TOKEN_TILE), lambda i: (0, 0, i))
    else:
        kv_shape = jax.ShapeDtypeStruct((n, N_HEADS, HEAD_DIM), F32)
        kv_spec = pl.BlockSpec((TOKEN_TILE, N_HEADS, HEAD_DIM), lambda i: (i, 0, 0))
        q_shape = jax.ShapeDtypeStruct((N_SLABS, n, LANES), BF16)
        q_spec = slabs
    return pl.pallas_call(
        functools.partial(_fox_proj_kernel, time_minor=time_minor),
        out_shape=(q_shape, kv_shape, kv_shape,
                   jax.ShapeDtypeStruct((N_SLABS, n, LANES), BF16),
                   jax.ShapeDtypeStruct((N_SLABS, n // FOX_TK, LANES, FOX_TK), BF16),
                   jax.ShapeDtypeStruct((n, LANES), F32)),
        grid=(n // TOKEN_TILE,),
        in_specs=[wide, _resident(w.shape), _resident(w_t.shape), _resident((1, LANES))],
        out_specs=(q_spec, kv_spec, kv_spec, slabs,
                   pl.BlockSpec((N_SLABS, kb_per_tile, LANES, FOX_TK), lambda i: (0, i, 0, 0)),
                   pl.BlockSpec((TOKEN_TILE, LANES), row)),
        compiler_params=_params(("parallel",)),
        name="fox_proj",
    )(x, w, w_t, b_f)


def _lane_split3(x, lane):
    hi = x.astype(BF16)
    r1 = x - hi.astype(F32)
    mid = r1.astype(BF16)
    lo = (r1 - mid.astype(F32)).astype(BF16)
    return jnp.where(lane < N_HEADS, hi, jnp.where(lane < 2 * N_HEADS, mid, lo))


def _fox_bias_kernel(lf_ref, tri_ref, place_ref, const_ref, bq_ref, bk_ref, carry_scr):
    t = pl.program_id(1)

    @pl.when(t == 0)
    def _():
        carry_scr[...] = jnp.zeros_like(carry_scr)

    lane = lax.broadcasted_iota(jnp.int32, (BIAS_TILE, LANES), 1)
    sums = carry_scr[...] + jnp.dot(tri_ref[...], _lane_split3(lf_ref[0], lane),
                                    preferred_element_type=F32)
    carry_scr[...] = sums[BIAS_TILE - 1:BIAS_TILE, :]
    c = sums + pltpu.roll(sums, LANES - N_HEADS, 1) + pltpu.roll(sums, LANES - 2 * N_HEADS, 1)
    c = jnp.where(lane < N_HEADS, c, 0.0)
    c = c + pltpu.roll(c, N_HEADS, 1) + pltpu.roll(c, 2 * N_HEADS, 1)
    both = const_ref[...] + jnp.dot(_lane_split3(c, lane), place_ref[...],
                                    preferred_element_type=F32)
    bq_ref[0] = both[:, :LANES].astype(BF16)
    bk_ref[0] = both[:, LANES:].astype(BF16)


def _fox_bias_tables():
    place = [[0.0] * (2 * LANES) for _ in range(LANES)]
    const = [0.0] * (2 * LANES)
    for i in range(3):
        for h in range(N_HEADS):
            const[N_HEADS * i + h] = 1.0
            place[N_HEADS * i + h][N_HEADS * (3 + i) + h] = 1.0
            place[N_HEADS * i + h][LANES + N_HEADS * i + h] = -1.0
            const[LANES + N_HEADS * (3 + i) + h] = 1.0
    return jnp.array(place, BF16), jnp.array([const], F32)


def _fox_bias_call(logf):
    n_seq, n_rows, _ = logf.shape
    place, const = _fox_bias_tables()
    tri = jnp.tril(jnp.ones((BIAS_TILE, BIAS_TILE), BF16))
    blk = pl.BlockSpec((1, BIAS_TILE, LANES), lambda b, t: (b, t, 0))
    return pl.pallas_call(
        _fox_bias_kernel,
        out_shape=(jax.ShapeDtypeStruct((n_seq, n_rows, LANES), BF16),
                   jax.ShapeDtypeStruct((n_seq, n_rows, LANES), BF16)),
        grid=(n_seq, n_rows // BIAS_TILE),
        in_specs=[blk, _resident((BIAS_TILE, BIAS_TILE)), _resident((LANES, 2 * LANES)),
                  _resident((1, 2 * LANES))],
        out_specs=(blk, blk),
        scratch_shapes=[pltpu.VMEM((1, LANES), F32)],
        compiler_params=_params(("parallel", "arbitrary")),
        name="fox_bias",
    )(logf, tri, place, const)


def _fox_attn_core(q_ref, bq_ref, bk_ref, o_ref, scratch, *, tq, tk, first_q, q_time_minor, run_blocks):
    qaug_scr, m_scr, acc_scr, s_even, s_odd, smax_even, smax_odd = scratch

    feat = lax.broadcasted_iota(jnp.int32, (LANES, tq) if q_time_minor else (tq, LANES),
                                0 if q_time_minor else 1)
    low_feat = feat < HEAD_DIM
    bias_feat = feat < 6 * N_HEADS
    bq = bq_ref[0]
    if q_time_minor:
        bq = bq.astype(F32).T.astype(BF16)
    zero = jnp.zeros_like(bq)
    for head in range(N_HEADS):
        q_slab = q_ref[head // HEADS_PER_SLAB]
        own_half = low_feat if head % HEADS_PER_SLAB == 0 else jnp.logical_not(low_feat)
        sel = jnp.logical_and((feat & (N_HEADS - 1)) == head, bias_feat)
        qaug_scr[head] = jnp.concatenate([jnp.where(own_half, q_slab, zero), jnp.where(sel, bq, zero)],
                                         axis=0 if q_time_minor else 1)
    q_contract = 0 if q_time_minor else 1
    m_scr[...] = jnp.full_like(m_scr, -jnp.inf)
    acc_scr[...] = jnp.zeros_like(acc_scr)
    slots = ((s_even, smax_even), (s_odd, smax_odd))

    def key_block(kb, masked, k_of, vt_of, n_keys=tk):
        key0 = pl.multiple_of(kb * tk, tk)
        bk_blk = bk_ref[0, pl.ds(key0, n_keys), :]

        def scores(hp):
            s_buf, smax_buf = slots[hp % 2]
            k_aug = jnp.concatenate([k_of(hp), bk_blk], axis=1)
            for par in range(HEADS_PER_SLAB):
                s = lax.dot_general(k_aug, qaug_scr[hp * HEADS_PER_SLAB + par],
                                    (((1,), (q_contract,)), ((), ())),
                                    preferred_element_type=F32)
                if masked:
                    k_pos = key0 + lax.broadcasted_iota(jnp.int32, (n_keys, tq), 0)
                    q_pos = first_q + lax.broadcasted_iota(jnp.int32, (n_keys, tq), 1)
                    s = jnp.where(k_pos <= q_pos, s, -jnp.inf)
                s_buf[par, 0:n_keys, :] = s
                smax_buf[par] = jnp.max(s, axis=0, keepdims=True)

        def absorb(hp):
            s_buf, smax_buf = slots[hp % 2]
            vt = vt_of(hp)
            for par in range(HEADS_PER_SLAB):
                head = hp * HEADS_PER_SLAB + par
                m = m_scr[head]
                m_new = jnp.maximum(m, smax_buf[par])
                p = jnp.exp(s_buf[par, 0:n_keys, :] - m_new).astype(BF16)
                vt_aug = jnp.concatenate([vt[par * HEAD_DIM:(par + 1) * HEAD_DIM, :],
                                          jnp.ones((FOX_ONES_ROWS, n_keys), BF16)], axis=0)
                acc_scr[head] = (jnp.exp(m - m_new) * acc_scr[head]
                                 + jnp.dot(vt_aug, p, preferred_element_type=F32))
                m_scr[head] = m_new

        scores(0)
        for hp in range(N_SLABS):
            if hp + 1 < N_SLABS:
                scores(hp + 1)
            absorb(hp)

    run_blocks(key_block)

    for hp in range(N_SLABS):
        halves = []
        for par in range(HEADS_PER_SLAB):
            acc = acc_scr[hp * HEADS_PER_SLAB + par]
            halves.append(acc[:HEAD_DIM] / acc[HEAD_DIM:HEAD_DIM + 1])
        o_ref[hp] = jnp.concatenate(halves, axis=0).T.astype(BF16)


def _fox_attn_scratch(tq, tk, q_time_minor):
    return [pltpu.VMEM((N_HEADS, 2 * LANES, tq) if q_time_minor else (N_HEADS, tq, 2 * LANES), BF16),
            pltpu.VMEM((N_HEADS, 1, tq), F32),
            pltpu.VMEM((N_HEADS, HEAD_DIM + FOX_ONES_ROWS, tq), F32),
            pltpu.VMEM((HEADS_PER_SLAB, tk, tq), F32),
            pltpu.VMEM((HEADS_PER_SLAB, tk, tq), F32),
            pltpu.VMEM((HEADS_PER_SLAB, 1, tq), F32),
            pltpu.VMEM((HEADS_PER_SLAB, 1, tq), F32)]


def _fox_attn_kernel(q_ref, bq_ref, k_ref, bk_ref, vt_ref, o_ref, *scratch, tq, tk):
    first_q = pl.program_id(1) * tq
    n_full = first_q // tk

    def run_blocks(key_block):
        def block(kb, masked):
            r0 = pl.multiple_of(kb * tk, tk)
            key_block(kb, masked, lambda hp: k_ref[hp, pl.ds(r0, tk), :], lambda hp: vt_ref[hp, kb])

        def full_block(kb, carry):
            block(kb, False)
            return carry

        lax.fori_loop(0, n_full, full_block, 0)
        block(n_full, True)

    _fox_attn_core(q_ref, bq_ref, bk_ref, o_ref, scratch, tq=tq, tk=tk, first_q=first_q,
                   q_time_minor=True, run_blocks=run_blocks)


def _fox_attn_call(q, bq, k, bk, vt, *, n_seq, seq_len, tq, tk):
    assert tq <= tk and tk % tq == 0 and seq_len % tk == 0
    n_qt = seq_len // tq
    n_kb = seq_len // tk
    return pl.pallas_call(
        functools.partial(_fox_attn_kernel, tq=tq, tk=tk),
        out_shape=jax.ShapeDtypeStruct((N_SLABS, n_seq * seq_len, LANES), BF16),
        grid=(n_seq, n_qt),
        in_specs=[pl.BlockSpec((N_SLABS, LANES, tq), lambda b, i: (0, 0, b * n_qt + i)),
                  pl.BlockSpec((1, tq, LANES), lambda b, i: (b, i, 0)),
                  pl.BlockSpec((N_SLABS, seq_len, LANES), lambda b, i: (0, b, 0)),
                  pl.BlockSpec((1, seq_len, LANES), lambda b, i: (b, 0, 0)),
                  pl.BlockSpec((N_SLABS, n_kb, LANES, tk), lambda b, i: (0, b, 0, 0))],
        out_specs=pl.BlockSpec((N_SLABS, tq, LANES), lambda b, i: (0, b * n_qt + i, 0)),
        scratch_shapes=_fox_attn_scratch(tq, tk, True),
        compiler_params=_params(("parallel", "arbitrary")),
        name="fox_attn",
    )(q, bq, k, bk, vt)


def _fox_step_attn_kernel(q_ref, bq_ref, kold_ref, knew_ref, bk_ref, vtold_ref, vtnew_ref, o_ref,
                          *scratch, n_new, n_new_pad, n_old, tk):
    def run_blocks(key_block):
        for kb in range(n_old // tk):
            rows = slice(kb * tk, (kb + 1) * tk)
            key_block(kb, False, lambda hp, rows=rows: kold_ref[0, hp, :, rows].T.astype(BF16),
                      lambda hp, rows=rows: vtold_ref[0, hp, :, rows].astype(BF16))
        key_block(n_old // tk, True, lambda hp: knew_ref[hp], lambda hp: vtnew_ref[hp, 0],
                  n_keys=n_new_pad)

    _fox_attn_core(q_ref, bq_ref, bk_ref, o_ref, scratch, tq=n_new, tk=tk, first_q=n_old,
                   q_time_minor=False, run_blocks=run_blocks)


def _fox_step_attn_call(q, bq, kt_old, k_new, bk, vt_old, vt_new, *, n_seq, n_new, n_old, tk):
    n_new_pad = vt_new.shape[-1]
    assert n_new <= n_new_pad <= tk and n_old % tk == 0 and n_old % n_new == 0
    return pl.pallas_call(
        functools.partial(_fox_step_attn_kernel, n_new=n_new, n_new_pad=n_new_pad, n_old=n_old, tk=tk),
        out_shape=jax.ShapeDtypeStruct((N_SLABS, n_seq * n_new, LANES), BF16),
        grid=(n_seq,),
        in_specs=[pl.BlockSpec((N_SLABS, n_new, LANES), lambda b: (0, b, 0)),
                  pl.BlockSpec((1, n_new, LANES), lambda b: (b, n_old // n_new, 0)),
                  pl.BlockSpec((1, N_SLABS, LANES, n_old), lambda b: (b, 0, 0, 0)),
                  pl.BlockSpec((N_SLABS, n_new_pad, LANES), lambda b: (0, b, 0)),
                  pl.BlockSpec((1, n_old + tk, LANES), lambda b: (b, 0, 0)),
                  pl.BlockSpec((1, N_SLABS, LANES, n_old), lambda b: (b, 0, 0, 0)),
                  pl.BlockSpec((N_SLABS, 1, LANES, n_new_pad), lambda b: (0, b, 0, 0))],
        out_specs=pl.BlockSpec((N_SLABS, n_new, LANES), lambda b: (0, b, 0)),
        scratch_shapes=_fox_attn_scratch(n_new, tk, False),
        compiler_params=_params(("parallel",)),
        name="fox_step_attn",
    )(q, bq, kt_old, k_new, bk, vt_old, vt_new)


def _rg_layer(xs, conv_state, h_state, w_in, gate_w, small, layer):
    x_p, x_s = xs
    o_p, nb_p, h_p = _rg_call(
        x_p, jnp.zeros((BATCH, CONV_WIDTH - 1, LRU_WIDTH), F32), jnp.zeros((BATCH, 1, LRU_WIDTH), F32),
        w_in, gate_w, small, layer, n_seq=BATCH, seq_len=SEQ, tile=RG_TILE)
    o_s, nb_s, h_s = _rg_call(
        x_s, conv_state, h_state.reshape(DEC_BATCH, 1, LRU_WIDTH),
        w_in, gate_w, small, layer, n_seq=DEC_BATCH, seq_len=DEC_SEQ, tile=DEC_SEQ)
    return (o_p, o_s), (nb_p, nb_s, h_p, h_s)


def _swa_layer(xs, k_cache, v_cache, w_qkv, sinks, table):
    kvd = SWA_KV_HEADS * HEAD_DIM
    order = jnp.array(_SWA_HEAD_ORDER)
    w_q = (w_qkv[:, :D_MODEL] * ATTN_SCALE).reshape(D_MODEL, N_HEADS, HEAD_DIM)[:, order]
    w = jnp.concatenate([w_q.reshape(D_MODEL, D_MODEL), w_qkv[:, D_MODEL:]], axis=1).astype(BF16)
    bias, sink = _swa_row_tables(table, sinks)

    q_p, k_p, v_p, kb_p, vb_p = _swa_proj_call(xs[0], w)
    pad = ((0, 0), (WINDOW, 0), (0, 0))
    o_p = _swa_attn_call(q_p.reshape(BATCH, SEQ, D_MODEL),
                         jnp.pad(kb_p.reshape(BATCH, SEQ, kvd), pad),
                         jnp.pad(vb_p.reshape(BATCH, SEQ, kvd), pad),
                         bias, sink, n_seq=BATCH, seq_len=SEQ, tile=SWA_TILE, n_invalid=WINDOW)

    q_s, k_s, v_s, kb_s, vb_s = _swa_proj_call(xs[1], w)
    kc = k_cache.reshape(DEC_BATCH, WINDOW, kvd)
    vc = v_cache.reshape(DEC_BATCH, WINDOW, kvd)
    o_s = _swa_attn_call(q_s.reshape(DEC_BATCH, DEC_SEQ, D_MODEL),
                         jnp.concatenate([kc.astype(BF16), kb_s.reshape(DEC_BATCH, DEC_SEQ, kvd)], axis=1),
                         jnp.concatenate([vc.astype(BF16), vb_s.reshape(DEC_BATCH, DEC_SEQ, kvd)], axis=1),
                         bias, sink, n_seq=DEC_BATCH, seq_len=DEC_SEQ, tile=DEC_SEQ, n_invalid=0)

    def tails(new_p, new_s, cache):
        tail_p = new_p.reshape(BATCH, SEQ, kvd)[:, SEQ - WINDOW:]
        tail_s = jnp.concatenate([cache.reshape(DEC_BATCH, WINDOW, kvd)[:, DEC_SEQ:],
                                  new_s.reshape(DEC_BATCH, DEC_SEQ, kvd)], axis=1)
        return (tail_p.reshape(BATCH, WINDOW, SWA_KV_HEADS, HEAD_DIM),
                tail_s.reshape(DEC_BATCH, WINDOW, SWA_KV_HEADS, HEAD_DIM))

    k_tp, k_ts = tails(k_p, k_s, k_cache)
    v_tp, v_ts = tails(v_p, v_s, v_cache)
    return ((o_p.reshape(N_PROMPT, D_MODEL), o_s.reshape(N_SAMPLE, D_MODEL)),
            (k_tp, k_ts, v_tp, v_ts))


def _fox_layer(xs, k_cache, v_cache, logf_cache, w_in, b_f):
    hd = N_HEADS * HEAD_DIM
    def thrice(a):
        rep = jnp.concatenate([a, a, a], axis=-1)
        return jnp.pad(rep, [(0, 0)] * (a.ndim - 1) + [(0, LANES - 3 * N_HEADS)])

    w_q = w_in[:, :hd] * ATTN_SCALE
    w_f = thrice(w_in[:, 3 * hd:])
    w_kf = jnp.concatenate([w_in[:, hd:2 * hd], w_f], axis=1).astype(BF16)
    w = jnp.concatenate([w_q, w_in[:, hd:3 * hd], w_f], axis=1).astype(BF16)
    bf = thrice(b_f.astype(F32)).reshape(1, LANES)
    w_vt = w_in[:, 2 * hd:3 * hd].T.astype(BF16)
    w_qvt = jnp.concatenate([w_q.T.astype(BF16), w_vt], axis=0)

    q_p, kt_p, vt32_p, kb_p, vt_p, lf_p = _fox_proj_call(xs[0], w_kf, w_qvt, bf, seq_len=SEQ,
                                                         time_minor=True)
    lf_p = lf_p.reshape(BATCH, SEQ, LANES)
    bq_p, bk_p = _fox_bias_call(lf_p)
    o_p = _fox_attn_call(q_p, bq_p, kb_p, bk_p, vt_p, n_seq=BATCH, seq_len=SEQ, tq=FOX_TQ, tk=FOX_TK)

    q_s, k_s, v_s, kb_s, vt_s, lf_s = _fox_proj_call(xs[1], w, w_vt, bf, seq_len=DEC_SEQ,
                                                     time_minor=False)
    lf_s = lf_s.reshape(DEC_BATCH, DEC_SEQ, LANES)
    tail = FOX_TK - DEC_SEQ
    bq_s, bk_s = _fox_bias_call(jnp.pad(jnp.concatenate([thrice(logf_cache.astype(F32)), lf_s], axis=1),
                                        ((0, 0), (0, tail), (0, 0))))

    def time_minor(cache):
        return cache.astype(F32).reshape(DEC_BATCH, PAST_LEN, N_SLABS, LANES).transpose(0, 2, 3, 1)

    kt_old = time_minor(k_cache)
    vt_old = time_minor(v_cache)
    per_blk = FOX_TK // DEC_SEQ
    vt_new = vt_s.reshape(N_SLABS, N_SAMPLE // FOX_TK, LANES, per_blk, DEC_SEQ)
    vt_new = vt_new.transpose(0, 1, 3, 2, 4).reshape(N_SLABS, DEC_BATCH, LANES, DEC_SEQ)
    new_pad = FOX_STEP_KEYS - DEC_SEQ
    vt_new = jnp.pad(vt_new, ((0, 0), (0, 0), (0, 0), (0, new_pad)))
    k_new = jnp.pad(kb_s.reshape(N_SLABS, DEC_BATCH, DEC_SEQ, LANES), ((0, 0), (0, 0), (0, new_pad), (0, 0)))
    k_new = k_new.reshape(N_SLABS, DEC_BATCH * FOX_STEP_KEYS, LANES)
    o_s = _fox_step_attn_call(q_s, bq_s, kt_old, k_new, bk_s, vt_old, vt_new, n_seq=DEC_BATCH,
                              n_new=DEC_SEQ, n_old=PAST_LEN, tk=FOX_TK)

    def heads_last(a):
        return a.reshape(BATCH, N_HEADS, HEAD_DIM, SEQ).transpose(0, 3, 1, 2)

    outs = (heads_last(kt_p), k_s.reshape(DEC_BATCH, DEC_SEQ, N_HEADS, HEAD_DIM),
            heads_last(vt32_p), v_s.reshape(DEC_BATCH, DEC_SEQ, N_HEADS, HEAD_DIM),
            lf_p[..., :N_HEADS], lf_s[..., :N_HEADS])
    return (o_p, o_s), outs


def kernel(x_prompt, x_sample, state_rg_conv, state_rg_h, cache_swa_k, cache_swa_v, cache_fox_k, cache_fox_v, cache_fox_logf, ln_gain, ln_bias, ffn_w_up, ffn_w_down, rg_w_in, rg_conv_w, rg_conv_b, rg_gate_w, rg_gate_b, rg_lambda, rg_w_out, swa_w_qkv, swa_sinks, swa_w_out, rel_bias_table, fox_w_in, fox_b_f, fox_w_out):
    xs = (x_prompt.reshape(N_PROMPT, D_MODEL), x_sample.reshape(N_SAMPLE, D_MODEL))
    w_up = ffn_w_up.astype(BF16)
    w_down = ffn_w_down.astype(BF16)
    rg_in = rg_w_in.astype(BF16)
    rg_gate = rg_gate_w.astype(BF16)
    swa_order = jnp.array(_SWA_HEAD_ORDER)
    n_swa = swa_w_out.shape[0]
    w_out = {0: rg_w_out.astype(BF16),
             1: swa_w_out.reshape(n_swa, N_HEADS, HEAD_DIM, D_MODEL)[:, swa_order]
                         .reshape(n_swa, D_MODEL, D_MODEL).astype(BF16),
             2: fox_w_out.astype(BF16)}
    rg_out, swa_out, fox_out = [], [], []
    for i in range(DEPTH):
        kind, j = i % 3, i // 3
        if kind == 0:
            small = (rg_conv_w[j], rg_conv_b[j].reshape(1, LRU_WIDTH),
                     rg_gate_b[j].reshape(LRU_BLOCKS, 1, 2 * LRU_BLOCK_W), rg_lambda[j].reshape(1, LRU_WIDTH))
            os_, extra = _rg_layer(xs, state_rg_conv[j], state_rg_h[j], rg_in, rg_gate, small, j)
            rg_out.append(extra)
        elif kind == 1:
            os_, extra = _swa_layer(xs, cache_swa_k[j], cache_swa_v[j], swa_w_qkv[j], swa_sinks[j],
                                    rel_bias_table)
            swa_out.append(extra)
        else:
            os_, extra = _fox_layer(xs, cache_fox_k[j], cache_fox_v[j], cache_fox_logf[j], fox_w_in[j],
                                    fox_b_f[j])
            fox_out.append(extra)
        xs = tuple(_post_call(x, o, w_out[kind], j, w_up, w_down, ln_gain, ln_bias, i)
                   for x, o in zip(xs, os_))

    def stack(items, idx):
        return jnp.stack([it[idx] for it in items])

    return (xs[0].reshape(BATCH, SEQ, D_MODEL), xs[1].reshape(DEC_BATCH, DEC_SEQ, D_MODEL),
            stack(rg_out, 0), stack(rg_out, 1), stack(rg_out, 2), stack(rg_out, 3),
            stack(swa_out, 0), stack(swa_out, 1), stack(swa_out, 2), stack(swa_out, 3),
            stack(fox_out, 0), stack(fox_out, 1), stack(fox_out, 2), stack(fox_out, 3),
            stack(fox_out, 4), stack(fox_out, 5))
```
